```python
import jax, jax.numpy as jnp
from jax import lax
import numpy as np

D_MODEL = 2048
BATCH = 4
SEQ = 2048
DEPTH = 2
DEC_BATCH = 16
DEC_SEQ = 32
PAST_LEN = 4096

CHUNK = 64
D_MIX = D_MODEL
D_GROUP = D_MIX // 4
H_RET = 4
DK_RET = D_GROUP // H_RET
DV_RET = D_GROUP // H_RET
ROPE_BASE = 10000.0
H_LRU = 8
BLK_LRU = D_GROUP // H_LRU
LRU_C = 8.0
CONV_W = 4
H_SG = 4
SG_CHUNK = 128
H_DN = 4
DK_DN = D_GROUP // H_DN
DV_DN = D_GROUP // H_DN
D_IN = 12 * D_GROUP + 2 * H_DN
D_FF = 5632
ALPHA = (2.0 * DEPTH) ** 0.25
BETA_INIT = (8.0 * DEPTH) ** -0.25
EPS = 1e-5

kernel_name = 'hybrid_streaming_encoder_step'

F32 = jnp.float32


def layer_norm(x, g, b):
    xf = x.astype(F32)
    mu = jnp.mean(xf, -1, keepdims=True)
    var = jnp.mean(jnp.square(xf - mu), -1, keepdims=True)
    return ((xf - mu) * lax.rsqrt(var + EPS) * g + b).astype(x.dtype)


def swiglu(x, w_in, w_out):
    gate, up = jnp.split(x @ w_in, 2, axis=-1)
    return (jax.nn.silu(gate) * up) @ w_out


def causal_conv(x, buf, w):
    L = x.shape[1]
    xp = jnp.concatenate([buf.astype(x.dtype), x], axis=1)
    out = xp[:, 0:L] * w[0]
    for j in range(1, CONV_W):
        out = out + xp[:, j:j + L] * w[j]
    return out, xp[:, -(CONV_W - 1):]


def rotary(x, pos):
    half = x.shape[-1] // 2
    inv = ROPE_BASE ** (-jnp.arange(half, dtype=F32) / half)
    ang = pos.astype(F32)[:, None] * inv[None, :]
    cos = jnp.cos(ang)[None, :, None, :]
    sin = jnp.sin(ang)[None, :, None, :]
    x1, x2 = x[..., :half], x[..., half:]
    return jnp.concatenate([x1 * cos - x2 * sin, x1 * sin + x2 * cos], axis=-1)


def retention(q, k, v, state0):
    B, L, H, dk = q.shape
    c = min(CHUNK, L)
    n = L // c
    log_g = jnp.log1p(-jnp.exp2(-5.0 - jnp.arange(H, dtype=F32)))
    q = q.reshape(B, n, c, H, dk)
    k = k.reshape(B, n, c, H, dk) * (dk ** -0.5)
    v = v.astype(F32).reshape(B, n, c, H, -1)
    idx = jnp.arange(c, dtype=F32)
    diff = idx[:, None] - idx[None, :]
    dmask = jnp.where(diff >= 0, jnp.exp(log_g[:, None, None] * jnp.maximum(diff, 0.0)), 0.0)
    scores = jnp.einsum('bnihd,bnjhd->bnhij', q, k) * dmask
    o_intra = jnp.einsum('bnhij,bnjhe->bnihe', scores, v)
    k_dec = k * jnp.exp(log_g[None, :] * (c - 1.0 - idx)[:, None])[None, None, :, :, None]
    u = jnp.einsum('bnjhd,bnjhe->nbhde', k_dec, v)
    g_c = jnp.exp(log_g * c)[None, :, None, None]

    def step(s, u_n):
        return g_c * s + u_n, s

    s_final, s_starts = lax.scan(step, state0.astype(F32), u)
    q_dec = q * jnp.exp(log_g[None, :] * (idx + 1.0)[:, None])[None, None, :, :, None]
    o_inter = jnp.einsum('bnihd,nbhde->bnihe', q_dec, s_starts)
    o = (o_intra + o_inter).reshape(B, L, H, -1)
    return o, s_final.astype(state0.dtype)


def group_norm(o, g):
    mu = jnp.mean(o, -1, keepdims=True)
    var = jnp.mean(jnp.square(o - mu), -1, keepdims=True)
    on = (o - mu) * lax.rsqrt(var + EPS)
    return on.reshape(o.shape[0], o.shape[1], -1) * g


def rg_lru(x, w_a, b_a, w_x, b_x, lam, h0):
    x = x.astype(F32)
    B, L, _ = x.shape
    xb = x.reshape(B, L, H_LRU, BLK_LRU)
    r = jax.nn.sigmoid(jnp.einsum('blhi,hij->blhj', xb, w_a).reshape(B, L, -1) + b_a)
    i = jax.nn.sigmoid(jnp.einsum('blhi,hij->blhj', xb, w_x).reshape(B, L, -1) + b_x)
    log_a = -LRU_C * r * jax.nn.softplus(-lam.astype(F32))
    a = jnp.exp(log_a)
    b = jnp.sqrt(-jnp.expm1(2.0 * log_a)) * (i * x)

    def combine(e1, e2):
        a1, b1 = e1
        a2, b2 = e2
        return a1 * a2, a2 * b1 + b2

    a_cum, b_cum = lax.associative_scan(combine, (a, b), axis=1)
    h = a_cum * h0.astype(F32)[:, None, :] + b_cum
    return h, h[:, -1].astype(h0.dtype)


def spatial_gate(u, v, ln_g, ln_b, w_s, b_s):
    B, L, _ = v.shape
    vn = layer_norm(v, ln_g, ln_b)
    c = min(SG_CHUNK, L)
    n = L // c
    pos = jnp.arange(c)
    mask = (pos[None, :] // CHUNK) <= (pos[:, None] // CHUNK)
    w = jnp.where(mask[None], w_s[:, :c, :c], 0.0)
    vh = vn.reshape(B, n, c, H_SG, -1)
    s = jnp.einsum('hij,bnjhe->bnihe', w, vh) + b_s[:, :c].T[None, None, :, :, None]
    return u * s.reshape(B, L, -1), vn


def l2norm(x):
    xf = x.astype(F32)
    return xf * lax.rsqrt(jnp.sum(xf * xf, -1, keepdims=True) + 1e-6)


def gated_delta(q, k, v, g, beta, state0):
    B, L, H, dk = q.shape
    c = min(CHUNK, L)
    n = L // c

    def chunks(t):
        return t.astype(F32).reshape(B, n, c, H, -1).transpose(0, 3, 1, 2, 4)

    q = chunks(q) * (dk ** -0.5)
    k = chunks(k)
    v = chunks(v)
    g = g.reshape(B, n, c, H).transpose(0, 3, 1, 2)
    beta = beta.reshape(B, n, c, H).transpose(0, 3, 1, 2)
    gc = jnp.cumsum(g, axis=-1)
    tri = jnp.tril(jnp.ones((c, c), bool))
    strict = jnp.tril(jnp.ones((c, c), bool), -1)
    decay = jnp.exp(jnp.where(tri, gc[..., :, None] - gc[..., None, :], -jnp.inf))
    k_beta = k * beta[..., None]
    v_beta = v * beta[..., None]
    a_mat = jnp.where(strict, jnp.einsum('bhnid,bhnjd->bhnij', k_beta, k) * decay, 0.0)
    t_mat = a_mat + jnp.eye(c, dtype=F32)
    u = lax.linalg.triangular_solve(t_mat, v_beta, left_side=True, lower=True, unit_diagonal=True)
    w = lax.linalg.triangular_solve(t_mat, k_beta * jnp.exp(gc)[..., None], left_side=True, lower=True,
                                    unit_diagonal=True)
    attn = jnp.einsum('bhnid,bhnjd->bhnij', q, k) * decay
    q_dec = q * jnp.exp(gc)[..., None]
    g_last = gc[..., -1]
    k_dec = k * jnp.exp(g_last[..., None] - gc)[..., None]
    xs = (jnp.moveaxis(u, 2, 0), jnp.moveaxis(w, 2, 0), jnp.moveaxis(attn, 2, 0),
          jnp.moveaxis(q_dec, 2, 0), jnp.moveaxis(k_dec, 2, 0), jnp.moveaxis(g_last, 2, 0))

    def step(s, inp):
        u_n, w_n, attn_n, qd_n, kd_n, gl_n = inp
        v_new = u_n - jnp.einsum('bhid,bhde->bhie', w_n, s)
        o_n = jnp.einsum('bhid,bhde->bhie', qd_n, s) + jnp.einsum('bhij,bhje->bhie', attn_n, v_new)
        s = s * jnp.exp(gl_n)[..., None, None] + jnp.einsum('bhid,bhie->bhde', kd_n, v_new)
        return s, o_n

    s_final, o = lax.scan(step, state0.astype(F32), xs)
    o = o.transpose(1, 0, 3, 2, 4).reshape(B, L, H, -1)
    return o, s_final.astype(state0.dtype)


def gated_rms_norm(o, g, z):
    on = o * lax.rsqrt(jnp.mean(o * o, -1, keepdims=True) + EPS) * g
    return on.reshape(o.shape[0], o.shape[1], -1) * jax.nn.silu(z.astype(F32))


def split_columns(proj):
    sizes = (D_GROUP,) * 4 + (D_GROUP,) * 2 + (D_GROUP,) * 2 + (3 * D_GROUP, D_GROUP, H_DN, H_DN)
    idx = np.cumsum(sizes)[:-1].tolist()
    return jnp.split(proj, idx, axis=-1)


def token_mixers(h, pos, s_ret, s_lru, s_lru_conv, s_dn, s_dn_conv, p):
    B, L, _ = h.shape
    proj = h @ p['w_mix_in']
    a_q, a_k, a_v, a_g, b_y, b_x, c_u, c_v, d_qkv, d_z, d_a, d_b = split_columns(proj)
    q = rotary(a_q.astype(F32).reshape(B, L, H_RET, DK_RET), pos)
    k = rotary(a_k.astype(F32).reshape(B, L, H_RET, DK_RET), pos)
    o_ret, s_ret_new = retention(q, k, a_v.reshape(B, L, H_RET, DV_RET), s_ret)
    out_a = group_norm(o_ret, p['ret_norm_g']) * jax.nn.silu(a_g.astype(F32))
    xc, lru_conv_new = causal_conv(b_x, s_lru, p['lru_conv_w']) if False else causal_conv(b_x, s_lru_conv, p['lru_conv_w'])
    hs, lru_h_new = rg_lru(xc + p['lru_conv_b'], p['lru_w_a'], p['lru_b_a'], p['lru_w_x'], p['lru_b_x'],
                           p['lru_lam'], s_lru)
    out_b = hs * jax.nn.gelu(b_y.astype(F32))
    out_c, sg_v = spatial_gate(jax.nn.gelu(c_u), jax.nn.gelu(c_v), p['sg_ln_g'], p['sg_ln_b'],
                               p['sg_w'], p['sg_b'])
    qkv, dn_conv_new = causal_conv(d_qkv, s_dn_conv, p['dn_conv_w'])
    dq, dk, dv = jnp.split(jax.nn.silu(qkv), 3, axis=-1)
    dq = l2norm(dq.reshape(B, L, H_DN, DK_DN))
    dk = l2norm(dk.reshape(B, L, H_DN, DK_DN))
    g = -jnp.exp(p['dn_a_log'].astype(F32)) * jax.nn.softplus(d_a.astype(F32) + p['dn_dt_bias'])
    beta = jax.nn.sigmoid(d_b.astype(F32))
    o_dn, s_dn_new = gated_delta(dq, dk, dv.reshape(B, L, H_DN, DV_DN), g, beta, s_dn)
    out_d = gated_rms_norm(o_dn, p['dn_norm_g'], d_z)
    merged = jnp.concatenate([out_a, out_b, out_c.astype(F32), out_d], axis=-1).astype(h.dtype)
    return merged @ p['w_mix_out'], (s_ret_new, lru_h_new, lru_conv_new, s_dn_new, dn_conv_new, sg_v)


def run_trunk(x, pos, init_states, params):
    new_states = []
    for l in range(DEPTH):
        p = params[l]
        x = layer_norm(ALPHA * x + 0.5 * swiglu(x, p['ffn1_w_in'], p['ffn1_w_out']), p['ln1_g'], p['ln1_b'])
        mix, st = token_mixers(x, pos, *init_states[l], p)
        x = layer_norm(ALPHA * x + mix, p['ln2_g'], p['ln2_b'])
        x = layer_norm(ALPHA * x + 0.5 * swiglu(x, p['ffn2_w_in'], p['ffn2_w_out']), p['ln3_g'], p['ln3_b'])
        new_states.append(st)
    return x, new_states


def setup_inputs(seed: int = 0) -> dict:
    key = jax.random.key(seed)
    ks = iter(jax.random.split(key, 64))

    def nrm(shape, scale):
        return jax.random.normal(next(ks), shape, F32) * scale

    def unif(shape, lo, hi):
        return jax.random.uniform(next(ks), shape, F32, lo, hi)

    G = D_GROUP
    inp = {}
    inp['x_prompt'] = nrm((BATCH, SEQ, D_MODEL), 1.0)
    inp['x_sample'] = nrm((DEC_BATCH, DEC_SEQ, D_MODEL), 1.0)
    inp['state_ret'] = nrm((DEPTH, DEC_BATCH, H_RET, DK_RET, DV_RET), 1.0)
    inp['state_lru_h'] = nrm((DEPTH, DEC_BATCH, G), 0.5)
    inp['state_lru_conv'] = nrm((DEPTH, DEC_BATCH, CONV_W - 1, G), 1.0)
    inp['state_dn'] = nrm((DEPTH, DEC_BATCH, H_DN, DK_DN, DV_DN), 0.1)
    inp['state_dn_conv'] = nrm((DEPTH, DEC_BATCH, CONV_W - 1, 3 * G), 1.0)
    inp['ffn1_w_in'] = nrm((DEPTH, D_MODEL, 2 * D_FF), D_MODEL ** -0.5)
    inp['ffn1_w_out'] = nrm((DEPTH, D_FF, D_MODEL), BETA_INIT * D_FF ** -0.5)
    inp['ln1_g'] = 1.0 + nrm((DEPTH, D_MODEL), 0.02)
    inp['ln1_b'] = nrm((DEPTH, D_MODEL), 0.02)
    inp['w_mix_in'] = nrm((DEPTH, D_MODEL, D_IN), D_MODEL ** -0.5)
    inp['ret_norm_g'] = 1.0 + nrm((DEPTH, G), 0.02)
    inp['lru_conv_w'] = nrm((DEPTH, CONV_W, G), 0.5)
    inp['lru_conv_b'] = nrm((DEPTH, G), 0.02)
    inp['lru_w_a'] = nrm((DEPTH, H_LRU, BLK_LRU, BLK_LRU), BLK_LRU ** -0.5)
    inp['lru_b_a'] = nrm((DEPTH, G), 0.1)
    inp['lru_w_x'] = nrm((DEPTH, H_LRU, BLK_LRU, BLK_LRU), BLK_LRU ** -0.5)
    inp['lru_b_x'] = nrm((DEPTH, G), 0.1)
    a_c = unif((DEPTH, G), 0.9, 0.999)
    a_base = a_c ** (1.0 / LRU_C)
    inp['lru_lam'] = jnp.log(a_base) - jnp.log1p(-a_base)
    inp['sg_ln_g'] = 1.0 + nrm((DEPTH, G), 0.02)
    inp['sg_ln_b'] = nrm((DEPTH, G), 0.02)
    inp['sg_w'] = nrm((DEPTH, H_SG, SG_CHUNK, SG_CHUNK), 0.5 * SG_CHUNK ** -0.5)
    inp['sg_b'] = 1.0 + nrm((DEPTH, H_SG, SG_CHUNK), 0.1)
    inp['dn_conv_w'] = nrm((DEPTH, CONV_W, 3 * G), 0.5)
    inp['dn_a_log'] = jnp.log(unif((DEPTH, H_DN), 1.0, 16.0))
    dt = jnp.exp(unif((DEPTH, H_DN), float(np.log(1e-3)), float(np.log(1e-1))))
    inp['dn_dt_bias'] = dt + jnp.log(-jnp.expm1(-dt))
    inp['dn_norm_g'] = 1.0 + nrm((DEPTH, DV_DN), 0.02)
    inp['w_mix_out'] = nrm((DEPTH, D_MIX, D_MODEL), BETA_INIT * D_MIX ** -0.5)
    inp['ln2_g'] = 1.0 + nrm((DEPTH, D_MODEL), 0.02)
    inp['ln2_b'] = nrm((DEPTH, D_MODEL), 0.02)
    inp['ffn2_w_in'] = nrm((DEPTH, D_MODEL, 2 * D_FF), D_MODEL ** -0.5)
    inp['ffn2_w_out'] = nrm((DEPTH, D_FF, D_MODEL), BETA_INIT * D_FF ** -0.5)
    inp['ln3_g'] = 1.0 + nrm((DEPTH, D_MODEL), 0.02)
    inp['ln3_b'] = nrm((DEPTH, D_MODEL), 0.02)
    return inp


def reference(x_prompt, x_sample, state_ret, state_lru_h, state_lru_conv, state_dn, state_dn_conv,
              ffn1_w_in, ffn1_w_out, ln1_g, ln1_b, w_mix_in, ret_norm_g, lru_conv_w, lru_conv_b,
              lru_w_a, lru_b_a, lru_w_x, lru_b_x, lru_lam, sg_ln_g, sg_ln_b, sg_w, sg_b,
              dn_conv_w, dn_a_log, dn_dt_bias, dn_norm_g, w_mix_out, ln2_g, ln2_b,
              ffn2_w_in, ffn2_w_out, ln3_g, ln3_b):
    params = [dict(ffn1_w_in=ffn1_w_in[l], ffn1_w_out=ffn1_w_out[l], ln1_g=ln1_g[l], ln1_b=ln1_b[l],
                   w_mix_in=w_mix_in[l], ret_norm_g=ret_norm_g[l], lru_conv_w=lru_conv_w[l],
                   lru_conv_b=lru_conv_b[l], lru_w_a=lru_w_a[l], lru_b_a=lru_b_a[l], lru_w_x=lru_w_x[l],
                   lru_b_x=lru_b_x[l], lru_lam=lru_lam[l], sg_ln_g=sg_ln_g[l], sg_ln_b=sg_ln_b[l],
                   sg_w=sg_w[l], sg_b=sg_b[l], dn_conv_w=dn_conv_w[l], dn_a_log=dn_a_log[l],
                   dn_dt_bias=dn_dt_bias[l], dn_norm_g=dn_norm_g[l], w_mix_out=w_mix_out[l],
                   ln2_g=ln2_g[l], ln2_b=ln2_b[l], ffn2_w_in=ffn2_w_in[l], ffn2_w_out=ffn2_w_out[l],
                   ln3_g=ln3_g[l], ln3_b=ln3_b[l])
              for l in range(DEPTH)]

    bp, lp = x_prompt.shape[0], x_prompt.shape[1]
    dt_p = x_prompt.dtype
    prompt_init = [(jnp.zeros((bp, H_RET, DK_RET, DV_RET), dt_p), jnp.zeros((bp, D_GROUP), dt_p),
                    jnp.zeros((bp, CONV_W - 1, D_GROUP), dt_p), jnp.zeros((bp, H_DN, DK_DN, DV_DN), dt_p),
                    jnp.zeros((bp, CONV_W - 1, 3 * D_GROUP), dt_p)) for _ in range(DEPTH)]
    y_prompt, st_p = run_trunk(x_prompt, jnp.arange(lp), prompt_init, params)

    ls = x_sample.shape[1]
    sample_init = [(state_ret[l], state_lru_h[l], state_lru_conv[l], state_dn[l], state_dn_conv[l])
                   for l in range(DEPTH)]
    y_sample, st_s = run_trunk(x_sample, PAST_LEN + jnp.arange(ls), sample_init, params)

    ret_p = jnp.stack([s[0] for s in st_p])
    lru_h_p = jnp.stack([s[1] for s in st_p])
    lru_conv_p = jnp.stack([s[2] for s in st_p])
    dn_p = jnp.stack([s[3] for s in st_p])
    dn_conv_p = jnp.stack([s[4] for s in st_p])
    ret_s = jnp.stack([s[0] for s in st_s])
    lru_h_s = jnp.stack([s[1] for s in st_s])
    lru_conv_s = jnp.stack([s[2] for s in st_s])
    dn_s = jnp.stack([s[3] for s in st_s])
    dn_conv_s = jnp.stack([s[4] for s in st_s])
    sg_v_s = jnp.stack([s[5] for s in st_s])
    return (y_prompt, y_sample, ret_p, lru_h_p, lru_conv_p, dn_p, dn_conv_p,
            ret_s, lru_h_s, lru_conv_s, dn_s, dn_conv_s, sg_v_s)
```

```python
import functools
import math

import numpy as np
import jax
import jax.numpy as jnp
from jax import lax
from jax.experimental import pallas as pl
from jax.experimental.pallas import tpu as pltpu

F32 = jnp.float32
BF16 = jnp.bfloat16

DEPTH = 2
PAST_LEN = 4096
CHUNK = 64
D_GROUP = 512
N_HEADS = 4
D_HEAD = D_GROUP // N_HEADS
H_LRU = 8
LRU_C = 8.0
CONV_W = 4
SG_CHUNK = 128
ROPE_BASE = 10000.0
ALPHA = (2.0 * DEPTH) ** 0.25
EPS = 1e-5

VMEM_LIMIT = 52 * 1024 * 1024


def _params(*sem):
    return pltpu.CompilerParams(dimension_semantics=sem, vmem_limit_bytes=VMEM_LIMIT)


def _mm(a, b):
    return jnp.dot(a.astype(BF16), b.astype(BF16), preferred_element_type=F32)


def _mm_nt(a, b):
    return lax.dot_general(a.astype(BF16), b.astype(BF16), (((1,), (1,)), ((), ())),
                           preferred_element_type=F32)


def _mm_tn(a, b):
    return lax.dot_general(a.astype(BF16), b.astype(BF16), (((0,), (0,)), ((), ())),
                           preferred_element_type=F32)


def _layer_norm(y, g, b):
    mu = jnp.mean(y, -1, keepdims=True)
    yc = y - mu
    var = jnp.mean(yc * yc, -1, keepdims=True)
    return yc * lax.rsqrt(var + EPS) * g + b


def _silu(x):
    return x * jax.nn.sigmoid(x)


def _gelu(x):
    return jax.nn.gelu(x, approximate=True)


def _ffn_kernel(x_ref, wg_ref, wu_ref, wo_ref, g_ref, b_ref, o_ref, xb_ref, acc_ref, *, nf):
    f = pl.program_id(1)

    @pl.when(f == 0)
    def _():
        xb_ref[...] = x_ref[...].astype(BF16)
        acc_ref[...] = jnp.zeros_like(acc_ref)

    xb = xb_ref[...]
    gate = jnp.dot(xb, wg_ref[...], preferred_element_type=F32)
    up = jnp.dot(xb, wu_ref[...], preferred_element_type=F32)
    h = (_silu(gate) * up).astype(BF16)
    acc_ref[...] += jnp.dot(h, wo_ref[...], preferred_element_type=F32)

    @pl.when(f == nf - 1)
    def _():
        y = ALPHA * x_ref[...] + 0.5 * acc_ref[...]
        o_ref[...] = _layer_norm(y, g_ref[...], b_ref[...])


def _ffn(x, w_in, w_out, g, b, *, tm, tf):
    m, d = x.shape
    d_ff = w_out.shape[0]
    nf = d_ff // tf
    return pl.pallas_call(
        functools.partial(_ffn_kernel, nf=nf),
        grid=(m // tm, nf),
        in_specs=[
            pl.BlockSpec((tm, d), lambda i, f: (i, 0)),
            pl.BlockSpec((d, tf), lambda i, f: (0, f)),
            pl.BlockSpec((d, tf), lambda i, f: (0, f + nf)),
            pl.BlockSpec((tf, d), lambda i, f: (f, 0)),
            pl.BlockSpec((1, d), lambda i, f: (0, 0)),
            pl.BlockSpec((1, d), lambda i, f: (0, 0)),
        ],
        out_specs=pl.BlockSpec((tm, d), lambda i, f: (i, 0)),
        out_shape=jax.ShapeDtypeStruct((m, d), F32),
        scratch_shapes=[pltpu.VMEM((tm, d), BF16), pltpu.VMEM((tm, d), F32)],
        compiler_params=_params("parallel", "arbitrary"),
        name="ffn",
    )(x, w_in, w_in, w_out, g, b)


def _proj_kernel(x_ref, w_ref, wab_ref, o_ref, ab_ref, xb_ref):
    j = pl.program_id(1)

    @pl.when(j == 0)
    def _():
        xb = x_ref[...].astype(BF16)
        xb_ref[...] = xb
        ab_ref[...] = jnp.dot(xb, wab_ref[...], preferred_element_type=F32)

    o_ref[...] = jnp.dot(xb_ref[...], w_ref[...], preferred_element_type=F32)


def _proj(x, w, wab, *, tm, tn):
    m, d = x.shape
    n = 12 * D_GROUP
    return pl.pallas_call(
        _proj_kernel,
        grid=(m // tm, n // tn),
        in_specs=[
            pl.BlockSpec((tm, d), lambda i, j: (i, 0)),
            pl.BlockSpec((d, tn), lambda i, j: (0, j)),
            pl.BlockSpec((d, 128), lambda i, j: (0, 0)),
        ],
        out_specs=[
            pl.BlockSpec((tm, tn), lambda i, j: (i, j)),
            pl.BlockSpec((tm, 128), lambda i, j: (i, 0)),
        ],
        out_shape=[jax.ShapeDtypeStruct((m, n), F32), jax.ShapeDtypeStruct((m, 128), F32)],
        scratch_shapes=[pltpu.VMEM((tm, d), BF16)],
        compiler_params=_params("parallel", "arbitrary"),
        name="proj",
    )(x, w, wab)


def _ret_tables(pos, c):
    half = D_HEAD // 2
    inv = ROPE_BASE ** (-np.arange(half, dtype=np.float64) / half)
    ang = pos.astype(np.float64)[:, None] * inv[None, :]
    cos = np.concatenate([np.cos(ang), np.cos(ang)], -1)
    sin = np.concatenate([-np.sin(ang), np.sin(ang)], -1)
    log_g = np.log1p(-np.exp2(-5.0 - np.arange(N_HEADS, dtype=np.float64)))
    idx = np.arange(c, dtype=np.float64)
    diff = idx[:, None] - idx[None, :]
    scale = D_HEAD ** -0.5
    dmask = np.where(diff >= 0, np.exp(log_g[:, None, None] * np.maximum(diff, 0.0)), 0.0) * scale
    kdec = np.exp(log_g[:, None] * (c - 1.0 - idx)[None, :]) * scale
    qdec = np.exp(log_g[:, None] * (idx + 1.0)[None, :])
    kdec = np.broadcast_to(kdec[:, :, None], (N_HEADS, c, D_HEAD))
    qdec = np.broadcast_to(qdec[:, :, None], (N_HEADS, c, D_HEAD))
    g_c = np.exp(log_g * c)
    as32 = lambda a: jnp.asarray(np.ascontiguousarray(a), F32)
    return as32(cos), as32(sin), as32(dmask), as32(qdec), as32(kdec), [float(v) for v in g_c]


def _ret_kernel(q_ref, k_ref, v_ref, g_ref, cos_ref, sin_ref, dm_ref, qd_ref, kd_ref, gn_ref, s0_ref,
                o_ref, sf_ref, s_scr, *, c, n_chunks, g_c, n_t):
    t = pl.program_id(1)

    @pl.when(t == 0)
    def _():
        s_scr[...] = s0_ref[0]

    def rot(x, cos, sin):
        return x * cos + pltpu.roll(x, D_HEAD // 2, 1) * sin

    def chunk(i, carry):
        r0 = pl.multiple_of(i * c, c)
        rows = pl.ds(r0, c)
        cos = cos_ref[rows, :]
        sin = sin_ref[rows, :]
        for h in range(N_HEADS):
            lanes = slice(h * D_HEAD, (h + 1) * D_HEAD)
            q = rot(q_ref[0, rows, lanes], cos, sin)
            k = rot(k_ref[0, rows, lanes], cos, sin)
            v = v_ref[0, rows, lanes]
            s = s_scr[h]
            scores = _mm_nt(q, k) * dm_ref[h]
            o = _mm(scores, v) + _mm(q * qd_ref[h], s)
            s_scr[h] = g_c[h] * s + _mm_tn(k * kd_ref[h], v)
            mu = jnp.mean(o, -1, keepdims=True)
            oc = o - mu
            var = jnp.mean(oc * oc, -1, keepdims=True)
            on = oc * lax.rsqrt(var + EPS) * gn_ref[:, lanes]
            o_ref[0, rows, lanes] = (on * _silu(g_ref[0, rows, lanes])).astype(o_ref.dtype)
        return carry

    lax.fori_loop(0, n_chunks, chunk, 0)

    @pl.when(t == n_t - 1)
    def _():
        sf_ref[0] = s_scr[...]


def _retention(proj3, state0, gn, pos, *, tt):
    bsz, seq, _ = proj3.shape
    c = min(CHUNK, seq)
    cos, sin, dmask, qdec, kdec, g_c = _ret_tables(pos, c)
    n_t = seq // tt
    col = lambda j: pl.BlockSpec((1, tt, D_GROUP), lambda b, t: (b, t, j))
    full3 = pl.BlockSpec((N_HEADS, c, c), lambda b, t: (0, 0, 0))
    full3d = pl.BlockSpec((N_HEADS, c, D_HEAD), lambda b, t: (0, 0, 0))
    st = pl.BlockSpec((1, N_HEADS, D_HEAD, D_HEAD), lambda b, t: (b, 0, 0, 0))
    return pl.pallas_call(
        functools.partial(_ret_kernel, c=c, n_chunks=tt // c, g_c=g_c, n_t=n_t),
        grid=(bsz, n_t),
        in_specs=[col(0), col(1), col(2), col(3),
                  pl.BlockSpec((tt, D_HEAD), lambda b, t: (t, 0)),
                  pl.BlockSpec((tt, D_HEAD), lambda b, t: (t, 0)),
                  full3, full3d, full3d,
                  pl.BlockSpec((1, D_GROUP), lambda b, t: (0, 0)),
                  st],
        out_specs=[pl.BlockSpec((1, tt, D_GROUP), lambda b, t: (b, t, 0)), st],
        out_shape=[jax.ShapeDtypeStruct((bsz, seq, D_GROUP), BF16),
                   jax.ShapeDtypeStruct(state0.shape, F32)],
        scratch_shapes=[pltpu.VMEM((N_HEADS, D_HEAD, D_HEAD), F32)],
        compiler_params=_params("parallel", "arbitrary"),
        name="ret",
    )(proj3, proj3, proj3, proj3, cos, sin, dmask, qdec, kdec, gn, state0)


PAD = 8


def _conv_block(ext_ref, x, w_ref, tt):
    ext_ref[PAD:PAD + tt, :] = x
    out = x * w_ref[CONV_W - 1:CONV_W, :]
    for j in range(1, CONV_W):
        out = out + ext_ref[PAD - j:PAD - j + tt, :] * w_ref[CONV_W - 1 - j:CONV_W - j, :]
    ext_ref[PAD - (CONV_W - 1):PAD, :] = x[tt - (CONV_W - 1):, :]
    return out


def _lru_kernel(y_ref, x_ref, cs_ref, h0_ref, cw_ref, cb_ref, wa_ref, ba_ref, wx_ref, bx_ref, lam_ref,
                o_ref, hn_ref, cn_ref, ext_ref, a_scr, b_scr, h_scr, hc_ref, *, tt, n_t):
    t = pl.program_id(1)

    @pl.when(t == 0)
    def _():
        ext_ref[PAD - (CONV_W - 1):PAD, :] = cs_ref[0]
        hc_ref[...] = h0_ref[0]

    x = x_ref[0]
    xc = _conv_block(ext_ref, x, cw_ref, tt) + cb_ref[...]
    r = jax.nn.sigmoid(_mm(xc, wa_ref[...]) + ba_ref[...])
    i = jax.nn.sigmoid(_mm(xc, wx_ref[...]) + bx_ref[...])
    log_a = -LRU_C * r * jax.nn.softplus(-lam_ref[...])
    a_scr[...] = jnp.exp(log_a)
    th = jnp.tanh(log_a)
    one_minus_a2 = -2.0 * th / (1.0 - th)
    b_scr[...] = jnp.sqrt(one_minus_a2) * (i * xc)

    def step(s, h):
        row = pl.ds(s, 1)
        h = a_scr[row, :] * h + b_scr[row, :]
        h_scr[row, :] = h
        return h

    hc_ref[...] = lax.fori_loop(0, tt, step, hc_ref[...], unroll=8)
    o_ref[0] = (h_scr[...] * _gelu(y_ref[0])).astype(o_ref.dtype)

    @pl.when(t == n_t - 1)
    def _():
        hn_ref[0] = hc_ref[...]
        cn_ref[0] = x[tt - (CONV_W - 1):, :]


def _rglru(proj3, conv_state, h0, cw, cb, wa, ba, wx, bx, lam, *, tt):
    bsz, seq, _ = proj3.shape
    n_t = seq // tt
    col = lambda j: pl.BlockSpec((1, tt, D_GROUP), lambda b, t: (b, t, j))
    vec = pl.BlockSpec((1, D_GROUP), lambda b, t: (0, 0))
    sq = pl.BlockSpec((D_GROUP, D_GROUP), lambda b, t: (0, 0))
    cs = pl.BlockSpec((1, CONV_W - 1, D_GROUP), lambda b, t: (b, 0, 0))
    hs = pl.BlockSpec((1, 1, D_GROUP), lambda b, t: (b, 0, 0))
    return pl.pallas_call(
        functools.partial(_lru_kernel, tt=tt, n_t=n_t),
        grid=(bsz, n_t),
        in_specs=[col(4), col(5), cs, hs,
                  pl.BlockSpec((CONV_W, D_GROUP), lambda b, t: (0, 0)), vec, sq, vec, sq, vec, vec],
        out_specs=[pl.BlockSpec((1, tt, D_GROUP), lambda b, t: (b, t, 0)), hs, cs],
        out_shape=[jax.ShapeDtypeStruct((bsz, seq, D_GROUP), BF16),
                   jax.ShapeDtypeStruct((bsz, 1, D_GROUP), F32),
                   jax.ShapeDtypeStruct((bsz, CONV_W - 1, D_GROUP), F32)],
        scratch_shapes=[pltpu.VMEM((tt + PAD, D_GROUP), F32), pltpu.VMEM((tt, D_GROUP), F32),
                        pltpu.VMEM((tt, D_GROUP), F32), pltpu.VMEM((tt, D_GROUP), F32),
                        pltpu.VMEM((1, D_GROUP), F32)],
        compiler_params=_params("parallel", "arbitrary"),
        name="lru",
    )(proj3, proj3, conv_state, h0, cw, cb, wa, ba, wx, bx, lam)


def _sg_kernel(u_ref, v_ref, lg_ref, lb_ref, ws_ref, bs_ref, o_ref, vn_ref, *, c):
    u = _gelu(u_ref[0])
    vn = _layer_norm(_gelu(v_ref[0]), lg_ref[...], lb_ref[...])
    vn_ref[0] = vn
    row = lax.broadcasted_iota(jnp.int32, (c, c), 0)
    col = lax.broadcasted_iota(jnp.int32, (c, c), 1)
    mask = (col // CHUNK) <= (row // CHUNK)
    for h in range(N_HEADS):
        lanes = slice(h * D_HEAD, (h + 1) * D_HEAD)
        w = jnp.where(mask, ws_ref[h], 0.0)
        s = _mm(w, vn[:, lanes]) + bs_ref[:, h:h + 1]
        o_ref[0, :, lanes] = (u[:, lanes] * s).astype(o_ref.dtype)


def _spatial_gate(proj3, lg, lb, ws, bs):
    bsz, seq, _ = proj3.shape
    c = min(SG_CHUNK, seq)
    ws = ws[:, :c, :c]
    bs_t = bs[:, :c].T
    col = lambda j: pl.BlockSpec((1, c, D_GROUP), lambda b, t: (b, t, j))
    vec = pl.BlockSpec((1, D_GROUP), lambda b, t: (0, 0))
    out = pl.BlockSpec((1, c, D_GROUP), lambda b, t: (b, t, 0))
    return pl.pallas_call(
        functools.partial(_sg_kernel, c=c),
        grid=(bsz, seq // c),
        in_specs=[col(6), col(7), vec, vec,
                  pl.BlockSpec((N_HEADS, c, c), lambda b, t: (0, 0, 0)),
                  pl.BlockSpec((c, N_HEADS), lambda b, t: (0, 0))],
        out_specs=[out, out],
        out_shape=[jax.ShapeDtypeStruct((bsz, seq, D_GROUP), BF16),
                   jax.ShapeDtypeStruct((bsz, seq, D_GROUP), F32)],
        compiler_params=_params("parallel", "parallel"),
        name="sg",
    )(proj3, proj3, lg, lb, ws, bs_t)


INV_BLOCK = 16


def _unit_lower_inverse(a, c):
    row = lax.broadcasted_iota(jnp.int32, (c, c), 0)
    col = lax.broadcasted_iota(jnp.int32, (c, c), 1)
    eye = (row == col).astype(F32)
    d = jnp.where(row // INV_BLOCK == col // INV_BLOCK, a, 0.0)
    x = eye - d
    p = d
    for _ in range(int(math.log2(INV_BLOCK)) - 1):
        p = _mm(p, p)
        x = x + _mm(x, p)
    size = INV_BLOCK
    while size < c:
        big = 2 * size
        off = jnp.where((row // big == col // big) & (row // size != col // size), a, 0.0)
        x = x - _mm(x, _mm(off, x))
        size = big
    return x


def _dn_kernel(q_ref, k_ref, v_ref, z_ref, ab_ref, cs_ref, cw_ref, al_ref, dt_ref, gn_ref, s0_ref,
               o_ref, sf_ref, cn_ref, ext_ref, qkv_scr, s_scr, *, c, tt, n_t):
    t = pl.program_id(1)

    @pl.when(t == 0)
    def _():
        for j in range(3):
            ext_ref[j, PAD - (CONV_W - 1):PAD, :] = cs_ref[0, :, j * D_GROUP:(j + 1) * D_GROUP]
        s_scr[...] = s0_ref[0]

    raw = (q_ref, k_ref, v_ref)
    for j in range(3):
        x = raw[j][0]
        qkv_scr[j] = _silu(_conv_block(ext_ref.at[j], x, cw_ref.at[:, j * D_GROUP:(j + 1) * D_GROUP], tt))

    row = lax.broadcasted_iota(jnp.int32, (c, c), 0)
    col = lax.broadcasted_iota(jnp.int32, (c, c), 1)
    tri = row >= col
    strict = row > col
    tri_f = tri.astype(F32)
    scale = D_HEAD ** -0.5

    def l2n(x):
        return x * lax.rsqrt(jnp.sum(x * x, -1, keepdims=True) + 1e-6)

    def chunk(i, carry):
        r0 = pl.multiple_of(i * c, c)
        rows = pl.ds(r0, c)
        ab = ab_ref[0, rows, :]
        g_all = -jnp.exp(al_ref[...]) * jax.nn.softplus(ab + dt_ref[...])
        beta_all = jax.nn.sigmoid(ab)
        gc_all = jnp.dot(tri_f, g_all, preferred_element_type=F32, precision=lax.Precision.HIGHEST)
        gc_all_t = gc_all.T
        for h in range(N_HEADS):
            lanes = slice(h * D_HEAD, (h + 1) * D_HEAD)
            q = l2n(qkv_scr[0, rows, lanes]) * scale
            k = l2n(qkv_scr[1, rows, lanes])
            v = qkv_scr[2, rows, lanes]
            beta = beta_all[:, N_HEADS + h:N_HEADS + h + 1]
            gcol = gc_all[:, h:h + 1]
            grow = gc_all_t[h:h + 1, :]
            glast = gcol[c - 1:c, :]
            decay = jnp.where(tri, jnp.exp(gcol - grow), 0.0)
            egc = jnp.exp(gcol)
            kb = k * beta
            vb = v * beta
            a_mat = jnp.where(strict, _mm_nt(kb, k) * decay, 0.0)
            tinv = _unit_lower_inverse(a_mat, c)
            u = _mm(tinv, vb)
            w = _mm(tinv, kb * egc)
            attn = _mm_nt(q, k) * decay
            s = s_scr[h]
            v_new = u - _mm(w, s)
            o = _mm(q * egc, s) + _mm(attn, v_new)
            s_scr[h] = s * jnp.exp(glast) + _mm_tn(k * jnp.exp(glast - gcol), v_new)
            on = o * lax.rsqrt(jnp.mean(o * o, -1, keepdims=True) + EPS) * gn_ref[...]
            o_ref[0, rows, lanes] = (on * _silu(z_ref[0, rows, lanes])).astype(o_ref.dtype)
        return carry

    lax.fori_loop(0, tt // c, chunk, 0)

    @pl.when(t == n_t - 1)
    def _():
        sf_ref[0] = s_scr[...]
        for j in range(3):
            cn_ref[0, :, j * D_GROUP:(j + 1) * D_GROUP] = raw[j][0, tt - (CONV_W - 1):, :]


def _deltanet(proj3, ab3, conv_state, state0, cw, a_log, dt_bias, gn, *, tt):
    bsz, seq, _ = proj3.shape
    c = min(CHUNK, seq)
    n_t = seq // tt
    col = lambda j: pl.BlockSpec((1, tt, D_GROUP), lambda b, t: (b, t, j))
    st = pl.BlockSpec((1, N_HEADS, D_HEAD, D_HEAD), lambda b, t: (b, 0, 0, 0))
    cs = pl.BlockSpec((1, CONV_W - 1, 3 * D_GROUP), lambda b, t: (b, 0, 0))
    lane_row = pl.BlockSpec((1, 128), lambda b, t: (0, 0))
    return pl.pallas_call(
        functools.partial(_dn_kernel, c=c, tt=tt, n_t=n_t),
        grid=(bsz, n_t),
        in_specs=[col(8), col(9), col(10), col(11),
                  pl.BlockSpec((1, tt, 128), lambda b, t: (b, t, 0)),
                  cs,
                  pl.BlockSpec((CONV_W, 3 * D_GROUP), lambda b, t: (0, 0)),
                  lane_row, lane_row,
                  pl.BlockSpec((1, D_HEAD), lambda b, t: (0, 0)),
                  st],
        out_specs=[pl.BlockSpec((1, tt, D_GROUP), lambda b, t: (b, t, 0)), st, cs],
        out_shape=[jax.ShapeDtypeStruct((bsz, seq, D_GROUP), BF16),
                   jax.ShapeDtypeStruct(state0.shape, F32),
                   jax.ShapeDtypeStruct((bsz, CONV_W - 1, 3 * D_GROUP), F32)],
        scratch_shapes=[pltpu.VMEM((3, tt + PAD, D_GROUP), F32), pltpu.VMEM((3, tt, D_GROUP), F32),
                        pltpu.VMEM((N_HEADS, D_HEAD, D_HEAD), F32)],
        compiler_params=_params("parallel", "arbitrary"),
        name="dn",
    )(proj3, proj3, proj3, proj3, ab3, conv_state, cw, a_log, dt_bias, gn, state0)


def _mixout_kernel(a_ref, b_ref, c_ref, d_ref, x_ref, w_ref, g_ref, bb_ref, o_ref):
    acc = jnp.dot(a_ref[...], w_ref[0:D_GROUP, :], preferred_element_type=F32)
    acc += jnp.dot(b_ref[...], w_ref[D_GROUP:2 * D_GROUP, :], preferred_element_type=F32)
    acc += jnp.dot(c_ref[...], w_ref[2 * D_GROUP:3 * D_GROUP, :], preferred_element_type=F32)
    acc += jnp.dot(d_ref[...], w_ref[3 * D_GROUP:4 * D_GROUP, :], preferred_element_type=F32)
    o_ref[...] = _layer_norm(ALPHA * x_ref[...] + acc, g_ref[...], bb_ref[...])


def _mixout(a, b, c, d, x, w, g, bb, *, tm):
    m, dm = x.shape
    part = pl.BlockSpec((tm, D_GROUP), lambda i: (i, 0))
    vec = pl.BlockSpec((1, dm), lambda i: (0, 0))
    return pl.pallas_call(
        _mixout_kernel,
        grid=(m // tm,),
        in_specs=[part, part, part, part,
                  pl.BlockSpec((tm, dm), lambda i: (i, 0)),
                  pl.BlockSpec((4 * D_GROUP, dm), lambda i: (0, 0)),
                  vec, vec],
        out_specs=pl.BlockSpec((tm, dm), lambda i: (i, 0)),
        out_shape=jax.ShapeDtypeStruct((m, dm), F32),
        compiler_params=_params("parallel"),
        name="mixout",
    )(a, b, c, d, x, w, g, bb)


def _block_diag(w):
    h, n, _ = w.shape
    eye = jnp.eye(h, dtype=w.dtype)
    return (eye[:, None, :, None] * w[:, :, None, :]).reshape(h * n, h * n)


def _prep_layer(p):
    row = lambda v: v.reshape(1, -1)
    n_main = 12 * D_GROUP
    pad128 = lambda v: jnp.pad(v.reshape(1, -1), ((0, 0), (0, 128 - v.size)))
    q = dict(p)
    for name in ('ffn1_w_in', 'ffn1_w_out', 'ffn2_w_in', 'ffn2_w_out', 'w_mix_out'):
        q[name] = p[name].astype(BF16)
    w_in = p['w_mix_in'].astype(BF16)
    q['w_mix_in'] = w_in
    q['w_mix_ab'] = jnp.pad(w_in[:, n_main:], ((0, 0), (0, 128 - 2 * N_HEADS)))
    for name in ('ln1_g', 'ln1_b', 'ln2_g', 'ln2_b', 'ln3_g', 'ln3_b', 'ret_norm_g', 'lru_conv_b',
                 'lru_b_a', 'lru_b_x', 'lru_lam', 'sg_ln_g', 'sg_ln_b', 'dn_norm_g'):
        q[name] = row(p[name])
    q['lru_w_a'] = _block_diag(p['lru_w_a']).astype(BF16)
    q['lru_w_x'] = _block_diag(p['lru_w_x']).astype(BF16)
    q['dn_a_log'] = pad128(p['dn_a_log'])
    q['dn_dt_bias'] = pad128(p['dn_dt_bias'])
    return q


def _run_trunk(x3, pos, states, layers, *, tm, tf, tt):
    bsz, seq, dm = x3.shape
    x = x3.reshape(bsz * seq, dm)
    new_states = []
    for l in range(DEPTH):
        p = layers[l]
        s_ret, s_lru, s_lru_conv, s_dn, s_dn_conv = states[l]
        x = _ffn(x, p['ffn1_w_in'], p['ffn1_w_out'], p['ln1_g'], p['ln1_b'], tm=tm, tf=tf)
        proj, ab = _proj(x, p['w_mix_in'], p['w_mix_ab'], tm=tm, tn=D_GROUP)
        proj3 = proj.reshape(bsz, seq, -1)
        ab3 = ab.reshape(bsz, seq, 128)
        out_a, ret_new = _retention(proj3, s_ret, p['ret_norm_g'], pos, tt=tt)
        out_b, lru_h_new, lru_conv_new = _rglru(
            proj3, s_lru_conv, s_lru.reshape(bsz, 1, D_GROUP), p['lru_conv_w'], p['lru_conv_b'],
            p['lru_w_a'], p['lru_b_a'], p['lru_w_x'], p['lru_b_x'], p['lru_lam'], tt=tt)
        out_c, sg_v = _spatial_gate(proj3, p['sg_ln_g'], p['sg_ln_b'], p['sg_w'], p['sg_b'])
        out_d, dn_new, dn_conv_new = _deltanet(proj3, ab3, s_dn_conv, s_dn, p['dn_conv_w'],
                                               p['dn_a_log'], p['dn_dt_bias'], p['dn_norm_g'], tt=tt)
        flat = lambda o: o.reshape(bsz * seq, D_GROUP)
        x = _mixout(flat(out_a), flat(out_b), flat(out_c), flat(out_d), x, p['w_mix_out'],
                    p['ln2_g'], p['ln2_b'], tm=tm)
        x = _ffn(x, p['ffn2_w_in'], p['ffn2_w_out'], p['ln3_g'], p['ln3_b'], tm=tm, tf=tf)
        new_states.append((ret_new, lru_h_new.reshape(bsz, D_GROUP), lru_conv_new, dn_new, dn_conv_new, sg_v))
    return x.reshape(bsz, seq, dm), new_states


def kernel(x_prompt, x_sample, state_ret, state_lru_h, state_lru_conv, state_dn, state_dn_conv, ffn1_w_in, ffn1_w_out, ln1_g, ln1_b, w_mix_in, ret_norm_g, lru_conv_w, lru_conv_b, lru_w_a, lru_b_a, lru_w_x, lru_b_x, lru_lam, sg_ln_g, sg_ln_b, sg_w, sg_b, dn_conv_w, dn_a_log, dn_dt_bias, dn_norm_g, w_mix_out, ln2_g, ln2_b, ffn2_w_in, ffn2_w_out, ln3_g, ln3_b):
    weights = dict(ffn1_w_in=ffn1_w_in, ffn1_w_out=ffn1_w_out, ln1_g=ln1_g, ln1_b=ln1_b, w_mix_in=w_mix_in,
                   ret_norm_g=ret_norm_g, lru_conv_w=lru_conv_w, lru_conv_b=lru_conv_b, lru_w_a=lru_w_a,
                   lru_b_a=lru_b_a, lru_w_x=lru_w_x, lru_b_x=lru_b_x, lru_lam=lru_lam, sg_ln_g=sg_ln_g,
                   sg_ln_b=sg_ln_b, sg_w=sg_w, sg_b=sg_b, dn_conv_w=dn_conv_w, dn_a_log=dn_a_log,
                   dn_dt_bias=dn_dt_bias, dn_norm_g=dn_norm_g, w_mix_out=w_mix_out, ln2_g=ln2_g, ln2_b=ln2_b,
                   ffn2_w_in=ffn2_w_in, ffn2_w_out=ffn2_w_out, ln3_g=ln3_g, ln3_b=ln3_b)
    layers = [_prep_layer({k: v[l] for k, v in weights.items()}) for l in range(DEPTH)]

    bp, lp, _ = x_prompt.shape
    bs, ls, _ = x_sample.shape
    zeros = lambda *shape: jnp.zeros(shape, F32)
    prompt_init = [(zeros(bp, N_HEADS, D_HEAD, D_HEAD), zeros(bp, D_GROUP), zeros(bp, CONV_W - 1, D_GROUP),
                    zeros(bp, N_HEADS, D_HEAD, D_HEAD), zeros(bp, CONV_W - 1, 3 * D_GROUP))
                   for _ in range(DEPTH)]
    sample_init = [(state_ret[l], state_lru_h[l], state_lru_conv[l], state_dn[l], state_dn_conv[l])
                   for l in range(DEPTH)]

    y_prompt, st_p = _run_trunk(x_prompt, np.arange(lp), prompt_init, layers, tm=512, tf=512, tt=min(lp, 256))
    y_sample, st_s = _run_trunk(x_sample, PAST_LEN + np.arange(ls), sample_init, layers,
                                tm=bs * ls, tf=512, tt=ls)

    stack = lambda sts, i: jnp.stack([s[i] for s in sts])
    return (y_prompt, y_sample,
            stack(st_p, 0), stack(st_p, 1), stack(st_p, 2), stack(st_p, 3), stack(st_p, 4),
            stack(st_s, 0), stack(st_s, 1), stack(st_s, 2), stack(st_s, 3), stack(st_s, 4), stack(st_s, 5))
```

```python
import functools
import math

import numpy as np
import jax
import jax.numpy as jnp
from jax import lax
from jax.experimental import pallas as pl
from jax.experimental.pallas import tpu as pltpu

F32 = jnp.float32
BF16 = jnp.bfloat16

DEPTH = 2
PAST_LEN = 4096
CHUNK = 64
D_GROUP = 512
N_HEADS = 4
D_HEAD = D_GROUP // N_HEADS
H_LRU = 8
LRU_C = 8.0
CONV_W = 4
SG_CHUNK = 128
ROPE_BASE = 10000.0
ALPHA = (2.0 * DEPTH) ** 0.25
EPS = 1e-5

VMEM_LIMIT = 58 * 1024 * 1024


def _params(*sem):
    return pltpu.CompilerParams(dimension_semantics=sem, vmem_limit_bytes=VMEM_LIMIT)


def _mm(a, b):
    return jnp.dot(a.astype(BF16), b.astype(BF16), preferred_element_type=F32)


def _mm_nt(a, b):
    return lax.dot_general(a.astype(BF16), b.astype(BF16), (((1,), (1,)), ((), ())),
                           preferred_element_type=F32)


def _mm_tn(a, b):
    return lax.dot_general(a.astype(BF16), b.astype(BF16), (((0,), (0,)), ((), ())),
                           preferred_element_type=F32)


def _layer_norm(y, g, b):
    mu = jnp.mean(y, -1, keepdims=True)
    yc = y - mu
    var = jnp.mean(yc * yc, -1, keepdims=True)
    return yc * lax.rsqrt(var + EPS) * g + b


def _silu(x):
    return x * jax.nn.sigmoid(x)


def _gelu(x):
    return jax.nn.gelu(x, approximate=True)


def _ffn_kernel(x_ref, wg_ref, wu_ref, wo_ref, g_ref, b_ref, o_ref, xb_ref, *, nf):
    f = pl.program_id(1)

    @pl.when(f == 0)
    def _():
        xb_ref[...] = x_ref[...].astype(BF16)
        o_ref[...] = jnp.zeros_like(o_ref)

    xb = xb_ref[...]
    gate = jnp.dot(xb, wg_ref[...], preferred_element_type=F32)
    up = jnp.dot(xb, wu_ref[...], preferred_element_type=F32)
    h = (_silu(gate) * up).astype(BF16)
    o_ref[...] += jnp.dot(h, wo_ref[...], preferred_element_type=F32)

    @pl.when(f == nf - 1)
    def _():
        y = ALPHA * x_ref[...] + 0.5 * o_ref[...]
        o_ref[...] = _layer_norm(y, g_ref[...], b_ref[...])


def _ffn(x, w_in, w_out, g, b, l, *, tm, tf):
    m, d = x.shape
    d_ff = w_out.shape[1]
    nf = d_ff // tf
    vec = pl.BlockSpec((None, 1, d), lambda i, f: (l, 0, 0))
    return pl.pallas_call(
        functools.partial(_ffn_kernel, nf=nf),
        grid=(m // tm, nf),
        in_specs=[
            pl.BlockSpec((tm, d), lambda i, f: (i, 0), pipeline_mode=pl.Buffered(1)),
            pl.BlockSpec((None, d, tf), lambda i, f: (l, 0, f)),
            pl.BlockSpec((None, d, tf), lambda i, f: (l, 0, f + nf)),
            pl.BlockSpec((None, tf, d), lambda i, f: (l, f, 0)),
            vec, vec,
        ],
        out_specs=pl.BlockSpec((tm, d), lambda i, f: (i, 0)),
        out_shape=jax.ShapeDtypeStruct((m, d), F32),
        scratch_shapes=[pltpu.VMEM((tm, d), BF16)],
        compiler_params=_params("parallel", "arbitrary"),
        name="ffn",
    )(x, w_in, w_in, w_out, g, b)


def _proj_kernel(x_ref, w_ref, wab_ref, o_ref, ab_ref, xb_ref):
    j = pl.program_id(1)

    @pl.when(j == 0)
    def _():
        xb = x_ref[...].astype(BF16)
        xb_ref[...] = xb
        ab_ref[...] = jnp.dot(xb, wab_ref[...], preferred_element_type=F32)

    o_ref[...] = jnp.dot(xb_ref[...], w_ref[...].astype(BF16), preferred_element_type=F32)


def _proj(x, w, wab, l, *, tm, tn):
    m, d = x.shape
    n = 12 * D_GROUP
    return pl.pallas_call(
        _proj_kernel,
        grid=(m // tm, n // tn),
        in_specs=[
            pl.BlockSpec((tm, d), lambda i, j: (i, 0)),
            pl.BlockSpec((None, d, tn), lambda i, j: (l, 0, j)),
            pl.BlockSpec((None, d, 128), lambda i, j: (l, 0, 0)),
        ],
        out_specs=[
            pl.BlockSpec((tm, tn), lambda i, j: (i, j)),
            pl.BlockSpec((tm, 128), lambda i, j: (i, 0)),
        ],
        out_shape=[jax.ShapeDtypeStruct((m, n), F32), jax.ShapeDtypeStruct((m, 128), F32)],
        scratch_shapes=[pltpu.VMEM((tm, d), BF16)],
        compiler_params=_params("parallel", "arbitrary"),
        name="proj",
    )(x, w, wab)


def _ret_tables(pos, c):
    half = D_HEAD // 2
    inv = ROPE_BASE ** (-np.arange(half, dtype=np.float64) / half)
    ang = pos.astype(np.float64)[:, None] * inv[None, :]
    cos = np.concatenate([np.cos(ang), np.cos(ang)], -1)
    sin = np.concatenate([-np.sin(ang), np.sin(ang)], -1)
    log_g = np.log1p(-np.exp2(-5.0 - np.arange(N_HEADS, dtype=np.float64)))
    idx = np.arange(c, dtype=np.float64)
    diff = idx[:, None] - idx[None, :]
    scale = D_HEAD ** -0.5
    dmask = np.where(diff >= 0, np.exp(log_g[:, None, None] * np.maximum(diff, 0.0)), 0.0) * scale
    kdec = np.exp(log_g[:, None] * (c - 1.0 - idx)[None, :]) * scale
    qdec = np.exp(log_g[:, None] * (idx + 1.0)[None, :])
    kdec = np.broadcast_to(kdec[:, :, None], (N_HEADS, c, D_HEAD))
    qdec = np.broadcast_to(qdec[:, :, None], (N_HEADS, c, D_HEAD))
    g_c = np.exp(log_g * c)
    as32 = lambda a: jnp.asarray(np.ascontiguousarray(a), F32)
    return as32(cos), as32(sin), as32(dmask), as32(qdec), as32(kdec), [float(v) for v in g_c]


def _ret_kernel(q_ref, k_ref, v_ref, g_ref, cos_ref, sin_ref, dm_ref, qd_ref, kd_ref, gn_ref, s0_ref,
                o_ref, sf_ref, s_scr, *, c, n_chunks, g_c, n_t):
    t = pl.program_id(1)

    @pl.when(t == 0)
    def _():
        s_scr[...] = s0_ref[0]

    def rot(x, cos, sin):
        return x * cos + pltpu.roll(x, D_HEAD // 2, 1) * sin

    def chunk(i, carry):
        r0 = pl.multiple_of(i * c, c)
        rows = pl.ds(r0, c)
        cos = cos_ref[rows, :]
        sin = sin_ref[rows, :]
        for h in range(N_HEADS):
            lanes = slice(h * D_HEAD, (h + 1) * D_HEAD)
            q = rot(q_ref[0, rows, lanes], cos, sin)
            k = rot(k_ref[0, rows, lanes], cos, sin)
            v = v_ref[0, rows, lanes]
            s = s_scr[h]
            scores = _mm_nt(q, k) * dm_ref[h]
            o = _mm(scores, v) + _mm(q * qd_ref[h], s)
            s_scr[h] = g_c[h] * s + _mm_tn(k * kd_ref[h], v)
            mu = jnp.mean(o, -1, keepdims=True)
            oc = o - mu
            var = jnp.mean(oc * oc, -1, keepdims=True)
            on = oc * lax.rsqrt(var + EPS) * gn_ref[:, lanes]
            o_ref[0, rows, lanes] = (on * _silu(g_ref[0, rows, lanes])).astype(o_ref.dtype)
        return carry

    lax.fori_loop(0, n_chunks, chunk, 0)

    @pl.when(t == n_t - 1)
    def _():
        sf_ref[0] = s_scr[...]


def _retention(proj3, state0, gn, pos, l, *, tt):
    bsz, seq, _ = proj3.shape
    c = min(CHUNK, seq)
    cos, sin, dmask, qdec, kdec, g_c = _ret_tables(pos, c)
    n_t = seq // tt
    col = lambda j: pl.BlockSpec((1, tt, D_GROUP), lambda b, t: (b, t, j))
    full3 = pl.BlockSpec((N_HEADS, c, c), lambda b, t: (0, 0, 0))
    full3d = pl.BlockSpec((N_HEADS, c, D_HEAD), lambda b, t: (0, 0, 0))
    st = pl.BlockSpec((1, N_HEADS, D_HEAD, D_HEAD), lambda b, t: (b, 0, 0, 0))
    return pl.pallas_call(
        functools.partial(_ret_kernel, c=c, n_chunks=tt // c, g_c=g_c, n_t=n_t),
        grid=(bsz, n_t),
        in_specs=[col(0), col(1), col(2), col(3),
                  pl.BlockSpec((tt, D_HEAD), lambda b, t: (t, 0)),
                  pl.BlockSpec((tt, D_HEAD), lambda b, t: (t, 0)),
                  full3, full3d, full3d,
                  pl.BlockSpec((None, 1, D_GROUP), lambda b, t: (l, 0, 0)),
                  st],
        out_specs=[pl.BlockSpec((1, tt, D_GROUP), lambda b, t: (b, t, 0)), st],
        out_shape=[jax.ShapeDtypeStruct((bsz, seq, D_GROUP), BF16),
                   jax.ShapeDtypeStruct(state0.shape, F32)],
        scratch_shapes=[pltpu.VMEM((N_HEADS, D_HEAD, D_HEAD), F32)],
        compiler_params=_params("parallel", "arbitrary"),
        name="ret",
    )(proj3, proj3, proj3, proj3, cos, sin, dmask, qdec, kdec, gn, state0)


PAD = 8


def _conv_block(ext_ref, x, w_ref, tt):
    ext_ref[PAD:PAD + tt, :] = x
    out = x * w_ref[CONV_W - 1:CONV_W, :]
    for j in range(1, CONV_W):
        out = out + ext_ref[PAD - j:PAD - j + tt, :] * w_ref[CONV_W - 1 - j:CONV_W - j, :]
    ext_ref[PAD - (CONV_W - 1):PAD, :] = x[tt - (CONV_W - 1):, :]
    return out


def _lru_kernel(y_ref, x_ref, cs_ref, h0_ref, cw_ref, cb_ref, wa_ref, ba_ref, wx_ref, bx_ref, lam_ref,
                o_ref, hn_ref, cn_ref, ext_ref, a_scr, b_scr, h_scr, hc_ref, *, tt, n_t):
    t = pl.program_id(1)

    @pl.when(t == 0)
    def _():
        ext_ref[PAD - (CONV_W - 1):PAD, :] = cs_ref[0]
        hc_ref[...] = h0_ref[0]

    x = x_ref[0]
    xc = _conv_block(ext_ref, x, cw_ref, tt) + cb_ref[...]
    r = jax.nn.sigmoid(_mm(xc, wa_ref[...]) + ba_ref[...])
    i = jax.nn.sigmoid(_mm(xc, wx_ref[...]) + bx_ref[...])
    log_a = -LRU_C * r * jax.nn.softplus(-lam_ref[...])
    a_scr[...] = jnp.exp(log_a)
    th = jnp.tanh(log_a)
    one_minus_a2 = -2.0 * th / (1.0 - th)
    b_scr[...] = jnp.sqrt(one_minus_a2) * (i * xc)

    def step(s, h):
        row = pl.ds(s, 1)
        h = a_scr[row, :] * h + b_scr[row, :]
        h_scr[row, :] = h
        return h

    hc_ref[...] = lax.fori_loop(0, tt, step, hc_ref[...], unroll=8)
    o_ref[0] = (h_scr[...] * _gelu(y_ref[0])).astype(o_ref.dtype)

    @pl.when(t == n_t - 1)
    def _():
        hn_ref[0] = hc_ref[...]
        cn_ref[0] = x[tt - (CONV_W - 1):, :]


def _rglru(proj3, conv_state, h0, cw, cb, wa, ba, wx, bx, lam, l, *, tt):
    bsz, seq, _ = proj3.shape
    n_t = seq // tt
    col = lambda j: pl.BlockSpec((1, tt, D_GROUP), lambda b, t: (b, t, j))
    vec = pl.BlockSpec((None, 1, D_GROUP), lambda b, t: (l, 0, 0))
    sq = pl.BlockSpec((None, D_GROUP, D_GROUP), lambda b, t: (l, 0, 0))
    cs = pl.BlockSpec((1, CONV_W - 1, D_GROUP), lambda b, t: (b, 0, 0))
    hs = pl.BlockSpec((1, 1, D_GROUP), lambda b, t: (b, 0, 0))
    return pl.pallas_call(
        functools.partial(_lru_kernel, tt=tt, n_t=n_t),
        grid=(bsz, n_t),
        in_specs=[col(4), col(5), cs, hs,
                  pl.BlockSpec((None, CONV_W, D_GROUP), lambda b, t: (l, 0, 0)), vec, sq, vec, sq, vec, vec],
        out_specs=[pl.BlockSpec((1, tt, D_GROUP), lambda b, t: (b, t, 0)), hs, cs],
        out_shape=[jax.ShapeDtypeStruct((bsz, seq, D_GROUP), BF16),
                   jax.ShapeDtypeStruct((bsz, 1, D_GROUP), F32),
                   jax.ShapeDtypeStruct((bsz, CONV_W - 1, D_GROUP), F32)],
        scratch_shapes=[pltpu.VMEM((tt + PAD, D_GROUP), F32), pltpu.VMEM((tt, D_GROUP), F32),
                        pltpu.VMEM((tt, D_GROUP), F32), pltpu.VMEM((tt, D_GROUP), F32),
                        pltpu.VMEM((1, D_GROUP), F32)],
        compiler_params=_params("parallel", "arbitrary"),
        name="lru",
    )(proj3, proj3, conv_state, h0, cw, cb, wa, ba, wx, bx, lam)


def _sg_kernel(u_ref, v_ref, lg_ref, lb_ref, ws_ref, bs_ref, o_ref, vn_ref, *, c):
    u = _gelu(u_ref[0])
    vn = _layer_norm(_gelu(v_ref[0]), lg_ref[...], lb_ref[...])
    vn_ref[0] = vn
    row = lax.broadcasted_iota(jnp.int32, (c, c), 0)
    col = lax.broadcasted_iota(jnp.int32, (c, c), 1)
    mask = (col // CHUNK) <= (row // CHUNK)
    for h in range(N_HEADS):
        lanes = slice(h * D_HEAD, (h + 1) * D_HEAD)
        w = jnp.where(mask, ws_ref[h], 0.0)
        s = _mm(w, vn[:, lanes]) + bs_ref[:, h:h + 1]
        o_ref[0, :, lanes] = (u[:, lanes] * s).astype(o_ref.dtype)


def _spatial_gate(proj3, lg, lb, ws, bs_t, l):
    bsz, seq, _ = proj3.shape
    c = min(SG_CHUNK, seq)
    col = lambda j: pl.BlockSpec((1, c, D_GROUP), lambda b, t: (b, t, j))
    vec = pl.BlockSpec((None, 1, D_GROUP), lambda b, t: (l, 0, 0))
    out = pl.BlockSpec((1, c, D_GROUP), lambda b, t: (b, t, 0))
    return pl.pallas_call(
        functools.partial(_sg_kernel, c=c),
        grid=(bsz, seq // c),
        in_specs=[col(6), col(7), vec, vec,
                  pl.BlockSpec((None, N_HEADS, c, c), lambda b, t: (l, 0, 0, 0)),
                  pl.BlockSpec((None, c, N_HEADS), lambda b, t: (l, 0, 0))],
        out_specs=[out, out],
        out_shape=[jax.ShapeDtypeStruct((bsz, seq, D_GROUP), BF16),
                   jax.ShapeDtypeStruct((bsz, seq, D_GROUP), F32)],
        compiler_params=_params("parallel", "parallel"),
        name="sg",
    )(proj3, proj3, lg, lb, ws, bs_t)


INV_BLOCK = 16


def _bmm(a, b):
    return jnp.einsum('nij,njk->nik', a.astype(BF16), b.astype(BF16), preferred_element_type=F32)


def _bmm_nt(a, b):
    return jnp.einsum('nid,njd->nij', a.astype(BF16), b.astype(BF16), preferred_element_type=F32)


def _unit_lower_inverse(mats, c):
    row = lax.broadcasted_iota(jnp.int32, (c, c), 0)
    col = lax.broadcasted_iota(jnp.int32, (c, c), 1)
    eye = (row == col).astype(F32)
    same = row // INV_BLOCK == col // INV_BLOCK
    ps = [jnp.where(same, a, 0.0) for a in mats]
    xs = [eye - p for p in ps]
    ps = [_bmm(p, p) for p in ps]
    n_sq = int(math.log2(INV_BLOCK)) - 1
    for step in range(n_sq):
        last = step == n_sq - 1
        lhs = xs if last else [jnp.concatenate([x, p], axis=1) for x, p in zip(xs, ps)]
        prod = [_bmm(l, p) for l, p in zip(lhs, ps)]
        xs = [x + r[:, :c, :] for x, r in zip(xs, prod)]
        if not last:
            ps = [r[:, c:, :] for r in prod]
    size = INV_BLOCK
    while size < c:
        big = 2 * size
        sel = (row // big == col // big) & (row // size != col // size)
        offs = [jnp.where(sel, a, 0.0) for a in mats]
        tmp = [_bmm(o, x) for o, x in zip(offs, xs)]
        xs = [x - _bmm(x, t) for x, t in zip(xs, tmp)]
        size = big
    return xs


def _dn_kernel(q_ref, k_ref, v_ref, z_ref, ab_ref, cs_ref, cw_ref, al_ref, dt_ref, gn_ref, s0_ref,
               o_ref, sf_ref, cn_ref, ext_ref, s_scr, u_scr, wq_scr, attn_scr, kdt_scr, egl_scr,
               *, c, tt, n_t):
    t = pl.program_id(1)
    n = tt // c
    heads = range(N_HEADS)

    @pl.when(t == 0)
    def _():
        for j in range(3):
            ext_ref[j, PAD - (CONV_W - 1):PAD, :] = cs_ref[0, :, j * D_GROUP:(j + 1) * D_GROUP]
        s_scr[...] = s0_ref[0]

    raw = (q_ref, k_ref, v_ref)
    qf, kf, vf = [_silu(_conv_block(ext_ref.at[j], raw[j][0], cw_ref.at[:, j * D_GROUP:(j + 1) * D_GROUP], tt))
                  for j in range(3)]

    row = lax.broadcasted_iota(jnp.int32, (c, c), 0)
    col = lax.broadcasted_iota(jnp.int32, (c, c), 1)
    tri = row >= col
    strict = row > col
    tri_f = tri.astype(F32)
    scale = D_HEAD ** -0.5

    ab = ab_ref[0]
    g_all = -jnp.exp(al_ref[...]) * jax.nn.softplus(ab + dt_ref[...])
    beta_all = jax.nn.sigmoid(ab)
    gc_all = jnp.concatenate(
        [jnp.dot(tri_f, g_all[i * c:(i + 1) * c], preferred_element_type=F32, precision=lax.Precision.HIGHEST)
         for i in range(n)], axis=0)
    gc_t = gc_all.T

    def l2n(x):
        return x * lax.rsqrt(jnp.sum(x * x, -1, keepdims=True) + 1e-6)

    def split(x, h):
        return x[:, h * D_HEAD:(h + 1) * D_HEAD].reshape(n, c, D_HEAD)

    q3 = [l2n(split(qf, h)) * scale for h in heads]
    k3 = [l2n(split(kf, h)) for h in heads]
    v3 = [split(vf, h) for h in heads]
    beta = [beta_all[:, N_HEADS + h:N_HEADS + h + 1].reshape(n, c, 1) for h in heads]
    gcol = [gc_all[:, h:h + 1].reshape(n, c, 1) for h in heads]
    grow = [jnp.stack([gc_t[h:h + 1, i * c:(i + 1) * c] for i in range(n)]) for h in heads]
    glast = [g[:, c - 1:c, :] for g in gcol]
    decay = [jnp.where(tri, jnp.exp(gc - gr), 0.0) for gc, gr in zip(gcol, grow)]
    egc = [jnp.exp(g) for g in gcol]
    kb = [k * b for k, b in zip(k3, beta)]
    vb = [v * b for v, b in zip(v3, beta)]
    a_mat = [jnp.where(strict, _bmm_nt(x, k) * d, 0.0) for x, k, d in zip(kb, k3, decay)]
    tinv = _unit_lower_inverse(a_mat, c)
    uw = [_bmm(ti, jnp.concatenate([x, y * e], axis=-1)) for ti, x, y, e in zip(tinv, vb, kb, egc)]
    attn = [_bmm_nt(q, k) * d for q, k, d in zip(q3, k3, decay)]
    for h in heads:
        u_scr[h] = uw[h][:, :, :D_HEAD]
        wq_scr[h] = jnp.concatenate([uw[h][:, :, D_HEAD:], q3[h] * egc[h]], axis=1).astype(BF16)
        attn_scr[h] = attn[h].astype(BF16)
        kdt_scr[h] = jnp.swapaxes(k3[h] * jnp.exp(glast[h] - gcol[h]), 1, 2).astype(BF16)
        egl_scr[h] = jnp.broadcast_to(jnp.exp(glast[h]), (n, 1, D_HEAD))

    def chunk(i, carry):
        rows = pl.ds(pl.multiple_of(i * c, c), c)
        s = [s_scr[h] for h in heads]
        r = [_mm(wq_scr[h, i], s[h]) for h in heads]
        v_new = [u_scr[h, i] - r[h][:c] for h in heads]
        for h in heads:
            s_scr[h] = s[h] * egl_scr[h, i] + _mm(kdt_scr[h, i], v_new[h])
        o = [r[h][c:] + _mm(attn_scr[h, i], v_new[h]) for h in heads]
        for h in heads:
            lanes = slice(h * D_HEAD, (h + 1) * D_HEAD)
            on = o[h] * lax.rsqrt(jnp.mean(o[h] * o[h], -1, keepdims=True) + EPS) * gn_ref[...]
            o_ref[0, rows, lanes] = (on * _silu(z_ref[0, rows, lanes])).astype(o_ref.dtype)
        return carry

    lax.fori_loop(0, n, chunk, 0)

    @pl.when(t == n_t - 1)
    def _():
        sf_ref[0] = s_scr[...]
        for j in range(3):
            cn_ref[0, :, j * D_GROUP:(j + 1) * D_GROUP] = raw[j][0, tt - (CONV_W - 1):, :]


def _deltanet(proj3, ab3, conv_state, state0, cw, a_log, dt_bias, gn, l, *, tt):
    bsz, seq, _ = proj3.shape
    c = min(CHUNK, seq)
    n_t = seq // tt
    n = tt // c
    col = lambda j: pl.BlockSpec((1, tt, D_GROUP), lambda b, t: (b, t, j))
    st = pl.BlockSpec((1, N_HEADS, D_HEAD, D_HEAD), lambda b, t: (b, 0, 0, 0))
    cs = pl.BlockSpec((1, CONV_W - 1, 3 * D_GROUP), lambda b, t: (b, 0, 0))
    lane_row = pl.BlockSpec((None, 1, 128), lambda b, t: (l, 0, 0))
    return pl.pallas_call(
        functools.partial(_dn_kernel, c=c, tt=tt, n_t=n_t),
        grid=(bsz, n_t),
        in_specs=[col(8), col(9), col(10), col(11),
                  pl.BlockSpec((1, tt, 128), lambda b, t: (b, t, 0)),
                  cs,
                  pl.BlockSpec((None, CONV_W, 3 * D_GROUP), lambda b, t: (l, 0, 0)),
                  lane_row, lane_row, lane_row,
                  st],
        out_specs=[pl.BlockSpec((1, tt, D_GROUP), lambda b, t: (b, t, 0)), st, cs],
        out_shape=[jax.ShapeDtypeStruct((bsz, seq, D_GROUP), BF16),
                   jax.ShapeDtypeStruct(state0.shape, F32),
                   jax.ShapeDtypeStruct((bsz, CONV_W - 1, 3 * D_GROUP), F32)],
        scratch_shapes=[pltpu.VMEM((3, tt + PAD, D_GROUP), F32),
                        pltpu.VMEM((N_HEADS, D_HEAD, D_HEAD), F32),
                        pltpu.VMEM((N_HEADS, n, c, D_HEAD), F32),
                        pltpu.VMEM((N_HEADS, n, 2 * c, D_HEAD), BF16),
                        pltpu.VMEM((N_HEADS, n, c, c), BF16),
                        pltpu.VMEM((N_HEADS, n, D_HEAD, c), BF16),
                        pltpu.VMEM((N_HEADS, n, 1, D_HEAD), F32)],
        compiler_params=_params("parallel", "arbitrary"),
        name="dn",
    )(proj3, proj3, proj3, proj3, ab3, conv_state, cw, a_log, dt_bias, gn, state0)


def _mixout_kernel(a_ref, b_ref, c_ref, d_ref, x_ref, w_ref, g_ref, bb_ref, o_ref):
    acc = jnp.dot(a_ref[...], w_ref[0:D_GROUP, :], preferred_element_type=F32)
    acc += jnp.dot(b_ref[...], w_ref[D_GROUP:2 * D_GROUP, :], preferred_element_type=F32)
    acc += jnp.dot(c_ref[...], w_ref[2 * D_GROUP:3 * D_GROUP, :], preferred_element_type=F32)
    acc += jnp.dot(d_ref[...], w_ref[3 * D_GROUP:4 * D_GROUP, :], preferred_element_type=F32)
    o_ref[...] = _layer_norm(ALPHA * x_ref[...] + acc, g_ref[...], bb_ref[...])


def _mixout(a, b, c, d, x, w, g, bb, l, *, tm):
    m, dm = x.shape
    part = pl.BlockSpec((tm, D_GROUP), lambda i: (i, 0))
    vec = pl.BlockSpec((None, 1, dm), lambda i: (l, 0, 0))
    return pl.pallas_call(
        _mixout_kernel,
        grid=(m // tm,),
        in_specs=[part, part, part, part,
                  pl.BlockSpec((tm, dm), lambda i: (i, 0)),
                  pl.BlockSpec((None, 4 * D_GROUP, dm), lambda i: (l, 0, 0)),
                  vec, vec],
        out_specs=pl.BlockSpec((tm, dm), lambda i: (i, 0)),
        out_shape=jax.ShapeDtypeStruct((m, dm), F32),
        compiler_params=_params("parallel"),
        name="mixout",
    )(a, b, c, d, x, w, g, bb)


def _block_diag(w):
    dep, h, n, _ = w.shape
    eye = jnp.eye(h, dtype=w.dtype)
    return (eye[None, :, None, :, None] * w[:, :, :, None, :]).reshape(dep, h * n, h * n)


def _prep(p):
    row = lambda v: v.reshape(DEPTH, 1, -1)
    pad128 = lambda v: jnp.pad(row(v), ((0, 0), (0, 0), (0, 128 - v.shape[-1])))
    q = dict(p)
    for name in ('ffn1_w_in', 'ffn1_w_out', 'ffn2_w_in', 'ffn2_w_out', 'w_mix_out'):
        q[name] = p[name].astype(BF16)
    q['w_mix_ab'] = jnp.pad(p['w_mix_in'][:, :, 12 * D_GROUP:], ((0, 0), (0, 0), (0, 128 - 2 * N_HEADS))).astype(BF16)
    for name in ('ln1_g', 'ln1_b', 'ln2_g', 'ln2_b', 'ln3_g', 'ln3_b', 'ret_norm_g', 'lru_conv_b',
                 'lru_b_a', 'lru_b_x', 'lru_lam', 'sg_ln_g', 'sg_ln_b', 'dn_norm_g'):
        q[name] = row(p[name])
    q['lru_w_a'] = _block_diag(p['lru_w_a']).astype(BF16)
    q['lru_w_x'] = _block_diag(p['lru_w_x']).astype(BF16)
    q['dn_a_log'] = pad128(p['dn_a_log'])
    q['dn_dt_bias'] = pad128(p['dn_dt_bias'])
    return q


def _run_trunk(x3, pos, states, p, *, tm, tf, tt):
    bsz, seq, dm = x3.shape
    x = x3.reshape(bsz * seq, dm)
    c_sg = min(SG_CHUNK, seq)
    sg_w = p['sg_w'][:, :, :c_sg, :c_sg]
    sg_b_t = jnp.swapaxes(p['sg_b'][:, :, :c_sg], 1, 2)
    new_states = []
    for l in range(DEPTH):
        s_ret, s_lru, s_lru_conv, s_dn, s_dn_conv = states[l]
        x = _ffn(x, p['ffn1_w_in'], p['ffn1_w_out'], p['ln1_g'], p['ln1_b'], l, tm=tm, tf=tf)
        proj, ab = _proj(x, p['w_mix_in'], p['w_mix_ab'], l, tm=tm, tn=D_GROUP)
        proj3 = proj.reshape(bsz, seq, -1)
        ab3 = ab.reshape(bsz, seq, 128)
        out_a, ret_new = _retention(proj3, s_ret, p['ret_norm_g'], pos, l, tt=tt)
        out_b, lru_h_new, lru_conv_new = _rglru(
            proj3, s_lru_conv, s_lru.reshape(bsz, 1, D_GROUP), p['lru_conv_w'], p['lru_conv_b'],
            p['lru_w_a'], p['lru_b_a'], p['lru_w_x'], p['lru_b_x'], p['lru_lam'], l, tt=tt)
        out_c, sg_v = _spatial_gate(proj3, p['sg_ln_g'], p['sg_ln_b'], sg_w, sg_b_t, l)
        out_d, dn_new, dn_conv_new = _deltanet(proj3, ab3, s_dn_conv, s_dn, p['dn_conv_w'],
                                               p['dn_a_log'], p['dn_dt_bias'], p['dn_norm_g'], l, tt=tt)
        flat = lambda o: o.reshape(bsz * seq, D_GROUP)
        x = _mixout(flat(out_a), flat(out_b), flat(out_c), flat(out_d), x, p['w_mix_out'],
                    p['ln2_g'], p['ln2_b'], l, tm=min(tm, 512))
        x = _ffn(x, p['ffn2_w_in'], p['ffn2_w_out'], p['ln3_g'], p['ln3_b'], l, tm=tm, tf=tf)
        new_states.append((ret_new, lru_h_new.reshape(bsz, D_GROUP), lru_conv_new, dn_new, dn_conv_new, sg_v))
    return x.reshape(bsz, seq, dm), new_states


def kernel(x_prompt, x_sample, state_ret, state_lru_h, state_lru_conv, state_dn, state_dn_conv, ffn1_w_in, ffn1_w_out, ln1_g, ln1_b, w_mix_in, ret_norm_g, lru_conv_w, lru_conv_b, lru_w_a, lru_b_a, lru_w_x, lru_b_x, lru_lam, sg_ln_g, sg_ln_b, sg_w, sg_b, dn_conv_w, dn_a_log, dn_dt_bias, dn_norm_g, w_mix_out, ln2_g, ln2_b, ffn2_w_in, ffn2_w_out, ln3_g, ln3_b):
    weights = dict(ffn1_w_in=ffn1_w_in, ffn1_w_out=ffn1_w_out, ln1_g=ln1_g, ln1_b=ln1_b, w_mix_in=w_mix_in,
                   ret_norm_g=ret_norm_g, lru_conv_w=lru_conv_w, lru_conv_b=lru_conv_b, lru_w_a=lru_w_a,
                   lru_b_a=lru_b_a, lru_w_x=lru_w_x, lru_b_x=lru_b_x, lru_lam=lru_lam, sg_ln_g=sg_ln_g,
                   sg_ln_b=sg_ln_b, sg_w=sg_w, sg_b=sg_b, dn_conv_w=dn_conv_w, dn_a_log=dn_a_log,
                   dn_dt_bias=dn_dt_bias, dn_norm_g=dn_norm_g, w_mix_out=w_mix_out, ln2_g=ln2_g, ln2_b=ln2_b,
                   ffn2_w_in=ffn2_w_in, ffn2_w_out=ffn2_w_out, ln3_g=ln3_g, ln3_b=ln3_b)
    params = _prep(weights)

    bp, lp, _ = x_prompt.shape
    bs, ls, _ = x_sample.shape
    zeros = lambda *shape: jnp.zeros(shape, F32)
    prompt_init = [(zeros(bp, N_HEADS, D_HEAD, D_HEAD), zeros(bp, D_GROUP), zeros(bp, CONV_W - 1, D_GROUP),
                    zeros(bp, N_HEADS, D_HEAD, D_HEAD), zeros(bp, CONV_W - 1, 3 * D_GROUP))
                   for _ in range(DEPTH)]
    sample_init = [(state_ret[l], state_lru_h[l], state_lru_conv[l], state_dn[l], state_dn_conv[l])
                   for l in range(DEPTH)]

    y_prompt, st_p = _run_trunk(x_prompt, np.arange(lp), prompt_init, params, tm=1024, tf=512, tt=min(lp, 256))
    y_sample, st_s = _run_trunk(x_sample, PAST_LEN + np.arange(ls), sample_init, params,
                                tm=bs * ls, tf=512, tt=ls)

    stack = lambda sts, i: jnp.stack([s[i] for s in sts])
    return (y_prompt, y_sample,
            stack(st_p, 0), stack(st_p, 1), stack(st_p, 2), stack(st_p, 3), stack(st_p, 4),
            stack(st_s, 0), stack(st_s, 1), stack(st_s, 2), stack(st_s, 3), stack(st_s, 4), stack(st_s, 5))
```

```python
import functools
import math

import numpy as np
import jax
import jax.numpy as jnp
from jax import lax
from jax.experimental import pallas as pl
from jax.experimental.pallas import tpu as pltpu

F32 = jnp.float32
BF16 = jnp.bfloat16

DEPTH = 2
PAST_LEN = 4096
CHUNK = 64
D_GROUP = 512
N_HEADS = 4
D_HEAD = D_GROUP // N_HEADS
H_LRU = 8
LRU_C = 8.0
CONV_W = 4
SG_CHUNK = 128
ROPE_BASE = 10000.0
ALPHA = (2.0 * DEPTH) ** 0.25
EPS = 1e-5

FFN_TF = 512
VMEM_LIMIT = 58 * 1024 * 1024


def _params(*sem):
    return pltpu.CompilerParams(dimension_semantics=sem, vmem_limit_bytes=VMEM_LIMIT)


def _mm(a, b):
    return jnp.dot(a.astype(BF16), b.astype(BF16), preferred_element_type=F32)


def _bmm(a, b):
    return jnp.einsum('nij,njk->nik', a.astype(BF16), b.astype(BF16), preferred_element_type=F32)


def _bmm_nt(a, b):
    return jnp.einsum('nid,njd->nij', a.astype(BF16), b.astype(BF16), preferred_element_type=F32)


def _layer_norm(y, g, b):
    mu = jnp.mean(y, -1, keepdims=True)
    yc = y - mu
    var = jnp.mean(yc * yc, -1, keepdims=True)
    return yc * lax.rsqrt(var + EPS) * g + b


def _silu(x):
    return x * jax.nn.sigmoid(x)


def _gelu(x):
    return jax.nn.gelu(x, approximate=True)


def _ffn_kernel(x_ref, wg_ref, wu_ref, wo_ref, g_ref, b_ref, o_ref, xb_ref, *, nf):
    f = pl.program_id(1)

    @pl.when(f == 0)
    def _():
        xb_ref[...] = x_ref[...].astype(BF16)
        o_ref[...] = jnp.zeros_like(o_ref)

    xb = xb_ref[...]
    gate = jnp.dot(xb, wg_ref[...], preferred_element_type=F32)
    up = jnp.dot(xb, wu_ref[...], preferred_element_type=F32)
    h = (_silu(gate) * up).astype(BF16)
    o_ref[...] += jnp.dot(h, wo_ref[...], preferred_element_type=F32)

    @pl.when(f == nf - 1)
    def _():
        y = ALPHA * x_ref[...] + 0.5 * o_ref[...]
        o_ref[...] = _layer_norm(y, g_ref[...], b_ref[...])


def _ffn(x, w_in, w_out, g, b, l, *, tm):
    m, d = x.shape
    tf = w_in.shape[-1]
    nf = w_in.shape[1] // 2
    vec = pl.BlockSpec((None, 1, d), lambda i, f: (l, 0, 0))
    return pl.pallas_call(
        functools.partial(_ffn_kernel, nf=nf),
        grid=(m // tm, nf),
        in_specs=[
            pl.BlockSpec((tm, d), lambda i, f: (i, 0), pipeline_mode=pl.Buffered(1)),
            pl.BlockSpec((None, None, d, tf), lambda i, f: (l, f, 0, 0)),
            pl.BlockSpec((None, None, d, tf), lambda i, f: (l, f + nf, 0, 0)),
            pl.BlockSpec((None, tf, d), lambda i, f: (l, f, 0)),
            vec, vec,
        ],
        out_specs=pl.BlockSpec((tm, d), lambda i, f: (i, 0)),
        out_shape=jax.ShapeDtypeStruct((m, d), F32),
        scratch_shapes=[pltpu.VMEM((tm, d), BF16)],
        compiler_params=_params("parallel", "arbitrary"),
        name="ffn",
    )(x, w_in, w_in, w_out, g, b)


def _proj_kernel(x_ref, w_ref, wab_ref, o_ref, ab_ref, xb_ref):
    j = pl.program_id(1)

    @pl.when(j == 0)
    def _():
        xb = x_ref[...].astype(BF16)
        xb_ref[...] = xb
        ab_ref[...] = jnp.dot(xb, wab_ref[...], preferred_element_type=F32)

    o_ref[...] = jnp.dot(xb_ref[...], w_ref[...], preferred_element_type=F32)


def _proj(x, w, wab, l, *, tm):
    m, d = x.shape
    tn = w.shape[-1]
    n = w.shape[1] * tn
    return pl.pallas_call(
        _proj_kernel,
        grid=(m // tm, n // tn),
        in_specs=[
            pl.BlockSpec((tm, d), lambda i, j: (i, 0)),
            pl.BlockSpec((None, None, d, tn), lambda i, j: (l, j, 0, 0)),
            pl.BlockSpec((None, d, 128), lambda i, j: (l, 0, 0)),
        ],
        out_specs=[
            pl.BlockSpec((tm, tn), lambda i, j: (i, j)),
            pl.BlockSpec((tm, 128), lambda i, j: (i, 0)),
        ],
        out_shape=[jax.ShapeDtypeStruct((m, n), F32), jax.ShapeDtypeStruct((m, 128), F32)],
        scratch_shapes=[pltpu.VMEM((tm, d), BF16)],
        compiler_params=_params("parallel", "arbitrary"),
        name="proj",
    )(x, w, wab)


def _ret_tables(pos, c):
    half = D_HEAD // 2
    inv = ROPE_BASE ** (-np.arange(half, dtype=np.float64) / half)
    ang = pos.astype(np.float64)[:, None] * inv[None, :]
    cos = np.concatenate([np.cos(ang), np.cos(ang)], -1)
    sin = np.concatenate([-np.sin(ang), np.sin(ang)], -1)
    log_g = np.log1p(-np.exp2(-5.0 - np.arange(N_HEADS, dtype=np.float64)))
    idx = np.arange(c, dtype=np.float64)
    diff = idx[:, None] - idx[None, :]
    scale = D_HEAD ** -0.5
    dmask = np.where(diff >= 0, np.exp(log_g[:, None, None] * np.maximum(diff, 0.0)), 0.0) * scale
    kdec = np.exp(log_g[:, None] * (c - 1.0 - idx)[None, :]) * scale
    qdec = np.exp(log_g[:, None] * (idx + 1.0)[None, :])
    kdec = np.broadcast_to(kdec[:, :, None], (N_HEADS, c, D_HEAD))
    qdec = np.broadcast_to(qdec[:, :, None], (N_HEADS, c, D_HEAD))
    g_c = np.exp(log_g * c)
    as32 = lambda a: jnp.asarray(np.ascontiguousarray(a), F32)
    return as32(cos), as32(sin), as32(dmask), as32(qdec), as32(kdec), [float(v) for v in g_c]


def _ret_kernel(q_ref, k_ref, v_ref, g_ref, cos_ref, sin_ref, dm_ref, qd_ref, kd_ref, gn_ref, s0_ref,
                o_ref, sf_ref, s_scr, sst_scr, *, c, n_chunks, g_c, n_t):
    t = pl.program_id(1)
    n = n_chunks
    heads = range(N_HEADS)
    lanes = [slice(h * D_HEAD, (h + 1) * D_HEAD) for h in heads]

    @pl.when(t == 0)
    def _():
        s_scr[...] = s0_ref[0]

    cos = cos_ref[...]
    sin = sin_ref[...]

    def rot(x):
        return (x * cos + pltpu.roll(x, D_HEAD // 2, 1) * sin).reshape(n, c, D_HEAD)

    q3 = [rot(q_ref[0, :, lanes[h]]) for h in heads]
    k3 = [rot(k_ref[0, :, lanes[h]]) for h in heads]
    v3 = [v_ref[0, :, lanes[h]].reshape(n, c, D_HEAD) for h in heads]
    scores = [_bmm_nt(q3[h], k3[h]) * dm_ref[h] for h in heads]
    o_intra = [_bmm(scores[h], v3[h]) for h in heads]
    kdt = [jnp.swapaxes(k3[h] * kd_ref[h], 1, 2) for h in heads]
    upd = [_bmm(kdt[h], v3[h]) for h in heads]
    for h in heads:
        s = s_scr[h]
        for i in range(n):
            sst_scr[h, i] = s.astype(BF16)
            s = g_c[h] * s + upd[h][i]
        s_scr[h] = s
    for h in heads:
        o = o_intra[h] + _bmm(q3[h] * qd_ref[h], sst_scr[h])
        mu = jnp.mean(o, -1, keepdims=True)
        oc = o - mu
        var = jnp.mean(oc * oc, -1, keepdims=True)
        on = (oc * lax.rsqrt(var + EPS)).reshape(n * c, D_HEAD) * gn_ref[:, lanes[h]]
        o_ref[0, :, lanes[h]] = (on * _silu(g_ref[0, :, lanes[h]])).astype(o_ref.dtype)

    @pl.when(t == n_t - 1)
    def _():
        sf_ref[0] = s_scr[...]


def _retention(proj3, state0, gn, pos, l, *, tt):
    bsz, seq, _ = proj3.shape
    c = min(CHUNK, seq)
    cos, sin, dmask, qdec, kdec, g_c = _ret_tables(pos, c)
    n_t = seq // tt
    col = lambda j: pl.BlockSpec((1, tt, D_GROUP), lambda b, t: (b, t, j))
    full3 = pl.BlockSpec((N_HEADS, c, c), lambda b, t: (0, 0, 0))
    full3d = pl.BlockSpec((N_HEADS, c, D_HEAD), lambda b, t: (0, 0, 0))
    st = pl.BlockSpec((1, N_HEADS, D_HEAD, D_HEAD), lambda b, t: (b, 0, 0, 0))
    return pl.pallas_call(
        functools.partial(_ret_kernel, c=c, n_chunks=tt // c, g_c=g_c, n_t=n_t),
        grid=(bsz, n_t),
        in_specs=[col(0), col(1), col(2), col(3),
                  pl.BlockSpec((tt, D_HEAD), lambda b, t: (t, 0)),
                  pl.BlockSpec((tt, D_HEAD), lambda b, t: (t, 0)),
                  full3, full3d, full3d,
                  pl.BlockSpec((None, 1, D_GROUP), lambda b, t: (l, 0, 0)),
                  st],
        out_specs=[pl.BlockSpec((1, tt, D_GROUP), lambda b, t: (b, t, 0)), st],
        out_shape=[jax.ShapeDtypeStruct((bsz, seq, D_GROUP), BF16),
                   jax.ShapeDtypeStruct(state0.shape, F32)],
        scratch_shapes=[pltpu.VMEM((N_HEADS, D_HEAD, D_HEAD), F32),
                        pltpu.VMEM((N_HEADS, tt // c, D_HEAD, D_HEAD), BF16)],
        compiler_params=_params("parallel", "arbitrary"),
        name="ret",
    )(proj3, proj3, proj3, proj3, cos, sin, dmask, qdec, kdec, gn, state0)


PAD = 8


def _conv_block(ext_ref, x, w_ref, tt):
    ext_ref[PAD:PAD + tt, :] = x
    out = x * w_ref[CONV_W - 1:CONV_W, :]
    for j in range(1, CONV_W):
        out = out + ext_ref[PAD - j:PAD - j + tt, :] * w_ref[CONV_W - 1 - j:CONV_W - j, :]
    ext_ref[PAD - (CONV_W - 1):PAD, :] = x[tt - (CONV_W - 1):, :]
    return out


def _lru_kernel(y_ref, x_ref, cs_ref, h0_ref, cw_ref, cb_ref, wa_ref, ba_ref, wx_ref, bx_ref, lam_ref,
                o_ref, hn_ref, cn_ref, ext_ref, a_scr, b_scr, h_scr, hc_ref, *, tt, n_t):
    t = pl.program_id(1)

    @pl.when(t == 0)
    def _():
        ext_ref[PAD - (CONV_W - 1):PAD, :] = cs_ref[0]
        hc_ref[...] = h0_ref[0]

    x = x_ref[0]
    xc = _conv_block(ext_ref, x, cw_ref, tt) + cb_ref[...]
    r = jax.nn.sigmoid(_mm(xc, wa_ref[...]) + ba_ref[...])
    i = jax.nn.sigmoid(_mm(xc, wx_ref[...]) + bx_ref[...])
    log_a = -LRU_C * r * jax.nn.softplus(-lam_ref[...])
    a_scr[...] = jnp.exp(log_a)
    th = jnp.tanh(log_a)
    one_minus_a2 = -2.0 * th / (1.0 - th)
    b_scr[...] = jnp.sqrt(one_minus_a2) * (i * xc)

    def step(s, h):
        row = pl.ds(s, 1)
        h = a_scr[row, :] * h + b_scr[row, :]
        h_scr[row, :] = h
        return h

    hc_ref[...] = lax.fori_loop(0, tt, step, hc_ref[...], unroll=8)
    o_ref[0] = (h_scr[...] * _gelu(y_ref[0])).astype(o_ref.dtype)

    @pl.when(t == n_t - 1)
    def _():
        hn_ref[0] = hc_ref[...]
        cn_ref[0] = x[tt - (CONV_W - 1):, :]


def _rglru(proj3, conv_state, h0, cw, cb, wa, ba, wx, bx, lam, l, *, tt):
    bsz, seq, _ = proj3.shape
    n_t = seq // tt
    col = lambda j: pl.BlockSpec((1, tt, D_GROUP), lambda b, t: (b, t, j))
    vec = pl.BlockSpec((None, 1, D_GROUP), lambda b, t: (l, 0, 0))
    sq = pl.BlockSpec((None, D_GROUP, D_GROUP), lambda b, t: (l, 0, 0))
    cs = pl.BlockSpec((1, CONV_W - 1, D_GROUP), lambda b, t: (b, 0, 0))
    hs = pl.BlockSpec((1, 1, D_GROUP), lambda b, t: (b, 0, 0))
    return pl.pallas_call(
        functools.partial(_lru_kernel, tt=tt, n_t=n_t),
        grid=(bsz, n_t),
        in_specs=[col(4), col(5), cs, hs,
                  pl.BlockSpec((None, CONV_W, D_GROUP), lambda b, t: (l, 0, 0)), vec, sq, vec, sq, vec, vec],
        out_specs=[pl.BlockSpec((1, tt, D_GROUP), lambda b, t: (b, t, 0)), hs, cs],
        out_shape=[jax.ShapeDtypeStruct((bsz, seq, D_GROUP), BF16),
                   jax.ShapeDtypeStruct((bsz, 1, D_GROUP), F32),
                   jax.ShapeDtypeStruct((bsz, CONV_W - 1, D_GROUP), F32)],
        scratch_shapes=[pltpu.VMEM((tt + PAD, D_GROUP), F32), pltpu.VMEM((tt, D_GROUP), F32),
                        pltpu.VMEM((tt, D_GROUP), F32), pltpu.VMEM((tt, D_GROUP), F32),
                        pltpu.VMEM((1, D_GROUP), F32)],
        compiler_params=_params("parallel", "arbitrary"),
        name="lru",
    )(proj3, proj3, conv_state, h0, cw, cb, wa, ba, wx, bx, lam)


def _sg_kernel(u_ref, v_ref, lg_ref, lb_ref, ws_ref, bs_ref, o_ref, *vn_ref, c, tt):
    u = _gelu(u_ref[0])
    vn = _layer_norm(_gelu(v_ref[0]), lg_ref[...], lb_ref[...])
    if vn_ref:
        vn_ref[0][0] = vn
    row = lax.broadcasted_iota(jnp.int32, (c, c), 0)
    col = lax.broadcasted_iota(jnp.int32, (c, c), 1)
    mask = (col // CHUNK) <= (row // CHUNK)
    for h in range(N_HEADS):
        lanes = slice(h * D_HEAD, (h + 1) * D_HEAD)
        w = jnp.where(mask, ws_ref[h], 0.0).astype(BF16)
        bias = bs_ref[:, h:h + 1]
        for i in range(tt // c):
            rows = slice(i * c, (i + 1) * c)
            s = _mm(w, vn[rows, lanes]) + bias
            o_ref[0, rows, lanes] = (u[rows, lanes] * s).astype(o_ref.dtype)


def _spatial_gate(proj3, lg, lb, ws, bs_t, l, *, tt, with_vn):
    bsz, seq, _ = proj3.shape
    c = min(SG_CHUNK, seq)
    col = lambda j: pl.BlockSpec((1, tt, D_GROUP), lambda b, t: (b, t, j))
    vec = pl.BlockSpec((None, 1, D_GROUP), lambda b, t: (l, 0, 0))
    out = pl.BlockSpec((1, tt, D_GROUP), lambda b, t: (b, t, 0))
    res = pl.pallas_call(
        functools.partial(_sg_kernel, c=c, tt=tt),
        grid=(bsz, seq // tt),
        in_specs=[col(6), col(7), vec, vec,
                  pl.BlockSpec((None, N_HEADS, c, c), lambda b, t: (l, 0, 0, 0)),
                  pl.BlockSpec((None, c, N_HEADS), lambda b, t: (l, 0, 0))],
        out_specs=[out, out] if with_vn else [out],
        out_shape=[jax.ShapeDtypeStruct((bsz, seq, D_GROUP), BF16)]
                  + ([jax.ShapeDtypeStruct((bsz, seq, D_GROUP), F32)] if with_vn else []),
        compiler_params=_params("parallel", "parallel"),
        name="sg",
    )(proj3, proj3, lg, lb, ws, bs_t)
    return (res[0], res[1]) if with_vn else (res[0], None)


INV_BLOCK = 16


def _unit_lower_inverse(mats, c):
    row = lax.broadcasted_iota(jnp.int32, (c, c), 0)
    col = lax.broadcasted_iota(jnp.int32, (c, c), 1)
    eye = (row == col).astype(F32)
    same = row // INV_BLOCK == col // INV_BLOCK
    ps = [jnp.where(same, a, 0.0) for a in mats]
    xs = [eye - p for p in ps]
    ps = [_bmm(p, p) for p in ps]
    n_sq = int(math.log2(INV_BLOCK)) - 1
    for step in range(n_sq):
        last = step == n_sq - 1
        lhs = xs if last else [jnp.concatenate([x, p], axis=1) for x, p in zip(xs, ps)]
        prod = [_bmm(l, p) for l, p in zip(lhs, ps)]
        xs = [x + r[:, :c, :] for x, r in zip(xs, prod)]
        if not last:
            ps = [r[:, c:, :] for r in prod]
    size = INV_BLOCK
    while size < c:
        big = 2 * size
        sel = (row // big == col // big) & (row // size != col // size)
        offs = [jnp.where(sel, a, 0.0) for a in mats]
        tmp = [_bmm(o, x) for o, x in zip(offs, xs)]
        xs = [x - _bmm(x, t) for x, t in zip(xs, tmp)]
        size = big
    return xs


def _dn_kernel(q_ref, k_ref, v_ref, z_ref, ab_ref, cs_ref, cw_ref, al_ref, dt_ref, gn_ref, s0_ref,
               o_ref, sf_ref, cn_ref, ext_ref, s_scr, u_scr, wq_scr, attn_scr, kdt_scr, egl_scr,
               *, c, tt, n_t):
    t = pl.program_id(1)
    n = tt // c
    heads = range(N_HEADS)

    @pl.when(t == 0)
    def _():
        for j in range(3):
            ext_ref[j, PAD - (CONV_W - 1):PAD, :] = cs_ref[0, :, j * D_GROUP:(j + 1) * D_GROUP]
        s_scr[...] = s0_ref[0]

    raw = (q_ref, k_ref, v_ref)
    qf, kf, vf = [_silu(_conv_block(ext_ref.at[j], raw[j][0], cw_ref.at[:, j * D_GROUP:(j + 1) * D_GROUP], tt))
                  for j in range(3)]

    row = lax.broadcasted_iota(jnp.int32, (c, c), 0)
    col = lax.broadcasted_iota(jnp.int32, (c, c), 1)
    tri = row >= col
    strict = row > col
    tri_f = tri.astype(F32)
    scale = D_HEAD ** -0.5

    ab = ab_ref[0]
    g_all = -jnp.exp(al_ref[...]) * jax.nn.softplus(ab + dt_ref[...])
    beta_all = jax.nn.sigmoid(ab)
    gc_all = jnp.concatenate(
        [jnp.dot(tri_f, g_all[i * c:(i + 1) * c], preferred_element_type=F32, precision=lax.Precision.HIGHEST)
         for i in range(n)], axis=0)
    gc_t = gc_all.T

    def l2n(x):
        return x * lax.rsqrt(jnp.sum(x * x, -1, keepdims=True) + 1e-6)

    def split(x, h):
        return x[:, h * D_HEAD:(h + 1) * D_HEAD].reshape(n, c, D_HEAD)

    q3 = [l2n(split(qf, h)) * scale for h in heads]
    k3 = [l2n(split(kf, h)) for h in heads]
    v3 = [split(vf, h) for h in heads]
    beta = [beta_all[:, N_HEADS + h:N_HEADS + h + 1].reshape(n, c, 1) for h in heads]
    gcol = [gc_all[:, h:h + 1].reshape(n, c, 1) for h in heads]
    grow = [jnp.stack([gc_t[h:h + 1, i * c:(i + 1) * c] for i in range(n)]) for h in heads]
    glast = [g[:, c - 1:c, :] for g in gcol]
    decay = [jnp.where(tri, jnp.exp(gc - gr), 0.0) for gc, gr in zip(gcol, grow)]
    egc = [jnp.exp(g) for g in gcol]
    kb = [k * b for k, b in zip(k3, beta)]
    vb = [v * b for v, b in zip(v3, beta)]
    a_mat = [jnp.where(strict, _bmm_nt(x, k) * d, 0.0) for x, k, d in zip(kb, k3, decay)]
    tinv = _unit_lower_inverse(a_mat, c)
    uw = [_bmm(ti, jnp.concatenate([x, y * e], axis=-1)) for ti, x, y, e in zip(tinv, vb, kb, egc)]
    attn = [_bmm_nt(q, k) * d for q, k, d in zip(q3, k3, decay)]
    for h in heads:
        u_scr[h] = uw[h][:, :, :D_HEAD]
        wq_scr[h] = jnp.concatenate([uw[h][:, :, D_HEAD:], q3[h] * egc[h]], axis=1).astype(BF16)
        attn_scr[h] = attn[h].astype(BF16)
        kdt_scr[h] = jnp.swapaxes(k3[h] * jnp.exp(glast[h] - gcol[h]), 1, 2).astype(BF16)
        egl_scr[h] = jnp.broadcast_to(jnp.exp(glast[h]), (n, 1, D_HEAD))

    def chunk(i, carry):
        rows = pl.ds(pl.multiple_of(i * c, c), c)
        s = [s_scr[h] for h in heads]
        r = [_mm(wq_scr[h, i], s[h]) for h in heads]
        v_new = [u_scr[h, i] - r[h][:c] for h in heads]
        for h in heads:
            s_scr[h] = s[h] * egl_scr[h, i] + _mm(kdt_scr[h, i], v_new[h])
        o = [r[h][c:] + _mm(attn_scr[h, i], v_new[h]) for h in heads]
        for h in heads:
            lanes = slice(h * D_HEAD, (h + 1) * D_HEAD)
            on = o[h] * lax.rsqrt(jnp.mean(o[h] * o[h], -1, keepdims=True) + EPS) * gn_ref[...]
            o_ref[0, rows, lanes] = (on * _silu(z_ref[0, rows, lanes])).astype(o_ref.dtype)
        return carry

    lax.fori_loop(0, n, chunk, 0)

    @pl.when(t == n_t - 1)
    def _():
        sf_ref[0] = s_scr[...]
        for j in range(3):
            cn_ref[0, :, j * D_GROUP:(j + 1) * D_GROUP] = raw[j][0, tt - (CONV_W - 1):, :]


def _deltanet(proj3, ab3, conv_state, state0, cw, a_log, dt_bias, gn, l, *, tt):
    bsz, seq, _ = proj3.shape
    c = min(CHUNK, seq)
    n_t = seq // tt
    n = tt // c
    col = lambda j: pl.BlockSpec((1, tt, D_GROUP), lambda b, t: (b, t, j))
    st = pl.BlockSpec((1, N_HEADS, D_HEAD, D_HEAD), lambda b, t: (b, 0, 0, 0))
    cs = pl.BlockSpec((1, CONV_W - 1, 3 * D_GROUP), lambda b, t: (b, 0, 0))
    lane_row = pl.BlockSpec((None, 1, 128), lambda b, t: (l, 0, 0))
    return pl.pallas_call(
        functools.partial(_dn_kernel, c=c, tt=tt, n_t=n_t),
        grid=(bsz, n_t),
        in_specs=[col(8), col(9), col(10), col(11),
                  pl.BlockSpec((1, tt, 128), lambda b, t: (b, t, 0)),
                  cs,
                  pl.BlockSpec((None, CONV_W, 3 * D_GROUP), lambda b, t: (l, 0, 0)),
                  lane_row, lane_row, lane_row,
                  st],
        out_specs=[pl.BlockSpec((1, tt, D_GROUP), lambda b, t: (b, t, 0)), st, cs],
        out_shape=[jax.ShapeDtypeStruct((bsz, seq, D_GROUP), BF16),
                   jax.ShapeDtypeStruct(state0.shape, F32),
                   jax.ShapeDtypeStruct((bsz, CONV_W - 1, 3 * D_GROUP), F32)],
        scratch_shapes=[pltpu.VMEM((3, tt + PAD, D_GROUP), F32),
                        pltpu.VMEM((N_HEADS, D_HEAD, D_HEAD), F32),
                        pltpu.VMEM((N_HEADS, n, c, D_HEAD), F32),
                        pltpu.VMEM((N_HEADS, n, 2 * c, D_HEAD), BF16),
                        pltpu.VMEM((N_HEADS, n, c, c), BF16),
                        pltpu.VMEM((N_HEADS, n, D_HEAD, c), BF16),
                        pltpu.VMEM((N_HEADS, n, 1, D_HEAD), F32)],
        compiler_params=_params("parallel", "arbitrary"),
        name="dn",
    )(proj3, proj3, proj3, proj3, ab3, conv_state, cw, a_log, dt_bias, gn, state0)


def _mixout_kernel(a_ref, b_ref, c_ref, d_ref, x_ref, w_ref, g_ref, bb_ref, o_ref):
    acc = jnp.dot(a_ref[...], w_ref[0:D_GROUP, :], preferred_element_type=F32)
    acc += jnp.dot(b_ref[...], w_ref[D_GROUP:2 * D_GROUP, :], preferred_element_type=F32)
    acc += jnp.dot(c_ref[...], w_ref[2 * D_GROUP:3 * D_GROUP, :], preferred_element_type=F32)
    acc += jnp.dot(d_ref[...], w_ref[3 * D_GROUP:4 * D_GROUP, :], preferred_element_type=F32)
    o_ref[...] = _layer_norm(ALPHA * x_ref[...] + acc, g_ref[...], bb_ref[...])


def _mixout(a, b, c, d, x, w, g, bb, l, *, tm):
    m, dm = x.shape
    part = pl.BlockSpec((tm, D_GROUP), lambda i: (i, 0))
    vec = pl.BlockSpec((None, 1, dm), lambda i: (l, 0, 0))
    return pl.pallas_call(
        _mixout_kernel,
        grid=(m // tm,),
        in_specs=[part, part, part, part,
                  pl.BlockSpec((tm, dm), lambda i: (i, 0)),
                  pl.BlockSpec((None, 4 * D_GROUP, dm), lambda i: (l, 0, 0), pipeline_mode=pl.Buffered(1)),
                  vec, vec],
        out_specs=pl.BlockSpec((tm, dm), lambda i: (i, 0)),
        out_shape=jax.ShapeDtypeStruct((m, dm), F32),
        compiler_params=_params("parallel"),
        name="mixout",
    )(a, b, c, d, x, w, g, bb)


def _block_diag(w):
    dep, h, n, _ = w.shape
    eye = jnp.eye(h, dtype=w.dtype)
    return (eye[None, :, None, :, None] * w[:, :, :, None, :]).reshape(dep, h * n, h * n)


def _prep(p):
    row = lambda v: v.reshape(DEPTH, 1, -1)
    pad128 = lambda v: jnp.pad(row(v), ((0, 0), (0, 0), (0, 128 - v.shape[-1])))
    q = dict(p)
    def col_tiles(w, t):
        dep, d, n = w.shape
        return jnp.transpose(w.reshape(dep, d, n // t, t), (0, 2, 1, 3)).astype(BF16)

    for name in ('ffn1_w_out', 'ffn2_w_out', 'w_mix_out'):
        q[name] = p[name].astype(BF16)
    q['ffn1_w_in'] = col_tiles(p['ffn1_w_in'], FFN_TF)
    q['ffn2_w_in'] = col_tiles(p['ffn2_w_in'], FFN_TF)
    q['w_mix_in'] = col_tiles(p['w_mix_in'][:, :, :12 * D_GROUP], D_GROUP)
    q['w_mix_ab'] = jnp.pad(p['w_mix_in'][:, :, 12 * D_GROUP:], ((0, 0), (0, 0), (0, 128 - 2 * N_HEADS))).astype(BF16)
    for name in ('ln1_g', 'ln1_b', 'ln2_g', 'ln2_b', 'ln3_g', 'ln3_b', 'ret_norm_g', 'lru_conv_b',
                 'lru_b_a', 'lru_b_x', 'lru_lam', 'sg_ln_g', 'sg_ln_b', 'dn_norm_g'):
        q[name] = row(p[name])
    q['lru_w_a'] = _block_diag(p['lru_w_a']).astype(BF16)
    q['lru_w_x'] = _block_diag(p['lru_w_x']).astype(BF16)
    q['dn_a_log'] = pad128(p['dn_a_log'])
    q['dn_dt_bias'] = pad128(p['dn_dt_bias'])
    return q


def _run_trunk(x3, pos, states, p, *, tm, tt, keep_sg_v):
    bsz, seq, dm = x3.shape
    x = x3.reshape(bsz * seq, dm)
    c_sg = min(SG_CHUNK, seq)
    sg_w = p['sg_w'][:, :, :c_sg, :c_sg]
    sg_b_t = jnp.swapaxes(p['sg_b'][:, :, :c_sg], 1, 2)
    new_states = []
    for l in range(DEPTH):
        s_ret, s_lru, s_lru_conv, s_dn, s_dn_conv = states[l]
        x = _ffn(x, p['ffn1_w_in'], p['ffn1_w_out'], p['ln1_g'], p['ln1_b'], l, tm=tm)
        proj, ab = _proj(x, p['w_mix_in'], p['w_mix_ab'], l, tm=tm)
        proj3 = proj.reshape(bsz, seq, -1)
        ab3 = ab.reshape(bsz, seq, 128)
        out_a, ret_new = _retention(proj3, s_ret, p['ret_norm_g'], pos, l, tt=tt)
        out_b, lru_h_new, lru_conv_new = _rglru(
            proj3, s_lru_conv, s_lru.reshape(bsz, 1, D_GROUP), p['lru_conv_w'], p['lru_conv_b'],
            p['lru_w_a'], p['lru_b_a'], p['lru_w_x'], p['lru_b_x'], p['lru_lam'], l, tt=tt)
        out_c, sg_v = _spatial_gate(proj3, p['sg_ln_g'], p['sg_ln_b'], sg_w, sg_b_t, l,
                                    tt=min(seq, 4 * SG_CHUNK), with_vn=keep_sg_v)
        out_d, dn_new, dn_conv_new = _deltanet(proj3, ab3, s_dn_conv, s_dn, p['dn_conv_w'],
                                               p['dn_a_log'], p['dn_dt_bias'], p['dn_norm_g'], l, tt=tt)
        flat = lambda o: o.reshape(bsz * seq, D_GROUP)
        x = _mixout(flat(out_a), flat(out_b), flat(out_c), flat(out_d), x, p['w_mix_out'],
                    p['ln2_g'], p['ln2_b'], l, tm=tm)
        x = _ffn(x, p['ffn2_w_in'], p['ffn2_w_out'], p['ln3_g'], p['ln3_b'], l, tm=tm)
        new_states.append((ret_new, lru_h_new.reshape(bsz, D_GROUP), lru_conv_new, dn_new, dn_conv_new, sg_v))
    return x.reshape(bsz, seq, dm), new_states


def kernel(x_prompt, x_sample, state_ret, state_lru_h, state_lru_conv, state_dn, state_dn_conv, ffn1_w_in, ffn1_w_out, ln1_g, ln1_b, w_mix_in, ret_norm_g, lru_conv_w, lru_conv_b, lru_w_a, lru_b_a, lru_w_x, lru_b_x, lru_lam, sg_ln_g, sg_ln_b, sg_w, sg_b, dn_conv_w, dn_a_log, dn_dt_bias, dn_norm_g, w_mix_out, ln2_g, ln2_b, ffn2_w_in, ffn2_w_out, ln3_g, ln3_b):
    weights = dict(ffn1_w_in=ffn1_w_in, ffn1_w_out=ffn1_w_out, ln1_g=ln1_g, ln1_b=ln1_b, w_mix_in=w_mix_in,
                   ret_norm_g=ret_norm_g, lru_conv_w=lru_conv_w, lru_conv_b=lru_conv_b, lru_w_a=lru_w_a,
                   lru_b_a=lru_b_a, lru_w_x=lru_w_x, lru_b_x=lru_b_x, lru_lam=lru_lam, sg_ln_g=sg_ln_g,
                   sg_ln_b=sg_ln_b, sg_w=sg_w, sg_b=sg_b, dn_conv_w=dn_conv_w, dn_a_log=dn_a_log,
                   dn_dt_bias=dn_dt_bias, dn_norm_g=dn_norm_g, w_mix_out=w_mix_out, ln2_g=ln2_g, ln2_b=ln2_b,
                   ffn2_w_in=ffn2_w_in, ffn2_w_out=ffn2_w_out, ln3_g=ln3_g, ln3_b=ln3_b)
    params = _prep(weights)

    bp, lp, _ = x_prompt.shape
    bs, ls, _ = x_sample.shape
    zeros = lambda *shape: jnp.zeros(shape, F32)
    prompt_init = [(zeros(bp, N_HEADS, D_HEAD, D_HEAD), zeros(bp, D_GROUP), zeros(bp, CONV_W - 1, D_GROUP),
                    zeros(bp, N_HEADS, D_HEAD, D_HEAD), zeros(bp, CONV_W - 1, 3 * D_GROUP))
                   for _ in range(DEPTH)]
    sample_init = [(state_ret[l], state_lru_h[l], state_lru_conv[l], state_dn[l], state_dn_conv[l])
                   for l in range(DEPTH)]

    y_prompt, st_p = _run_trunk(x_prompt, np.arange(lp), prompt_init, params, tm=1024, tt=min(lp, 256),
                                keep_sg_v=False)
    y_sample, st_s = _run_trunk(x_sample, PAST_LEN + np.arange(ls), sample_init, params,
                                tm=bs * ls, tt=ls, keep_sg_v=True)

    stack = lambda sts, i: jnp.stack([s[i] for s in sts])
    return (y_prompt, y_sample,
            stack(st_p, 0), stack(st_p, 1), stack(st_p, 2), stack(st_p, 3), stack(st_p, 4),
            stack(st_s, 0), stack(st_s, 1), stack(st_s, 2), stack(st_s, 3), stack(st_s, 4), stack(st_s, 5))
```

```python
import functools
import math

import numpy as np
import jax
import jax.numpy as jnp
from jax import lax
from jax.experimental import pallas as pl
from jax.experimental.pallas import tpu as pltpu

F32 = jnp.float32
BF16 = jnp.bfloat16

DEPTH = 2
PAST_LEN = 4096
CHUNK = 64
D_GROUP = 512
N_HEADS = 4
D_HEAD = D_GROUP // N_HEADS
H_LRU = 8
LRU_C = 8.0
CONV_W = 4
SG_CHUNK = 128
ROPE_BASE = 10000.0
ALPHA = (2.0 * DEPTH) ** 0.25
EPS = 1e-5

FFN_TF = 512
PROJ_TN = 1024
VMEM_LIMIT = 58 * 1024 * 1024


def _params(*sem):
    return pltpu.CompilerParams(dimension_semantics=sem, vmem_limit_bytes=VMEM_LIMIT)


def _mm(a, b):
    return jnp.dot(a.astype(BF16), b.astype(BF16), preferred_element_type=F32)


def _bmm(a, b):
    return jnp.einsum('nij,njk->nik', a.astype(BF16), b.astype(BF16), preferred_element_type=F32)


def _bmm_nt(a, b):
    return jnp.einsum('nid,njd->nij', a.astype(BF16), b.astype(BF16), preferred_element_type=F32)


def _layer_norm(y, g, b):
    mu = jnp.mean(y, -1, keepdims=True)
    yc = y - mu
    var = jnp.mean(yc * yc, -1, keepdims=True)
    return yc * lax.rsqrt(var + EPS) * g + b


def _silu(x):
    return x * jax.nn.sigmoid(x)


def _gelu(x):
    return jax.nn.gelu(x, approximate=True)


def _ffn_kernel(x_ref, wg_ref, wu_ref, wo_ref, g_ref, b_ref, *rest, nf, n_row, cast_next):
    if cast_next:
        ci_ref, co_ref, o_ref, cio_ref, coo_ref, xb_ref = rest
        cio_ref[...] = ci_ref[...].astype(BF16)
        coo_ref[...] = co_ref[...].astype(BF16)
    else:
        o_ref, xb_ref = rest
    f = pl.program_id(1)

    @pl.when(f == 0)
    def _():
        xb_ref[...] = x_ref[...].astype(BF16)
        o_ref[...] = jnp.zeros_like(o_ref)

    tr = o_ref.shape[0] // n_row
    for r in range(n_row):
        rows = slice(r * tr, (r + 1) * tr)
        xb = xb_ref[rows, :]
        gate = jnp.dot(xb, wg_ref[...], preferred_element_type=F32)
        up = jnp.dot(xb, wu_ref[...], preferred_element_type=F32)
        h = (_silu(gate) * up).astype(BF16)
        o_ref[rows, :] += jnp.dot(h, wo_ref[...], preferred_element_type=F32)

    @pl.when(f == nf - 1)
    def _():
        y = ALPHA * x_ref[...] + 0.5 * o_ref[...]
        o_ref[...] = _layer_norm(y, g_ref[...], b_ref[...])


def _ffn(x, w_in, w_out, g, b, l, *, tm, cast_next=None):
    m, d = x.shape
    d_ff = w_out.shape[0]
    tf = FFN_TF
    nf = d_ff // tf
    ni = m // tm
    vec = pl.BlockSpec((None, 1, d), lambda i, f: (l, 0, 0))
    in_specs = [
        pl.BlockSpec((tm, d), lambda i, f: (i, 0), pipeline_mode=pl.Buffered(1)),
        pl.BlockSpec((d, tf), lambda i, f: (0, f)),
        pl.BlockSpec((d, tf), lambda i, f: (0, f + nf)),
        pl.BlockSpec((tf, d), lambda i, f: (f, 0)),
        vec, vec,
    ]
    out_specs = [pl.BlockSpec((tm, d), lambda i, f: (i, 0))]
    out_shape = [jax.ShapeDtypeStruct((m, d), F32)]
    args = [x, w_in, w_in, w_out, g, b]
    if cast_next is not None:
        nw_in, nw_out, ln = cast_next
        ri, ci, ro = d // ni, 2 * d_ff // nf, d_ff // (ni * nf)
        assert ri * ni == d and ci * nf == 2 * d_ff and ro * ni * nf == d_ff
        in_specs += [pl.BlockSpec((None, ri, ci), lambda i, f: (ln, i, f)),
                     pl.BlockSpec((None, ro, d), lambda i, f: (ln, i * nf + f, 0))]
        out_specs += [pl.BlockSpec((ri, ci), lambda i, f: (i, f)),
                      pl.BlockSpec((ro, d), lambda i, f: (i * nf + f, 0))]
        out_shape += [jax.ShapeDtypeStruct((d, 2 * d_ff), BF16), jax.ShapeDtypeStruct((d_ff, d), BF16)]
        args += [nw_in, nw_out]
    res = pl.pallas_call(
        functools.partial(_ffn_kernel, nf=nf, n_row=2 if tm >= 1024 else 1, cast_next=cast_next is not None),
        grid=(ni, nf),
        in_specs=in_specs,
        out_specs=out_specs,
        out_shape=out_shape,
        scratch_shapes=[pltpu.VMEM((tm, d), BF16)],
        compiler_params=_params("parallel", "arbitrary"),
        name="ffn",
    )(*args)
    return res[0], (tuple(res[1:]) if cast_next is not None else None)


def _proj_kernel(x_ref, w_ref, wab_ref, o_ref, ab_ref, xb_ref):
    j = pl.program_id(1)

    @pl.when(j == 0)
    def _():
        xb = x_ref[...].astype(BF16)
        xb_ref[...] = xb
        ab_ref[...] = jnp.dot(xb, wab_ref[...], preferred_element_type=F32)

    o_ref[...] = jnp.dot(xb_ref[...], w_ref[...], preferred_element_type=F32)


def _proj(x, w, wab, l, *, tm):
    m, d = x.shape
    tn = PROJ_TN
    n = w.shape[-1]
    return pl.pallas_call(
        _proj_kernel,
        grid=(m // tm, n // tn),
        in_specs=[
            pl.BlockSpec((tm, d), lambda i, j: (i, 0)),
            pl.BlockSpec((None, d, tn), lambda i, j: (l, 0, j)),
            pl.BlockSpec((None, d, 128), lambda i, j: (l, 0, 0)),
        ],
        out_specs=[
            pl.BlockSpec((tm, tn), lambda i, j: (i, j)),
            pl.BlockSpec((tm, 128), lambda i, j: (i, 0)),
        ],
        out_shape=[jax.ShapeDtypeStruct((m, n), F32), jax.ShapeDtypeStruct((m, 128), F32)],
        scratch_shapes=[pltpu.VMEM((tm, d), BF16)],
        compiler_params=_params("parallel", "arbitrary"),
        name="proj",
    )(x, w, wab)


def _ret_tables(pos, c):
    half = D_HEAD // 2
    inv = ROPE_BASE ** (-np.arange(half, dtype=np.float64) / half)
    ang = pos.astype(np.float64)[:, None] * inv[None, :]
    cos = np.concatenate([np.cos(ang), np.cos(ang)], -1)
    sin = np.concatenate([-np.sin(ang), np.sin(ang)], -1)
    log_g = np.log1p(-np.exp2(-5.0 - np.arange(N_HEADS, dtype=np.float64)))
    idx = np.arange(c, dtype=np.float64)
    diff = idx[:, None] - idx[None, :]
    scale = D_HEAD ** -0.5
    dmask = np.where(diff >= 0, np.exp(log_g[:, None, None] * np.maximum(diff, 0.0)), 0.0) * scale
    kdec = np.exp(log_g[:, None] * (c - 1.0 - idx)[None, :]) * scale
    qdec = np.exp(log_g[:, None] * (idx + 1.0)[None, :])
    kdec = np.broadcast_to(kdec[:, :, None], (N_HEADS, c, D_HEAD))
    qdec = np.broadcast_to(qdec[:, :, None], (N_HEADS, c, D_HEAD))
    g_c = np.exp(log_g * c)
    as32 = lambda a: jnp.asarray(np.ascontiguousarray(a), F32)
    return as32(cos), as32(sin), as32(dmask), as32(qdec), as32(kdec), [float(v) for v in g_c]


def _ret_kernel(q_ref, k_ref, v_ref, g_ref, cos_ref, sin_ref, dm_ref, qd_ref, kd_ref, gn_ref, s0_ref,
                o_ref, sf_ref, s_scr, sst_scr, *, c, n_chunks, g_c, n_t):
    t = pl.program_id(1)
    n = n_chunks
    heads = range(N_HEADS)
    lanes = [slice(h * D_HEAD, (h + 1) * D_HEAD) for h in heads]

    @pl.when(t == 0)
    def _():
        s_scr[...] = s0_ref[0]

    cos = cos_ref[...]
    sin = sin_ref[...]

    def rot(x):
        return (x * cos + pltpu.roll(x, D_HEAD // 2, 1) * sin).reshape(n, c, D_HEAD)

    q3 = [rot(q_ref[0, :, lanes[h]]) for h in heads]
    k3 = [rot(k_ref[0, :, lanes[h]]) for h in heads]
    v3 = [v_ref[0, :, lanes[h]].reshape(n, c, D_HEAD) for h in heads]
    scores = [_bmm_nt(q3[h], k3[h]) * dm_ref[h] for h in heads]
    o_intra = [_bmm(scores[h], v3[h]) for h in heads]
    kdt = [jnp.swapaxes(k3[h] * kd_ref[h], 1, 2) for h in heads]
    upd = [_bmm(kdt[h], v3[h]) for h in heads]
    for h in heads:
        s = s_scr[h]
        for i in range(n):
            sst_scr[h, i] = s.astype(BF16)
            s = g_c[h] * s + upd[h][i]
        s_scr[h] = s
    for h in heads:
        o = o_intra[h] + _bmm(q3[h] * qd_ref[h], sst_scr[h])
        mu = jnp.mean(o, -1, keepdims=True)
        oc = o - mu
        var = jnp.mean(oc * oc, -1, keepdims=True)
        on = (oc * lax.rsqrt(var + EPS)).reshape(n * c, D_HEAD) * gn_ref[:, lanes[h]]
        o_ref[0, :, lanes[h]] = (on * _silu(g_ref[0, :, lanes[h]])).astype(o_ref.dtype)

    @pl.when(t == n_t - 1)
    def _():
        sf_ref[0] = s_scr[...]


def _retention(proj3, state0, gn, pos, l, *, tt):
    bsz, seq, _ = proj3.shape
    c = min(CHUNK, seq)
    cos, sin, dmask, qdec, kdec, g_c = _ret_tables(pos, c)
    n_t = seq // tt
    col = lambda j: pl.BlockSpec((1, tt, D_GROUP), lambda b, t: (b, t, j))
    full3 = pl.BlockSpec((N_HEADS, c, c), lambda b, t: (0, 0, 0))
    full3d = pl.BlockSpec((N_HEADS, c, D_HEAD), lambda b, t: (0, 0, 0))
    st = pl.BlockSpec((1, N_HEADS, D_HEAD, D_HEAD), lambda b, t: (b, 0, 0, 0))
    return pl.pallas_call(
        functools.partial(_ret_kernel, c=c, n_chunks=tt // c, g_c=g_c, n_t=n_t),
        grid=(bsz, n_t),
        in_specs=[col(0), col(1), col(2), col(3),
                  pl.BlockSpec((tt, D_HEAD), lambda b, t: (t, 0)),
                  pl.BlockSpec((tt, D_HEAD), lambda b, t: (t, 0)),
                  full3, full3d, full3d,
                  pl.BlockSpec((None, 1, D_GROUP), lambda b, t: (l, 0, 0)),
                  st],
        out_specs=[pl.BlockSpec((1, tt, D_GROUP), lambda b, t: (b, t, 0)), st],
        out_shape=[jax.ShapeDtypeStruct((bsz, seq, D_GROUP), BF16),
                   jax.ShapeDtypeStruct(state0.shape, F32)],
        scratch_shapes=[pltpu.VMEM((N_HEADS, D_HEAD, D_HEAD), F32),
                        pltpu.VMEM((N_HEADS, tt // c, D_HEAD, D_HEAD), BF16)],
        compiler_params=_params("parallel", "arbitrary"),
        name="ret",
    )(proj3, proj3, proj3, proj3, cos, sin, dmask, qdec, kdec, gn, state0)


PAD = 8


def _conv_block(ext_ref, x, w_ref, tt):
    ext_ref[PAD:PAD + tt, :] = x
    out = x * w_ref[CONV_W - 1:CONV_W, :]
    for j in range(1, CONV_W):
        out = out + ext_ref[PAD - j:PAD - j + tt, :] * w_ref[CONV_W - 1 - j:CONV_W - j, :]
    ext_ref[PAD - (CONV_W - 1):PAD, :] = x[tt - (CONV_W - 1):, :]
    return out


def _lru_kernel(y_ref, x_ref, cs_ref, h0_ref, cw_ref, cb_ref, wa_ref, ba_ref, wx_ref, bx_ref, lam_ref,
                o_ref, hn_ref, cn_ref, ext_ref, a_scr, b_scr, h_scr, hc_ref, *, tt, n_t):
    t = pl.program_id(1)

    @pl.when(t == 0)
    def _():
        ext_ref[PAD - (CONV_W - 1):PAD, :] = cs_ref[0]
        hc_ref[...] = h0_ref[0]

    x = x_ref[0]
    xc = _conv_block(ext_ref, x, cw_ref, tt) + cb_ref[...]
    r = jax.nn.sigmoid(_mm(xc, wa_ref[...]) + ba_ref[...])
    i = jax.nn.sigmoid(_mm(xc, wx_ref[...]) + bx_ref[...])
    log_a = -LRU_C * r * jax.nn.softplus(-lam_ref[...])
    a_scr[...] = jnp.exp(log_a)
    th = jnp.tanh(log_a)
    one_minus_a2 = -2.0 * th / (1.0 - th)
    b_scr[...] = jnp.sqrt(one_minus_a2) * (i * xc)

    def step(s, h):
        row = pl.ds(s, 1)
        h = a_scr[row, :] * h + b_scr[row, :]
        h_scr[row, :] = h
        return h

    hc_ref[...] = lax.fori_loop(0, tt, step, hc_ref[...], unroll=8)
    o_ref[0] = (h_scr[...] * _gelu(y_ref[0])).astype(o_ref.dtype)

    @pl.when(t == n_t - 1)
    def _():
        hn_ref[0] = hc_ref[...]
        cn_ref[0] = x[tt - (CONV_W - 1):, :]


def _rglru(proj3, conv_state, h0, cw, cb, wa, ba, wx, bx, lam, l, *, tt):
    bsz, seq, _ = proj3.shape
    n_t = seq // tt
    col = lambda j: pl.BlockSpec((1, tt, D_GROUP), lambda b, t: (b, t, j))
    vec = pl.BlockSpec((None, 1, D_GROUP), lambda b, t: (l, 0, 0))
    sq = pl.BlockSpec((None, D_GROUP, D_GROUP), lambda b, t: (l, 0, 0))
    cs = pl.BlockSpec((1, CONV_W - 1, D_GROUP), lambda b, t: (b, 0, 0))
    hs = pl.BlockSpec((1, 1, D_GROUP), lambda b, t: (b, 0, 0))
    return pl.pallas_call(
        functools.partial(_lru_kernel, tt=tt, n_t=n_t),
        grid=(bsz, n_t),
        in_specs=[col(4), col(5), cs, hs,
                  pl.BlockSpec((None, CONV_W, D_GROUP), lambda b, t: (l, 0, 0)), vec, sq, vec, sq, vec, vec],
        out_specs=[pl.BlockSpec((1, tt, D_GROUP), lambda b, t: (b, t, 0)), hs, cs],
        out_shape=[jax.ShapeDtypeStruct((bsz, seq, D_GROUP), BF16),
                   jax.ShapeDtypeStruct((bsz, 1, D_GROUP), F32),
                   jax.ShapeDtypeStruct((bsz, CONV_W - 1, D_GROUP), F32)],
        scratch_shapes=[pltpu.VMEM((tt + PAD, D_GROUP), F32), pltpu.VMEM((tt, D_GROUP), F32),
                        pltpu.VMEM((tt, D_GROUP), F32), pltpu.VMEM((tt, D_GROUP), F32),
                        pltpu.VMEM((1, D_GROUP), F32)],
        compiler_params=_params("parallel", "arbitrary"),
        name="lru",
    )(proj3, proj3, conv_state, h0, cw, cb, wa, ba, wx, bx, lam)


def _sg_kernel(u_ref, v_ref, lg_ref, lb_ref, ws_ref, bs_ref, o_ref, *vn_ref, c, tt):
    u = _gelu(u_ref[0])
    vn = _layer_norm(_gelu(v_ref[0]), lg_ref[...], lb_ref[...])
    if vn_ref:
        vn_ref[0][0] = vn
    row = lax.broadcasted_iota(jnp.int32, (c, c), 0)
    col = lax.broadcasted_iota(jnp.int32, (c, c), 1)
    mask = (col // CHUNK) <= (row // CHUNK)
    for h in range(N_HEADS):
        lanes = slice(h * D_HEAD, (h + 1) * D_HEAD)
        w = jnp.where(mask, ws_ref[h], 0.0).astype(BF16)
        bias = bs_ref[:, h:h + 1]
        for i in range(tt // c):
            rows = slice(i * c, (i + 1) * c)
            s = _mm(w, vn[rows, lanes]) + bias
            o_ref[0, rows, lanes] = (u[rows, lanes] * s).astype(o_ref.dtype)


def _spatial_gate(proj3, lg, lb, ws, bs_t, l, *, tt, with_vn):
    bsz, seq, _ = proj3.shape
    c = min(SG_CHUNK, seq)
    col = lambda j: pl.BlockSpec((1, tt, D_GROUP), lambda b, t: (b, t, j))
    vec = pl.BlockSpec((None, 1, D_GROUP), lambda b, t: (l, 0, 0))
    out = pl.BlockSpec((1, tt, D_GROUP), lambda b, t: (b, t, 0))
    res = pl.pallas_call(
        functools.partial(_sg_kernel, c=c, tt=tt),
        grid=(bsz, seq // tt),
        in_specs=[col(6), col(7), vec, vec,
                  pl.BlockSpec((None, N_HEADS, c, c), lambda b, t: (l, 0, 0, 0)),
                  pl.BlockSpec((None, c, N_HEADS), lambda b, t: (l, 0, 0))],
        out_specs=[out, out] if with_vn else [out],
        out_shape=[jax.ShapeDtypeStruct((bsz, seq, D_GROUP), BF16)]
                  + ([jax.ShapeDtypeStruct((bsz, seq, D_GROUP), F32)] if with_vn else []),
        compiler_params=_params("parallel", "parallel"),
        name="sg",
    )(proj3, proj3, lg, lb, ws, bs_t)
    return (res[0], res[1]) if with_vn else (res[0], None)


INV_BLOCK = 16


def _unit_lower_inverse(mats, c):
    row = lax.broadcasted_iota(jnp.int32, (c, c), 0)
    col = lax.broadcasted_iota(jnp.int32, (c, c), 1)
    eye = (row == col).astype(F32)
    same = row // INV_BLOCK == col // INV_BLOCK
    ps = [jnp.where(same, a, 0.0) for a in mats]
    xs = [eye - p for p in ps]
    ps = [_bmm(p, p) for p in ps]
    n_sq = int(math.log2(INV_BLOCK)) - 1
    for step in range(n_sq):
        last = step == n_sq - 1
        lhs = xs if last else [jnp.concatenate([x, p], axis=1) for x, p in zip(xs, ps)]
        prod = [_bmm(l, p) for l, p in zip(lhs, ps)]
        xs = [x + r[:, :c, :] for x, r in zip(xs, prod)]
        if not last:
            ps = [r[:, c:, :] for r in prod]
    size = INV_BLOCK
    while size < c:
        big = 2 * size
        sel = (row // big == col // big) & (row // size != col // size)
        offs = [jnp.where(sel, a, 0.0) for a in mats]
        tmp = [_bmm(o, x) for o, x in zip(offs, xs)]
        xs = [x - _bmm(x, t) for x, t in zip(xs, tmp)]
        size = big
    return xs


def _dn_kernel(q_ref, k_ref, v_ref, z_ref, ab_ref, cs_ref, cw_ref, al_ref, dt_ref, gn_ref, s0_ref,
               o_ref, sf_ref, cn_ref, ext_ref, s_scr, u_scr, wq_scr, attn_scr, kdt_scr, egl_scr,
               *, c, tt, n_t):
    t = pl.program_id(1)
    n = tt // c
    heads = range(N_HEADS)

    @pl.when(t == 0)
    def _():
        for j in range(3):
            ext_ref[j, PAD - (CONV_W - 1):PAD, :] = cs_ref[0, :, j * D_GROUP:(j + 1) * D_GROUP]
        s_scr[...] = s0_ref[0]

    raw = (q_ref, k_ref, v_ref)
    qf, kf, vf = [_silu(_conv_block(ext_ref.at[j], raw[j][0], cw_ref.at[:, j * D_GROUP:(j + 1) * D_GROUP], tt))
                  for j in range(3)]

    row = lax.broadcasted_iota(jnp.int32, (c, c), 0)
    col = lax.broadcasted_iota(jnp.int32, (c, c), 1)
    tri = row >= col
    strict = row > col
    tri_f = tri.astype(F32)
    scale = D_HEAD ** -0.5

    ab = ab_ref[0]
    g_all = -jnp.exp(al_ref[...]) * jax.nn.softplus(ab + dt_ref[...])
    beta_all = jax.nn.sigmoid(ab)
    gc_all = jnp.concatenate(
        [jnp.dot(tri_f, g_all[i * c:(i + 1) * c], preferred_element_type=F32, precision=lax.Precision.HIGHEST)
         for i in range(n)], axis=0)
    gc_t = gc_all.T

    def l2n(x):
        return x * lax.rsqrt(jnp.sum(x * x, -1, keepdims=True) + 1e-6)

    def split(x, h):
        return x[:, h * D_HEAD:(h + 1) * D_HEAD].reshape(n, c, D_HEAD)

    q3 = [l2n(split(qf, h)) * scale for h in heads]
    k3 = [l2n(split(kf, h)) for h in heads]
    v3 = [split(vf, h) for h in heads]
    beta = [beta_all[:, N_HEADS + h:N_HEADS + h + 1].reshape(n, c, 1) for h in heads]
    gcol = [gc_all[:, h:h + 1].reshape(n, c, 1) for h in heads]
    grow = [jnp.stack([gc_t[h:h + 1, i * c:(i + 1) * c] for i in range(n)]) for h in heads]
    glast = [g[:, c - 1:c, :] for g in gcol]
    decay = [jnp.where(tri, jnp.exp(gc - gr), 0.0) for gc, gr in zip(gcol, grow)]
    egc = [jnp.exp(g) for g in gcol]
    kb = [k * b for k, b in zip(k3, beta)]
    vb = [v * b for v, b in zip(v3, beta)]
    a_mat = [jnp.where(strict, _bmm_nt(x, k) * d, 0.0) for x, k, d in zip(kb, k3, decay)]
    tinv = _unit_lower_inverse(a_mat, c)
    uw = [_bmm(ti, jnp.concatenate([x, y * e], axis=-1)) for ti, x, y, e in zip(tinv, vb, kb, egc)]
    attn = [_bmm_nt(q, k) * d for q, k, d in zip(q3, k3, decay)]
    for h in heads:
        u_scr[h] = uw[h][:, :, :D_HEAD]
        wq_scr[h] = jnp.concatenate([uw[h][:, :, D_HEAD:], q3[h] * egc[h]], axis=1).astype(BF16)
        attn_scr[h] = attn[h].astype(BF16)
        kdt_scr[h] = jnp.swapaxes(k3[h] * jnp.exp(glast[h] - gcol[h]), 1, 2).astype(BF16)
        egl_scr[h] = jnp.broadcast_to(jnp.exp(glast[h]), (n, 1, D_HEAD))

    def chunk(i, carry):
        rows = pl.ds(pl.multiple_of(i * c, c), c)
        s = [s_scr[h] for h in heads]
        r = [_mm(wq_scr[h, i], s[h]) for h in heads]
        v_new = [u_scr[h, i] - r[h][:c] for h in heads]
        for h in heads:
            s_scr[h] = s[h] * egl_scr[h, i] + _mm(kdt_scr[h, i], v_new[h])
        o = [r[h][c:] + _mm(attn_scr[h, i], v_new[h]) for h in heads]
        for h in heads:
            lanes = slice(h * D_HEAD, (h + 1) * D_HEAD)
            on = o[h] * lax.rsqrt(jnp.mean(o[h] * o[h], -1, keepdims=True) + EPS) * gn_ref[...]
            o_ref[0, rows, lanes] = (on * _silu(z_ref[0, rows, lanes])).astype(o_ref.dtype)
        return carry

    lax.fori_loop(0, n, chunk, 0)

    @pl.when(t == n_t - 1)
    def _():
        sf_ref[0] = s_scr[...]
        for j in range(3):
            cn_ref[0, :, j * D_GROUP:(j + 1) * D_GROUP] = raw[j][0, tt - (CONV_W - 1):, :]


def _deltanet(proj3, ab3, conv_state, state0, cw, a_log, dt_bias, gn, l, *, tt):
    bsz, seq, _ = proj3.shape
    c = min(CHUNK, seq)
    n_t = seq // tt
    n = tt // c
    col = lambda j: pl.BlockSpec((1, tt, D_GROUP), lambda b, t: (b, t, j))
    st = pl.BlockSpec((1, N_HEADS, D_HEAD, D_HEAD), lambda b, t: (b, 0, 0, 0))
    cs = pl.BlockSpec((1, CONV_W - 1, 3 * D_GROUP), lambda b, t: (b, 0, 0))
    lane_row = pl.BlockSpec((None, 1, 128), lambda b, t: (l, 0, 0))
    return pl.pallas_call(
        functools.partial(_dn_kernel, c=c, tt=tt, n_t=n_t),
        grid=(bsz, n_t),
        in_specs=[col(8), col(9), col(10), col(11),
                  pl.BlockSpec((1, tt, 128), lambda b, t: (b, t, 0)),
                  cs,
                  pl.BlockSpec((None, CONV_W, 3 * D_GROUP), lambda b, t: (l, 0, 0)),
                  lane_row, lane_row, lane_row,
                  st],
        out_specs=[pl.BlockSpec((1, tt, D_GROUP), lambda b, t: (b, t, 0)), st, cs],
        out_shape=[jax.ShapeDtypeStruct((bsz, seq, D_GROUP), BF16),
                   jax.ShapeDtypeStruct(state0.shape, F32),
                   jax.ShapeDtypeStruct((bsz, CONV_W - 1, 3 * D_GROUP), F32)],
        scratch_shapes=[pltpu.VMEM((3, tt + PAD, D_GROUP), F32),
                        pltpu.VMEM((N_HEADS, D_HEAD, D_HEAD), F32),
                        pltpu.VMEM((N_HEADS, n, c, D_HEAD), F32),
                        pltpu.VMEM((N_HEADS, n, 2 * c, D_HEAD), BF16),
                        pltpu.VMEM((N_HEADS, n, c, c), BF16),
                        pltpu.VMEM((N_HEADS, n, D_HEAD, c), BF16),
                        pltpu.VMEM((N_HEADS, n, 1, D_HEAD), F32)],
        compiler_params=_params("parallel", "arbitrary"),
        name="dn",
    )(proj3, proj3, proj3, proj3, ab3, conv_state, cw, a_log, dt_bias, gn, state0)


def _mixout_kernel(a_ref, b_ref, c_ref, d_ref, x_ref, w_ref, g_ref, bb_ref, o_ref):
    acc = jnp.dot(a_ref[...], w_ref[0:D_GROUP, :], preferred_element_type=F32)
    acc += jnp.dot(b_ref[...], w_ref[D_GROUP:2 * D_GROUP, :], preferred_element_type=F32)
    acc += jnp.dot(c_ref[...], w_ref[2 * D_GROUP:3 * D_GROUP, :], preferred_element_type=F32)
    acc += jnp.dot(d_ref[...], w_ref[3 * D_GROUP:4 * D_GROUP, :], preferred_element_type=F32)
    o_ref[...] = _layer_norm(ALPHA * x_ref[...] + acc, g_ref[...], bb_ref[...])


def _mixout(a, b, c, d, x, w, g, bb, l, *, tm):
    m, dm = x.shape
    part = pl.BlockSpec((tm, D_GROUP), lambda i: (i, 0))
    vec = pl.BlockSpec((None, 1, dm), lambda i: (l, 0, 0))
    return pl.pallas_call(
        _mixout_kernel,
        grid=(m // tm,),
        in_specs=[part, part, part, part,
                  pl.BlockSpec((tm, dm), lambda i: (i, 0)),
                  pl.BlockSpec((None, 4 * D_GROUP, dm), lambda i: (l, 0, 0), pipeline_mode=pl.Buffered(1)),
                  vec, vec],
        out_specs=pl.BlockSpec((tm, dm), lambda i: (i, 0)),
        out_shape=jax.ShapeDtypeStruct((m, dm), F32),
        compiler_params=_params("parallel"),
        name="mixout",
    )(a, b, c, d, x, w, g, bb)


def _block_diag(w):
    dep, h, n, _ = w.shape
    eye = jnp.eye(h, dtype=w.dtype)
    return (eye[None, :, None, :, None] * w[:, :, :, None, :]).reshape(dep, h * n, h * n)


def _prep(p):
    row = lambda v: v.reshape(DEPTH, 1, -1)
    pad128 = lambda v: jnp.pad(row(v), ((0, 0), (0, 0), (0, 128 - v.shape[-1])))
    q = dict(p)
    q['w_mix_out'] = p['w_mix_out'].astype(BF16)
    q['w_mix_in'] = p['w_mix_in'][:, :, :12 * D_GROUP].astype(BF16)
    q['w_mix_ab'] = jnp.pad(p['w_mix_in'][:, :, 12 * D_GROUP:], ((0, 0), (0, 0), (0, 128 - 2 * N_HEADS))).astype(BF16)
    for name in ('ln1_g', 'ln1_b', 'ln2_g', 'ln2_b', 'ln3_g', 'ln3_b', 'ret_norm_g', 'lru_conv_b',
                 'lru_b_a', 'lru_b_x', 'lru_lam', 'sg_ln_g', 'sg_ln_b', 'dn_norm_g'):
        q[name] = row(p[name])
    q['lru_w_a'] = _block_diag(p['lru_w_a']).astype(BF16)
    q['lru_w_x'] = _block_diag(p['lru_w_x']).astype(BF16)
    q['dn_a_log'] = pad128(p['dn_a_log'])
    q['dn_dt_bias'] = pad128(p['dn_dt_bias'])
    return q


def _token_mixers(x, st, p, l, s):
    bsz, seq, tm, tt = s['bsz'], s['seq'], s['tm'], s['tt']
    s_ret, s_lru, s_lru_conv, s_dn, s_dn_conv = st
    c_sg = min(SG_CHUNK, seq)
    sg_w = p['sg_w'][:, :, :c_sg, :c_sg]
    sg_b_t = jnp.swapaxes(p['sg_b'][:, :, :c_sg], 1, 2)
    proj, ab = _proj(x, p['w_mix_in'], p['w_mix_ab'], l, tm=tm)
    proj3 = proj.reshape(bsz, seq, -1)
    ab3 = ab.reshape(bsz, seq, 128)
    out_a, ret_new = _retention(proj3, s_ret, p['ret_norm_g'], s['pos'], l, tt=tt)
    out_b, lru_h_new, lru_conv_new = _rglru(
        proj3, s_lru_conv, s_lru.reshape(bsz, 1, D_GROUP), p['lru_conv_w'], p['lru_conv_b'],
        p['lru_w_a'], p['lru_b_a'], p['lru_w_x'], p['lru_b_x'], p['lru_lam'], l, tt=tt)
    out_c, sg_v = _spatial_gate(proj3, p['sg_ln_g'], p['sg_ln_b'], sg_w, sg_b_t, l,
                                tt=min(seq, 4 * SG_CHUNK), with_vn=s['keep_sg_v'])
    out_d, dn_new, dn_conv_new = _deltanet(proj3, ab3, s_dn_conv, s_dn, p['dn_conv_w'],
                                           p['dn_a_log'], p['dn_dt_bias'], p['dn_norm_g'], l, tt=tt)
    flat = lambda o: o.reshape(bsz * seq, D_GROUP)
    x = _mixout(flat(out_a), flat(out_b), flat(out_c), flat(out_d), x, p['w_mix_out'],
                p['ln2_g'], p['ln2_b'], l, tm=tm)
    return x, (ret_new, lru_h_new.reshape(bsz, D_GROUP), lru_conv_new, dn_new, dn_conv_new, sg_v)


def kernel(x_prompt, x_sample, state_ret, state_lru_h, state_lru_conv, state_dn, state_dn_conv, ffn1_w_in, ffn1_w_out, ln1_g, ln1_b, w_mix_in, ret_norm_g, lru_conv_w, lru_conv_b, lru_w_a, lru_b_a, lru_w_x, lru_b_x, lru_lam, sg_ln_g, sg_ln_b, sg_w, sg_b, dn_conv_w, dn_a_log, dn_dt_bias, dn_norm_g, w_mix_out, ln2_g, ln2_b, ffn2_w_in, ffn2_w_out, ln3_g, ln3_b):
    weights = dict(ffn1_w_in=ffn1_w_in, ffn1_w_out=ffn1_w_out, ln1_g=ln1_g, ln1_b=ln1_b, w_mix_in=w_mix_in,
                   ret_norm_g=ret_norm_g, lru_conv_w=lru_conv_w, lru_conv_b=lru_conv_b, lru_w_a=lru_w_a,
                   lru_b_a=lru_b_a, lru_w_x=lru_w_x, lru_b_x=lru_b_x, lru_lam=lru_lam, sg_ln_g=sg_ln_g,
                   sg_ln_b=sg_ln_b, sg_w=sg_w, sg_b=sg_b, dn_conv_w=dn_conv_w, dn_a_log=dn_a_log,
                   dn_dt_bias=dn_dt_bias, dn_norm_g=dn_norm_g, w_mix_out=w_mix_out, ln2_g=ln2_g, ln2_b=ln2_b,
                   ffn2_w_in=ffn2_w_in, ffn2_w_out=ffn2_w_out, ln3_g=ln3_g, ln3_b=ln3_b)
    p = _prep(weights)

    bp, lp, dm = x_prompt.shape
    bs, ls, _ = x_sample.shape
    zeros = lambda *shape: jnp.zeros(shape, F32)
    prompt = dict(bsz=bp, seq=lp, tm=1024, tt=min(lp, 256), pos=np.arange(lp), keep_sg_v=False)
    sample = dict(bsz=bs, seq=ls, tm=bs * ls, tt=ls, pos=PAST_LEN + np.arange(ls), keep_sg_v=True)
    prompt_init = (zeros(bp, N_HEADS, D_HEAD, D_HEAD), zeros(bp, D_GROUP), zeros(bp, CONV_W - 1, D_GROUP),
                   zeros(bp, N_HEADS, D_HEAD, D_HEAD), zeros(bp, CONV_W - 1, 3 * D_GROUP))
    xp = x_prompt.reshape(bp * lp, dm)
    xs = x_sample.reshape(bs * ls, dm)

    ffns = [(l, name) for l in range(DEPTH) for name in ('ffn1', 'ffn2')]
    ln_of = {'ffn1': ('ln1_g', 'ln1_b'), 'ffn2': ('ln3_g', 'ln3_b')}
    wb = (weights['ffn1_w_in'][0].astype(BF16), weights['ffn1_w_out'][0].astype(BF16))

    def ffn_pair(k, xp, xs, wb):
        l, name = ffns[k]
        g, b = p[ln_of[name][0]], p[ln_of[name][1]]
        nxt = None
        if k + 1 < len(ffns):
            ln, nname = ffns[k + 1]
            nxt = (weights[nname + '_w_in'], weights[nname + '_w_out'], ln)
        xp, wb_next = _ffn(xp, wb[0], wb[1], g, b, l, tm=prompt['tm'], cast_next=nxt)
        xs, _ = _ffn(xs, wb[0], wb[1], g, b, l, tm=sample['tm'])
        return xp, xs, wb_next

    st_p, st_s = [], []
    for l in range(DEPTH):
        xp, xs, wb = ffn_pair(2 * l, xp, xs, wb)
        xp, new_p = _token_mixers(xp, prompt_init, p, l, prompt)
        xs, new_s = _token_mixers(xs, (state_ret[l], state_lru_h[l], state_lru_conv[l], state_dn[l],
                                       state_dn_conv[l]), p, l, sample)
        xp, xs, wb = ffn_pair(2 * l + 1, xp, xs, wb)
        st_p.append(new_p)
        st_s.append(new_s)

    stack = lambda sts, i: jnp.stack([s[i] for s in sts])
    return (xp.reshape(bp, lp, dm), xs.reshape(bs, ls, dm),
            stack(st_p, 0), stack(st_p, 1), stack(st_p, 2), stack(st_p, 3), stack(st_p, 4),
            stack(st_s, 0), stack(st_s, 1), stack(st_s, 2), stack(st_s, 3), stack(st_s, 4), stack(st_s, 5))
```

```python
import functools
import math

import numpy as np
import jax
import jax.numpy as jnp
from jax import lax
from jax.experimental import pallas as pl
from jax.experimental.pallas import tpu as pltpu

F32 = jnp.float32
BF16 = jnp.bfloat16

DEPTH = 2
PAST_LEN = 4096
CHUNK = 64
D_GROUP = 512
N_HEADS = 4
D_HEAD = D_GROUP // N_HEADS
H_LRU = 8
LRU_C = 8.0
CONV_W = 4
SG_CHUNK = 128
ROPE_BASE = 10000.0
ALPHA = (2.0 * DEPTH) ** 0.25
EPS = 1e-5

FFN_TF = 512
PROJ_TN = 1024
VMEM_LIMIT = 58 * 1024 * 1024


def _params(*sem):
    return pltpu.CompilerParams(dimension_semantics=sem, vmem_limit_bytes=VMEM_LIMIT)


def _mm(a, b):
    return jnp.dot(a.astype(BF16), b.astype(BF16), preferred_element_type=F32)


def _bmm(a, b):
    return jnp.einsum('nij,njk->nik', a.astype(BF16), b.astype(BF16), preferred_element_type=F32)


def _bmm_nt(a, b):
    return jnp.einsum('nid,njd->nij', a.astype(BF16), b.astype(BF16), preferred_element_type=F32)


def _layer_norm(y, g, b):
    mu = jnp.mean(y, -1, keepdims=True)
    yc = y - mu
    var = jnp.mean(yc * yc, -1, keepdims=True)
    return yc * lax.rsqrt(var + EPS) * g + b


def _silu(x):
    return x * jax.nn.sigmoid(x)


def _gelu(x):
    return jax.nn.gelu(x, approximate=True)


def _ffn_kernel(x_ref, wg_ref, wu_ref, wo_ref, g_ref, b_ref, *rest, nf, n_row, cast_next):
    if cast_next:
        ci_ref, co_ref, o_ref, cio_ref, coo_ref, xb_ref = rest
        cio_ref[...] = ci_ref[...].astype(BF16)
        coo_ref[...] = co_ref[...].astype(BF16)
    else:
        o_ref, xb_ref = rest
    f = pl.program_id(1)

    @pl.when(f == 0)
    def _():
        xb_ref[...] = x_ref[...].astype(BF16)
        o_ref[...] = jnp.zeros_like(o_ref)

    tr = o_ref.shape[0] // n_row
    for r in range(n_row):
        rows = slice(r * tr, (r + 1) * tr)
        xb = xb_ref[rows, :]
        gate = jnp.dot(xb, wg_ref[...], preferred_element_type=F32)
        up = jnp.dot(xb, wu_ref[...], preferred_element_type=F32)
        h = (_silu(gate) * up).astype(BF16)
        o_ref[rows, :] += jnp.dot(h, wo_ref[...], preferred_element_type=F32)

    @pl.when(f == nf - 1)
    def _():
        y = ALPHA * x_ref[...] + 0.5 * o_ref[...]
        o_ref[...] = _layer_norm(y, g_ref[...], b_ref[...])


def _ffn(x, w_in, w_out, g, b, l, *, tm, cast_next=None):
    m, d = x.shape
    d_ff = w_out.shape[0]
    tf = FFN_TF
    nf = d_ff // tf
    ni = m // tm
    vec = pl.BlockSpec((None, 1, d), lambda i, f: (l, 0, 0))
    in_specs = [
        pl.BlockSpec((tm, d), lambda i, f: (i, 0), pipeline_mode=pl.Buffered(1)),
        pl.BlockSpec((d, tf), lambda i, f: (0, f)),
        pl.BlockSpec((d, tf), lambda i, f: (0, f + nf)),
        pl.BlockSpec((tf, d), lambda i, f: (f, 0)),
        vec, vec,
    ]
    out_specs = [pl.BlockSpec((tm, d), lambda i, f: (i, 0))]
    out_shape = [jax.ShapeDtypeStruct((m, d), F32)]
    args = [x, w_in, w_in, w_out, g, b]
    if cast_next is not None:
        nw_in, nw_out, ln = cast_next
        ri, ci, ro = d // ni, 2 * d_ff // nf, d_ff // (ni * nf)
        assert ri * ni == d and ci * nf == 2 * d_ff and ro * ni * nf == d_ff
        in_specs += [pl.BlockSpec((None, ri, ci), lambda i, f: (ln, i, f)),
                     pl.BlockSpec((None, ro, d), lambda i, f: (ln, i * nf + f, 0))]
        out_specs += [pl.BlockSpec((ri, ci), lambda i, f: (i, f)),
                      pl.BlockSpec((ro, d), lambda i, f: (i * nf + f, 0))]
        out_shape += [jax.ShapeDtypeStruct((d, 2 * d_ff), BF16), jax.ShapeDtypeStruct((d_ff, d), BF16)]
        args += [nw_in, nw_out]
    res = pl.pallas_call(
        functools.partial(_ffn_kernel, nf=nf, n_row=2 if tm >= 1024 else 1, cast_next=cast_next is not None),
        grid=(ni, nf),
        in_specs=in_specs,
        out_specs=out_specs,
        out_shape=out_shape,
        scratch_shapes=[pltpu.VMEM((tm, d), BF16)],
        compiler_params=_params("parallel", "arbitrary"),
        name="ffn",
    )(*args)
    return res[0], (tuple(res[1:]) if cast_next is not None else None)


def _proj_kernel(x_ref, w_ref, wab_ref, o_ref, ab_ref, xb_ref):
    j = pl.program_id(1)

    @pl.when(j == 0)
    def _():
        xb = x_ref[...].astype(BF16)
        xb_ref[...] = xb
        ab_ref[...] = jnp.dot(xb, wab_ref[...], preferred_element_type=F32)

    o_ref[...] = jnp.dot(xb_ref[...], w_ref[...], preferred_element_type=F32)


def _proj(x, w, wab, l, *, tm):
    m, d = x.shape
    tn = PROJ_TN
    n = w.shape[-1]
    return pl.pallas_call(
        _proj_kernel,
        grid=(m // tm, n // tn),
        in_specs=[
            pl.BlockSpec((tm, d), lambda i, j: (i, 0)),
            pl.BlockSpec((d, tn), lambda i, j: (0, j)),
            pl.BlockSpec((None, d, 128), lambda i, j: (l, 0, 0)),
        ],
        out_specs=[
            pl.BlockSpec((tm, tn), lambda i, j: (i, j)),
            pl.BlockSpec((tm, 128), lambda i, j: (i, 0)),
        ],
        out_shape=[jax.ShapeDtypeStruct((m, n), F32), jax.ShapeDtypeStruct((m, 128), F32)],
        scratch_shapes=[pltpu.VMEM((tm, d), BF16)],
        compiler_params=_params("parallel", "arbitrary"),
        name="proj",
    )(x, w, wab)


def _ret_tables(pos, c):
    half = D_HEAD // 2
    inv = ROPE_BASE ** (-np.arange(half, dtype=np.float64) / half)
    ang = pos.astype(np.float64)[:, None] * inv[None, :]
    cos = np.concatenate([np.cos(ang), np.cos(ang)], -1)
    sin = np.concatenate([-np.sin(ang), np.sin(ang)], -1)
    log_g = np.log1p(-np.exp2(-5.0 - np.arange(N_HEADS, dtype=np.float64)))
    idx = np.arange(c, dtype=np.float64)
    diff = idx[:, None] - idx[None, :]
    scale = D_HEAD ** -0.5
    dmask = np.where(diff >= 0, np.exp(log_g[:, None, None] * np.maximum(diff, 0.0)), 0.0) * scale
    kdec = np.exp(log_g[:, None] * (c - 1.0 - idx)[None, :]) * scale
    qdec = np.exp(log_g[:, None] * (idx + 1.0)[None, :])
    kdec = np.broadcast_to(kdec[:, :, None], (N_HEADS, c, D_HEAD))
    qdec = np.broadcast_to(qdec[:, :, None], (N_HEADS, c, D_HEAD))
    g_c = np.exp(log_g * c)
    as32 = lambda a: jnp.asarray(np.ascontiguousarray(a), F32)
    return as32(cos), as32(sin), as32(dmask), as32(qdec), as32(kdec), [float(v) for v in g_c]


def _ret_kernel(q_ref, k_ref, v_ref, g_ref, cos_ref, sin_ref, dm_ref, qd_ref, kd_ref, gn_ref, *rest,
                c, n, g_c, n_t, bb, has_state):
    if has_state:
        s0_ref, o_ref, sf_ref, s_scr, sst_scr = rest
    else:
        o_ref, sf_ref, s_scr, sst_scr = rest
    t = pl.program_id(1)
    heads = range(N_HEADS)
    lanes = [slice(h * D_HEAD, (h + 1) * D_HEAD) for h in heads]
    tt = n * c

    @pl.when(t == 0)
    def _():
        s_scr[...] = s0_ref[...] if has_state else jnp.zeros_like(s_scr)

    cos = jnp.concatenate([cos_ref[...]] * bb, axis=0)
    sin = jnp.concatenate([sin_ref[...]] * bb, axis=0)

    def rows(ref, h):
        return ref[:, :, lanes[h]].reshape(bb * tt, D_HEAD)

    def rot(x):
        return (x * cos + pltpu.roll(x, D_HEAD // 2, 1) * sin).reshape(bb * n, c, D_HEAD)

    q3 = [rot(rows(q_ref, h)) for h in heads]
    k3 = [rot(rows(k_ref, h)) for h in heads]
    v3 = [rows(v_ref, h).reshape(bb * n, c, D_HEAD) for h in heads]
    scores = [_bmm_nt(q3[h], k3[h]) * dm_ref[h] for h in heads]
    o_intra = [_bmm(scores[h], v3[h]) for h in heads]
    kdt = [jnp.swapaxes(k3[h] * kd_ref[h], 1, 2) for h in heads]
    upd = [_bmm(kdt[h], v3[h]) for h in heads]
    for b in range(bb):
        for h in heads:
            s = s_scr[b, h]
            for i in range(n):
                sst_scr[h, b * n + i] = s.astype(BF16)
                s = g_c[h] * s + upd[h][b * n + i]
            s_scr[b, h] = s
    for h in heads:
        o = o_intra[h] + _bmm(q3[h] * qd_ref[h], sst_scr[h])
        mu = jnp.mean(o, -1, keepdims=True)
        oc = o - mu
        var = jnp.mean(oc * oc, -1, keepdims=True)
        on = (oc * lax.rsqrt(var + EPS)).reshape(bb * tt, D_HEAD) * gn_ref[:, lanes[h]]
        o_ref[:, :, lanes[h]] = (on * _silu(rows(g_ref, h))).reshape(bb, tt, D_HEAD).astype(o_ref.dtype)

    @pl.when(t == n_t - 1)
    def _():
        sf_ref[...] = s_scr[...]


def _retention(proj3, state0, gn, pos, l, *, tt, bb):
    bsz, seq, _ = proj3.shape
    c = min(CHUNK, seq)
    cos, sin, dmask, qdec, kdec, g_c = _ret_tables(pos, c)
    n_t = seq // tt
    n = tt // c
    col = lambda j: pl.BlockSpec((bb, tt, D_GROUP), lambda b, t: (b, t, j))
    full3 = pl.BlockSpec((N_HEADS, c, c), lambda b, t: (0, 0, 0))
    full3d = pl.BlockSpec((N_HEADS, c, D_HEAD), lambda b, t: (0, 0, 0))
    st = pl.BlockSpec((bb, N_HEADS, D_HEAD, D_HEAD), lambda b, t: (b, 0, 0, 0))
    has_state = state0 is not None
    state_in = [state0] if has_state else []
    state_spec = ([pl.BlockSpec((None, bb, N_HEADS, D_HEAD, D_HEAD), lambda b, t: (l, b, 0, 0, 0))]
                  if has_state else [])
    return pl.pallas_call(
        functools.partial(_ret_kernel, c=c, n=n, g_c=g_c, n_t=n_t, bb=bb, has_state=has_state),
        grid=(bsz // bb, n_t),
        in_specs=[col(0), col(1), col(2), col(3),
                  pl.BlockSpec((tt, D_HEAD), lambda b, t: (t, 0)),
                  pl.BlockSpec((tt, D_HEAD), lambda b, t: (t, 0)),
                  full3, full3d, full3d,
                  pl.BlockSpec((None, 1, D_GROUP), lambda b, t: (l, 0, 0))] + state_spec,
        out_specs=[pl.BlockSpec((bb, tt, D_GROUP), lambda b, t: (b, t, 0)), st],
        out_shape=[jax.ShapeDtypeStruct((bsz, seq, D_GROUP), BF16),
                   jax.ShapeDtypeStruct((bsz, N_HEADS, D_HEAD, D_HEAD), F32)],
        scratch_shapes=[pltpu.VMEM((bb, N_HEADS, D_HEAD, D_HEAD), F32),
                        pltpu.VMEM((N_HEADS, bb * n, D_HEAD, D_HEAD), BF16)],
        compiler_params=_params("parallel", "arbitrary"),
        name="ret",
    )(proj3, proj3, proj3, proj3, cos, sin, dmask, qdec, kdec, gn, *state_in)


PAD = 8


def _conv_block(ext_ref, x, w_ref, tt):
    ext_ref[PAD:PAD + tt, :] = x
    out = x * w_ref[CONV_W - 1:CONV_W, :]
    for j in range(1, CONV_W):
        out = out + ext_ref[PAD - j:PAD - j + tt, :] * w_ref[CONV_W - 1 - j:CONV_W - j, :]
    ext_ref[PAD - (CONV_W - 1):PAD, :] = x[tt - (CONV_W - 1):, :]
    return out


def _lru_kernel(y_ref, x_ref, cs_ref, h0_ref, cw_ref, cb_ref, wa_ref, ba_ref, wx_ref, bx_ref, lam_ref,
                o_ref, hn_ref, cn_ref, ext_ref, a_scr, b_scr, h_scr, hc_ref, *, tt, n_t):
    t = pl.program_id(1)

    @pl.when(t == 0)
    def _():
        ext_ref[PAD - (CONV_W - 1):PAD, :] = cs_ref[0]
        hc_ref[...] = h0_ref[0]

    x = x_ref[0]
    xc = _conv_block(ext_ref, x, cw_ref, tt) + cb_ref[...]
    r = jax.nn.sigmoid(_mm(xc, wa_ref[...]) + ba_ref[...])
    i = jax.nn.sigmoid(_mm(xc, wx_ref[...]) + bx_ref[...])
    log_a = -LRU_C * r * jax.nn.softplus(-lam_ref[...])
    a_scr[...] = jnp.exp(log_a)
    th = jnp.tanh(log_a)
    one_minus_a2 = -2.0 * th / (1.0 - th)
    b_scr[...] = jnp.sqrt(one_minus_a2) * (i * xc)

    def step(s, h):
        row = pl.ds(s, 1)
        h = a_scr[row, :] * h + b_scr[row, :]
        h_scr[row, :] = h
        return h

    hc_ref[...] = lax.fori_loop(0, tt, step, hc_ref[...], unroll=8)
    o_ref[0] = (h_scr[...] * _gelu(y_ref[0])).astype(o_ref.dtype)

    @pl.when(t == n_t - 1)
    def _():
        hn_ref[0] = hc_ref[...]
        cn_ref[0] = x[tt - (CONV_W - 1):, :]


def _rglru(proj3, conv_state, h0, cw, cb, wa, ba, wx, bx, lam, l, *, tt):
    bsz, seq, _ = proj3.shape
    n_t = seq // tt
    col = lambda j: pl.BlockSpec((1, tt, D_GROUP), lambda b, t: (b, t, j))
    vec = pl.BlockSpec((None, 1, D_GROUP), lambda b, t: (l, 0, 0))
    sq = pl.BlockSpec((None, D_GROUP, D_GROUP), lambda b, t: (l, 0, 0))
    cs = pl.BlockSpec((1, CONV_W - 1, D_GROUP), lambda b, t: (b, 0, 0))
    hs = pl.BlockSpec((1, 1, D_GROUP), lambda b, t: (b, 0, 0))
    return pl.pallas_call(
        functools.partial(_lru_kernel, tt=tt, n_t=n_t),
        grid=(bsz, n_t),
        in_specs=[col(4), col(5), cs, hs,
                  pl.BlockSpec((None, CONV_W, D_GROUP), lambda b, t: (l, 0, 0)), vec, sq, vec, sq, vec, vec],
        out_specs=[pl.BlockSpec((1, tt, D_GROUP), lambda b, t: (b, t, 0)), hs, cs],
        out_shape=[jax.ShapeDtypeStruct((bsz, seq, D_GROUP), BF16),
                   jax.ShapeDtypeStruct((bsz, 1, D_GROUP), F32),
                   jax.ShapeDtypeStruct((bsz, CONV_W - 1, D_GROUP), F32)],
        scratch_shapes=[pltpu.VMEM((tt + PAD, D_GROUP), F32), pltpu.VMEM((tt, D_GROUP), F32),
                        pltpu.VMEM((tt, D_GROUP), F32), pltpu.VMEM((tt, D_GROUP), F32),
                        pltpu.VMEM((1, D_GROUP), F32)],
        compiler_params=_params("parallel", "arbitrary"),
        name="lru",
    )(proj3, proj3, conv_state, h0, cw, cb, wa, ba, wx, bx, lam)


def _sg_kernel(u_ref, v_ref, lg_ref, lb_ref, ws_ref, bs_ref, o_ref, *vn_ref, c, tt, bb):
    u = _gelu(u_ref[...])
    vn = _layer_norm(_gelu(v_ref[...]), lg_ref[...], lb_ref[...])
    if vn_ref:
        vn_ref[0][...] = vn
    row = lax.broadcasted_iota(jnp.int32, (c, c), 0)
    col = lax.broadcasted_iota(jnp.int32, (c, c), 1)
    mask = (col // CHUNK) <= (row // CHUNK)
    for h in range(N_HEADS):
        lanes = slice(h * D_HEAD, (h + 1) * D_HEAD)
        w = jnp.where(mask, ws_ref[h], 0.0).astype(BF16)
        bias = bs_ref[:, h:h + 1]
        for b in range(bb):
            for i in range(tt // c):
                rows = slice(i * c, (i + 1) * c)
                s = _mm(w, vn[b, rows, lanes]) + bias
                o_ref[b, rows, lanes] = (u[b, rows, lanes] * s).astype(o_ref.dtype)


def _spatial_gate(proj3, lg, lb, ws, bs_t, l, *, tt, bb, with_vn):
    bsz, seq, _ = proj3.shape
    c = min(SG_CHUNK, seq)
    col = lambda j: pl.BlockSpec((bb, tt, D_GROUP), lambda b, t: (b, t, j))
    vec = pl.BlockSpec((None, 1, D_GROUP), lambda b, t: (l, 0, 0))
    out = pl.BlockSpec((bb, tt, D_GROUP), lambda b, t: (b, t, 0))
    res = pl.pallas_call(
        functools.partial(_sg_kernel, c=c, tt=tt, bb=bb),
        grid=(bsz // bb, seq // tt),
        in_specs=[col(6), col(7), vec, vec,
                  pl.BlockSpec((None, N_HEADS, c, c), lambda b, t: (l, 0, 0, 0)),
                  pl.BlockSpec((None, c, N_HEADS), lambda b, t: (l, 0, 0))],
        out_specs=[out, out] if with_vn else [out],
        out_shape=[jax.ShapeDtypeStruct((bsz, seq, D_GROUP), BF16)]
                  + ([jax.ShapeDtypeStruct((bsz, seq, D_GROUP), F32)] if with_vn else []),
        compiler_params=_params("parallel", "parallel"),
        name="sg",
    )(proj3, proj3, lg, lb, ws, bs_t)
    return (res[0], res[1]) if with_vn else (res[0], None)


INV_BLOCK = 16


def _unit_lower_inverse(mats, c):
    row = lax.broadcasted_iota(jnp.int32, (c, c), 0)
    col = lax.broadcasted_iota(jnp.int32, (c, c), 1)
    eye = (row == col).astype(F32)
    same = row // INV_BLOCK == col // INV_BLOCK
    ps = [jnp.where(same, a, 0.0) for a in mats]
    xs = [eye - p for p in ps]
    ps = [_bmm(p, p) for p in ps]
    n_sq = int(math.log2(INV_BLOCK)) - 1
    for step in range(n_sq):
        last = step == n_sq - 1
        lhs = xs if last else [jnp.concatenate([x, p], axis=1) for x, p in zip(xs, ps)]
        prod = [_bmm(l, p) for l, p in zip(lhs, ps)]
        xs = [x + r[:, :c, :] for x, r in zip(xs, prod)]
        if not last:
            ps = [r[:, c:, :] for r in prod]
    size = INV_BLOCK
    while size < c:
        big = 2 * size
        sel = (row // big == col // big) & (row // size != col // size)
        offs = [jnp.where(sel, a, 0.0) for a in mats]
        tmp = [_bmm(o, x) for o, x in zip(offs, xs)]
        xs = [x - _bmm(x, t) for x, t in zip(xs, tmp)]
        size = big
    return xs


def _dn_kernel(q_ref, k_ref, v_ref, z_ref, ab_ref, cw_ref, al_ref, dt_ref, gn_ref, *rest,
               c, n, n_t, bb, has_state):
    if has_state:
        cs_ref, s0_ref = rest[:2]
        rest = rest[2:]
    o_ref, sf_ref, cn_ref, ext_ref, s_scr, u_scr, wq_scr, attn_scr, kdt_scr, egl_scr = rest
    t = pl.program_id(1)
    tt = n * c
    nb = bb * n
    heads = range(N_HEADS)

    @pl.when(t == 0)
    def _():
        for j in range(3):
            hist = (cs_ref[:, :, j * D_GROUP:(j + 1) * D_GROUP] if has_state
                    else jnp.zeros((bb, CONV_W - 1, D_GROUP), F32))
            ext_ref[j, :, PAD - (CONV_W - 1):PAD, :] = hist
        s_scr[...] = s0_ref[...] if has_state else jnp.zeros_like(s_scr)

    raw = (q_ref, k_ref, v_ref)
    qf, kf, vf = [
        jnp.concatenate([_silu(_conv_block(ext_ref.at[j, b], raw[j][b],
                                           cw_ref.at[:, j * D_GROUP:(j + 1) * D_GROUP], tt))
                         for b in range(bb)], axis=0)
        for j in range(3)]

    row = lax.broadcasted_iota(jnp.int32, (c, c), 0)
    col = lax.broadcasted_iota(jnp.int32, (c, c), 1)
    tri = row >= col
    strict = row > col
    tri_f = tri.astype(F32)
    scale = D_HEAD ** -0.5

    ab = ab_ref[...].reshape(bb * tt, 128)
    g_all = -jnp.exp(al_ref[...]) * jax.nn.softplus(ab + dt_ref[...])
    beta_all = jax.nn.sigmoid(ab)
    gc_all = jnp.concatenate(
        [jnp.dot(tri_f, g_all[i * c:(i + 1) * c], preferred_element_type=F32, precision=lax.Precision.HIGHEST)
         for i in range(nb)], axis=0)
    gc_t = gc_all.T

    def l2n(x):
        return x * lax.rsqrt(jnp.sum(x * x, -1, keepdims=True) + 1e-6)

    def split(x, h):
        return x[:, h * D_HEAD:(h + 1) * D_HEAD].reshape(nb, c, D_HEAD)

    q3 = [l2n(split(qf, h)) * scale for h in heads]
    k3 = [l2n(split(kf, h)) for h in heads]
    v3 = [split(vf, h) for h in heads]
    beta = [beta_all[:, N_HEADS + h:N_HEADS + h + 1].reshape(nb, c, 1) for h in heads]
    gcol = [gc_all[:, h:h + 1].reshape(nb, c, 1) for h in heads]
    grow = [jnp.stack([gc_t[h:h + 1, i * c:(i + 1) * c] for i in range(nb)]) for h in heads]
    glast = [g[:, c - 1:c, :] for g in gcol]
    decay = [jnp.where(tri, jnp.exp(gc - gr), 0.0) for gc, gr in zip(gcol, grow)]
    egc = [jnp.exp(g) for g in gcol]
    kb = [k * b for k, b in zip(k3, beta)]
    vb = [v * b for v, b in zip(v3, beta)]
    a_mat = [jnp.where(strict, _bmm_nt(x, k) * d, 0.0) for x, k, d in zip(kb, k3, decay)]
    tinv = _unit_lower_inverse(a_mat, c)
    uw = [_bmm(ti, jnp.concatenate([x, y * e], axis=-1)) for ti, x, y, e in zip(tinv, vb, kb, egc)]
    attn = [_bmm_nt(q, k) * d for q, k, d in zip(q3, k3, decay)]
    for h in heads:
        u_scr[h] = uw[h][:, :, :D_HEAD]
        wq_scr[h] = jnp.concatenate([uw[h][:, :, D_HEAD:], q3[h] * egc[h]], axis=1).astype(BF16)
        attn_scr[h] = attn[h].astype(BF16)
        kdt_scr[h] = jnp.swapaxes(k3[h] * jnp.exp(glast[h] - gcol[h]), 1, 2).astype(BF16)
        egl_scr[h] = jnp.broadcast_to(jnp.exp(glast[h]), (nb, 1, D_HEAD))

    pairs = [(b, h) for b in range(bb) for h in heads]

    def chunk(i, carry):
        rows = pl.ds(pl.multiple_of(i * c, c), c)
        s = [s_scr[b, h] for b, h in pairs]
        r = [_mm(wq_scr[h, b * n + i], s[p]) for p, (b, h) in enumerate(pairs)]
        v_new = [u_scr[h, b * n + i] - r[p][:c] for p, (b, h) in enumerate(pairs)]
        for p, (b, h) in enumerate(pairs):
            s_scr[b, h] = s[p] * egl_scr[h, b * n + i] + _mm(kdt_scr[h, b * n + i], v_new[p])
        o = [r[p][c:] + _mm(attn_scr[h, b * n + i], v_new[p]) for p, (b, h) in enumerate(pairs)]
        for p, (b, h) in enumerate(pairs):
            lanes = slice(h * D_HEAD, (h + 1) * D_HEAD)
            on = o[p] * lax.rsqrt(jnp.mean(o[p] * o[p], -1, keepdims=True) + EPS) * gn_ref[...]
            o_ref[b, rows, lanes] = (on * _silu(z_ref[b, rows, lanes])).astype(o_ref.dtype)
        return carry

    lax.fori_loop(0, n, chunk, 0)

    @pl.when(t == n_t - 1)
    def _():
        sf_ref[...] = s_scr[...]
        for j in range(3):
            cn_ref[:, :, j * D_GROUP:(j + 1) * D_GROUP] = raw[j][:, tt - (CONV_W - 1):, :]


def _deltanet(proj3, ab3, states, cw, a_log, dt_bias, gn, l, *, tt, bb):
    bsz, seq, _ = proj3.shape
    c = min(CHUNK, seq)
    n_t = seq // tt
    n = tt // c
    nb = bb * n
    col = lambda j: pl.BlockSpec((bb, tt, D_GROUP), lambda b, t: (b, t, j))
    st = pl.BlockSpec((bb, N_HEADS, D_HEAD, D_HEAD), lambda b, t: (b, 0, 0, 0))
    cs = pl.BlockSpec((bb, CONV_W - 1, 3 * D_GROUP), lambda b, t: (b, 0, 0))
    lane_row = pl.BlockSpec((None, 1, 128), lambda b, t: (l, 0, 0))
    has_state = states is not None
    state_in = list(states) if has_state else []
    state_spec = ([pl.BlockSpec((None, bb, CONV_W - 1, 3 * D_GROUP), lambda b, t: (l, b, 0, 0)),
                   pl.BlockSpec((None, bb, N_HEADS, D_HEAD, D_HEAD), lambda b, t: (l, b, 0, 0, 0))]
                  if has_state else [])
    return pl.pallas_call(
        functools.partial(_dn_kernel, c=c, n=n, n_t=n_t, bb=bb, has_state=has_state),
        grid=(bsz // bb, n_t),
        in_specs=[col(8), col(9), col(10), col(11),
                  pl.BlockSpec((bb, tt, 128), lambda b, t: (b, t, 0)),
                  pl.BlockSpec((None, CONV_W, 3 * D_GROUP), lambda b, t: (l, 0, 0)),
                  lane_row, lane_row, lane_row] + state_spec,
        out_specs=[pl.BlockSpec((bb, tt, D_GROUP), lambda b, t: (b, t, 0)), st, cs],
        out_shape=[jax.ShapeDtypeStruct((bsz, seq, D_GROUP), BF16),
                   jax.ShapeDtypeStruct((bsz, N_HEADS, D_HEAD, D_HEAD), F32),
                   jax.ShapeDtypeStruct((bsz, CONV_W - 1, 3 * D_GROUP), F32)],
        scratch_shapes=[pltpu.VMEM((3, bb, tt + PAD, D_GROUP), F32),
                        pltpu.VMEM((bb, N_HEADS, D_HEAD, D_HEAD), F32),
                        pltpu.VMEM((N_HEADS, nb, c, D_HEAD), F32),
                        pltpu.VMEM((N_HEADS, nb, 2 * c, D_HEAD), BF16),
                        pltpu.VMEM((N_HEADS, nb, c, c), BF16),
                        pltpu.VMEM((N_HEADS, nb, D_HEAD, c), BF16),
                        pltpu.VMEM((N_HEADS, nb, 1, D_HEAD), F32)],
        compiler_params=_params("parallel", "arbitrary"),
        name="dn",
    )(proj3, proj3, proj3, proj3, ab3, cw, a_log, dt_bias, gn, *state_in)


def _mixout_kernel(a_ref, b_ref, c_ref, d_ref, x_ref, w_ref, g_ref, bb_ref, o_ref, cat_ref):
    for j, part in enumerate((a_ref, b_ref, c_ref, d_ref)):
        cat_ref[:, j * D_GROUP:(j + 1) * D_GROUP] = part[...]
    acc = jnp.dot(cat_ref[...], w_ref[...], preferred_element_type=F32)
    o_ref[...] = _layer_norm(ALPHA * x_ref[...] + acc, g_ref[...], bb_ref[...])


def _mixout(a, b, c, d, x, w, g, bb, l, *, tm):
    m, dm = x.shape
    part = pl.BlockSpec((tm, D_GROUP), lambda i: (i, 0))
    vec = pl.BlockSpec((None, 1, dm), lambda i: (l, 0, 0))
    return pl.pallas_call(
        _mixout_kernel,
        grid=(m // tm,),
        in_specs=[part, part, part, part,
                  pl.BlockSpec((tm, dm), lambda i: (i, 0)),
                  pl.BlockSpec((None, 4 * D_GROUP, dm), lambda i: (l, 0, 0), pipeline_mode=pl.Buffered(1)),
                  vec, vec],
        out_specs=pl.BlockSpec((tm, dm), lambda i: (i, 0)),
        out_shape=jax.ShapeDtypeStruct((m, dm), F32),
        scratch_shapes=[pltpu.VMEM((tm, 4 * D_GROUP), BF16)],
        compiler_params=_params("parallel"),
        name="mixout",
    )(a, b, c, d, x, w, g, bb)


def _block_diag(w):
    dep, h, n, _ = w.shape
    eye = jnp.eye(h, dtype=w.dtype)
    return (eye[None, :, None, :, None] * w[:, :, :, None, :]).reshape(dep, h * n, h * n)


def _prep(p):
    row = lambda v: v.reshape(DEPTH, 1, -1)
    pad128 = lambda v: jnp.pad(row(v), ((0, 0), (0, 0), (0, 128 - v.shape[-1])))
    q = dict(p)
    q['w_mix_out'] = p['w_mix_out'].astype(BF16)
    q['w_mix_in'] = [p['w_mix_in'][l, :, :12 * D_GROUP].astype(BF16) for l in range(DEPTH)]
    q['w_mix_ab'] = jnp.pad(p['w_mix_in'][:, :, 12 * D_GROUP:], ((0, 0), (0, 0), (0, 128 - 2 * N_HEADS))).astype(BF16)
    for name in ('ln1_g', 'ln1_b', 'ln2_g', 'ln2_b', 'ln3_g', 'ln3_b', 'ret_norm_g', 'lru_conv_b',
                 'lru_b_a', 'lru_b_x', 'lru_lam', 'sg_ln_g', 'sg_ln_b', 'dn_norm_g'):
        q[name] = row(p[name])
    q['lru_w_a'] = _block_diag(p['lru_w_a']).astype(BF16)
    q['lru_w_x'] = _block_diag(p['lru_w_x']).astype(BF16)
    q['dn_a_log'] = pad128(p['dn_a_log'])
    q['dn_dt_bias'] = pad128(p['dn_dt_bias'])
    return q


def _token_mixers(x, st, p, l, s):
    bsz, seq, tm, tt, bb = s['bsz'], s['seq'], s['tm'], s['tt'], s['bb']
    c_sg = min(SG_CHUNK, seq)
    sg_w = p['sg_w'][:, :, :c_sg, :c_sg]
    sg_b_t = jnp.swapaxes(p['sg_b'][:, :, :c_sg], 1, 2)
    proj, ab = _proj(x, p['w_mix_in'][l], p['w_mix_ab'], l, tm=tm)
    proj3 = proj.reshape(bsz, seq, -1)
    ab3 = ab.reshape(bsz, seq, 128)
    if st is None:
        s_ret = dn_states = None
        s_lru, s_lru_conv = jnp.zeros((bsz, D_GROUP), F32), jnp.zeros((bsz, CONV_W - 1, D_GROUP), F32)
    else:
        s_ret, s_lru, s_lru_conv, s_dn, s_dn_conv = st
        s_lru, s_lru_conv = s_lru[l], s_lru_conv[l]
        dn_states = (s_dn_conv, s_dn)
    out_a, ret_new = _retention(proj3, s_ret, p['ret_norm_g'], s['pos'], l, tt=tt, bb=bb)
    out_b, lru_h_new, lru_conv_new = _rglru(
        proj3, s_lru_conv, s_lru.reshape(bsz, 1, D_GROUP), p['lru_conv_w'], p['lru_conv_b'],
        p['lru_w_a'], p['lru_b_a'], p['lru_w_x'], p['lru_b_x'], p['lru_lam'], l, tt=tt)
    out_c, sg_v = _spatial_gate(proj3, p['sg_ln_g'], p['sg_ln_b'], sg_w, sg_b_t, l,
                                tt=min(seq, 4 * SG_CHUNK), bb=bb, with_vn=s['keep_sg_v'])
    out_d, dn_new, dn_conv_new = _deltanet(proj3, ab3, dn_states, p['dn_conv_w'],
                                           p['dn_a_log'], p['dn_dt_bias'], p['dn_norm_g'], l, tt=tt, bb=bb)
    flat = lambda o: o.reshape(bsz * seq, D_GROUP)
    x = _mixout(flat(out_a), flat(out_b), flat(out_c), flat(out_d), x, p['w_mix_out'],
                p['ln2_g'], p['ln2_b'], l, tm=tm)
    return x, (ret_new, lru_h_new.reshape(bsz, D_GROUP), lru_conv_new, dn_new, dn_conv_new, sg_v)


def kernel(x_prompt, x_sample, state_ret, state_lru_h, state_lru_conv, state_dn, state_dn_conv, ffn1_w_in, ffn1_w_out, ln1_g, ln1_b, w_mix_in, ret_norm_g, lru_conv_w, lru_conv_b, lru_w_a, lru_b_a, lru_w_x, lru_b_x, lru_lam, sg_ln_g, sg_ln_b, sg_w, sg_b, dn_conv_w, dn_a_log, dn_dt_bias, dn_norm_g, w_mix_out, ln2_g, ln2_b, ffn2_w_in, ffn2_w_out, ln3_g, ln3_b):
    weights = dict(ffn1_w_in=ffn1_w_in, ffn1_w_out=ffn1_w_out, ln1_g=ln1_g, ln1_b=ln1_b, w_mix_in=w_mix_in,
                   ret_norm_g=ret_norm_g, lru_conv_w=lru_conv_w, lru_conv_b=lru_conv_b, lru_w_a=lru_w_a,
                   lru_b_a=lru_b_a, lru_w_x=lru_w_x, lru_b_x=lru_b_x, lru_lam=lru_lam, sg_ln_g=sg_ln_g,
                   sg_ln_b=sg_ln_b, sg_w=sg_w, sg_b=sg_b, dn_conv_w=dn_conv_w, dn_a_log=dn_a_log,
                   dn_dt_bias=dn_dt_bias, dn_norm_g=dn_norm_g, w_mix_out=w_mix_out, ln2_g=ln2_g, ln2_b=ln2_b,
                   ffn2_w_in=ffn2_w_in, ffn2_w_out=ffn2_w_out, ln3_g=ln3_g, ln3_b=ln3_b)
    p = _prep(weights)

    bp, lp, dm = x_prompt.shape
    bs, ls, _ = x_sample.shape
    prompt = dict(bsz=bp, seq=lp, tm=1024, tt=min(lp, 256), bb=1, pos=np.arange(lp), keep_sg_v=False)
    sample = dict(bsz=bs, seq=ls, tm=bs * ls, tt=ls, bb=min(bs, 8), pos=PAST_LEN + np.arange(ls), keep_sg_v=True)
    sample_states = (state_ret, state_lru_h, state_lru_conv, state_dn, state_dn_conv)
    xp = x_prompt.reshape(bp * lp, dm)
    xs = x_sample.reshape(bs * ls, dm)

    ffns = [(l, name) for l in range(DEPTH) for name in ('ffn1', 'ffn2')]
    ln_of = {'ffn1': ('ln1_g', 'ln1_b'), 'ffn2': ('ln3_g', 'ln3_b')}
    wb = (weights['ffn1_w_in'][0].astype(BF16), weights['ffn1_w_out'][0].astype(BF16))

    def ffn_pair(k, xp, xs, wb):
        l, name = ffns[k]
        g, b = p[ln_of[name][0]], p[ln_of[name][1]]
        nxt = None
        if k + 1 < len(ffns):
            ln, nname = ffns[k + 1]
            nxt = (weights[nname + '_w_in'], weights[nname + '_w_out'], ln)
        xp, wb_next = _ffn(xp, wb[0], wb[1], g, b, l, tm=prompt['tm'], cast_next=nxt)
        xs, _ = _ffn(xs, wb[0], wb[1], g, b, l, tm=sample['tm'])
        return xp, xs, wb_next

    st_p, st_s = [], []
    for l in range(DEPTH):
        xp, xs, wb = ffn_pair(2 * l, xp, xs, wb)
        xp, new_p = _token_mixers(xp, None, p, l, prompt)
        xs, new_s = _token_mixers(xs, sample_states, p, l, sample)
        xp, xs, wb = ffn_pair(2 * l + 1, xp, xs, wb)
        st_p.append(new_p)
        st_s.append(new_s)

    stack = lambda sts, i: jnp.stack([s[i] for s in sts])
    return (xp.reshape(bp, lp, dm), xs.reshape(bs, ls, dm),
            stack(st_p, 0), stack(st_p, 1), stack(st_p, 2), stack(st_p, 3), stack(st_p, 4),
            stack(st_s, 0), stack(st_s, 1), stack(st_s, 2), stack(st_s, 3), stack(st_s, 4), stack(st_s, 5))
```

```python
import functools
import math

import numpy as np
import jax
import jax.numpy as jnp
from jax import lax
from jax.experimental import pallas as pl
from jax.experimental.pallas import tpu as pltpu

F32 = jnp.float32
BF16 = jnp.bfloat16

DEPTH = 2
PAST_LEN = 4096
CHUNK = 64
D_GROUP = 512
N_HEADS = 4
D_HEAD = D_GROUP // N_HEADS
H_LRU = 8
LRU_C = 8.0
CONV_W = 4
SG_CHUNK = 128
ROPE_BASE = 10000.0
ALPHA = (2.0 * DEPTH) ** 0.25
EPS = 1e-5

FFN_TF = 512
PROJ_TN = 1024
VMEM_LIMIT = 58 * 1024 * 1024


def _params(*sem):
    return pltpu.CompilerParams(dimension_semantics=sem, vmem_limit_bytes=VMEM_LIMIT)


def _mm(a, b):
    return jnp.dot(a.astype(BF16), b.astype(BF16), preferred_element_type=F32)


def _dot_nt(a, b):
    return lax.dot_general(a, b, (((1,), (1,)), ((), ())), preferred_element_type=F32)


def _bmm(a, b):
    return jnp.einsum('nij,njk->nik', a.astype(BF16), b.astype(BF16), preferred_element_type=F32)


def _bmm_nt(a, b):
    return jnp.einsum('nid,njd->nij', a.astype(BF16), b.astype(BF16), preferred_element_type=F32)


def _layer_norm(y, g, b):
    mu = jnp.mean(y, -1, keepdims=True)
    yc = y - mu
    var = jnp.mean(yc * yc, -1, keepdims=True)
    return yc * lax.rsqrt(var + EPS) * g + b


def _silu(x):
    return x * jax.nn.sigmoid(x)


def _gelu(x):
    return jax.nn.gelu(x, approximate=True)


def _ffn_kernel(x_ref, wg_ref, wu_ref, wo_ref, g_ref, b_ref, *rest, nf, n_row, mode):
    if mode == 'cast_next':
        ng_ref, nu_ref, no_ref, o_ref, ngo_ref, nuo_ref, noo_ref, xb_ref = rest
        ngo_ref[...] = ng_ref[...].astype(BF16)
        nuo_ref[...] = nu_ref[...].astype(BF16)
        noo_ref[...] = no_ref[...].astype(BF16)
    elif mode == 'from_f32':
        o_ref, wgo_ref, wuo_ref, woo_ref, xb_ref = rest
    else:
        o_ref, xb_ref = rest
    f = pl.program_id(1)

    @pl.when(f == 0)
    def _():
        xb_ref[...] = x_ref[...].astype(BF16)
        o_ref[...] = jnp.zeros_like(o_ref)

    wg, wu, wo = wg_ref[...], wu_ref[...], wo_ref[...]
    if mode == 'from_f32':
        wg, wu, wo = wg.astype(BF16), wu.astype(BF16), wo.astype(BF16)
        wgo_ref[...] = wg
        wuo_ref[...] = wu
        woo_ref[...] = wo
    tr = o_ref.shape[0] // n_row
    for r in range(n_row):
        rows = slice(r * tr, (r + 1) * tr)
        xb = xb_ref[rows, :]
        gate = jnp.dot(xb, wg, preferred_element_type=F32)
        up = jnp.dot(xb, wu, preferred_element_type=F32)
        h = (_silu(gate) * up).astype(BF16)
        o_ref[rows, :] += jnp.dot(h, wo, preferred_element_type=F32)

    @pl.when(f == nf - 1)
    def _():
        y = ALPHA * x_ref[...] + 0.5 * o_ref[...]
        o_ref[...] = _layer_norm(y, g_ref[...], b_ref[...])


def _ffn(x, w, g, b, l, *, tm, cast_next=None, from_f32=False):
    m, d = x.shape
    tf = FFN_TF
    d_ff = w[1].shape[1] if from_f32 else w[2].shape[0]
    nf = d_ff // tf
    ni = m // tm
    vec = pl.BlockSpec((None, 1, d), lambda i, f: (l, 0, 0))
    x_spec = pl.BlockSpec((tm, d), lambda i, f: (i, 0), pipeline_mode=pl.Buffered(1))
    tile_g = pl.BlockSpec((d, tf), lambda i, f: (0, f))
    tile_o = pl.BlockSpec((tf, d), lambda i, f: (f, 0))
    w_shapes = [jax.ShapeDtypeStruct((d, d_ff), BF16), jax.ShapeDtypeStruct((d, d_ff), BF16),
                jax.ShapeDtypeStruct((d_ff, d), BF16)]
    out_specs = [pl.BlockSpec((tm, d), lambda i, f: (i, 0))]
    out_shape = [jax.ShapeDtypeStruct((m, d), F32)]
    mode = None
    if from_f32:
        assert ni == 1 and cast_next is None
        mode = 'from_f32'
        w_in, w_out = w
        in_specs = [x_spec,
                    pl.BlockSpec((None, d, tf), lambda i, f: (l, 0, f)),
                    pl.BlockSpec((None, d, tf), lambda i, f: (l, 0, f + nf)),
                    pl.BlockSpec((None, tf, d), lambda i, f: (l, f, 0)), vec, vec]
        args = [x, w_in, w_in, w_out, g, b]
        out_specs += [tile_g, tile_g, tile_o]
        out_shape += w_shapes
    else:
        in_specs = [x_spec, tile_g, tile_g, tile_o, vec, vec]
        args = [x, w[0], w[1], w[2], g, b]
    if cast_next is not None:
        mode = 'cast_next'
        nw_in, nw_out, ln = cast_next
        ri, ro = d // ni, d_ff // (ni * nf)
        assert ri * ni == d and ro * ni * nf == d_ff
        in_specs += [pl.BlockSpec((None, ri, tf), lambda i, f: (ln, i, f)),
                     pl.BlockSpec((None, ri, tf), lambda i, f: (ln, i, f + nf)),
                     pl.BlockSpec((None, ro, d), lambda i, f: (ln, i * nf + f, 0))]
        args += [nw_in, nw_in, nw_out]
        out_specs += [pl.BlockSpec((ri, tf), lambda i, f: (i, f)), pl.BlockSpec((ri, tf), lambda i, f: (i, f)),
                      pl.BlockSpec((ro, d), lambda i, f: (i * nf + f, 0))]
        out_shape += w_shapes
    res = pl.pallas_call(
        functools.partial(_ffn_kernel, nf=nf, n_row=2 if tm >= 1024 else 1, mode=mode),
        grid=(ni, nf),
        in_specs=in_specs,
        out_specs=out_specs,
        out_shape=out_shape,
        scratch_shapes=[pltpu.VMEM((tm, d), BF16)],
        compiler_params=_params("parallel", "arbitrary"),
        name="ffn",
    )(*args)
    return res[0], (tuple(res[1:]) if mode is not None else None)


def _proj_kernel(x_ref, w_ref, wab_ref, o_ref, ab_ref, xb_ref):
    j = pl.program_id(1)

    @pl.when(j == 0)
    def _():
        xb = x_ref[...].astype(BF16)
        xb_ref[...] = xb
        ab_ref[...] = _dot_nt(xb, wab_ref[...])

    o_ref[...] = _dot_nt(xb_ref[...], w_ref[...])


def _proj(x, w_t, wab_t, l, *, tm):
    m, d = x.shape
    tn = PROJ_TN
    n = 12 * D_GROUP
    return pl.pallas_call(
        _proj_kernel,
        grid=(m // tm, n // tn),
        in_specs=[
            pl.BlockSpec((tm, d), lambda i, j: (i, 0)),
            pl.BlockSpec((None, tn, d), lambda i, j: (l, j, 0)),
            pl.BlockSpec((None, 128, d), lambda i, j: (l, 0, 0)),
        ],
        out_specs=[
            pl.BlockSpec((tm, tn), lambda i, j: (i, j)),
            pl.BlockSpec((tm, 128), lambda i, j: (i, 0)),
        ],
        out_shape=[jax.ShapeDtypeStruct((m, n), F32), jax.ShapeDtypeStruct((m, 128), F32)],
        scratch_shapes=[pltpu.VMEM((tm, d), BF16)],
        compiler_params=_params("parallel", "arbitrary"),
        name="proj",
    )(x, w_t, wab_t)


def _ret_tables(pos, c):
    half = D_HEAD // 2
    inv = ROPE_BASE ** (-np.arange(half, dtype=np.float64) / half)
    ang = pos.astype(np.float64)[:, None] * inv[None, :]
    cos = np.concatenate([np.cos(ang), np.cos(ang)], -1)
    sin = np.concatenate([-np.sin(ang), np.sin(ang)], -1)
    log_g = np.log1p(-np.exp2(-5.0 - np.arange(N_HEADS, dtype=np.float64)))
    idx = np.arange(c, dtype=np.float64)
    diff = idx[:, None] - idx[None, :]
    scale = D_HEAD ** -0.5
    dmask = np.where(diff >= 0, np.exp(log_g[:, None, None] * np.maximum(diff, 0.0)), 0.0) * scale
    kdec = np.exp(log_g[:, None] * (c - 1.0 - idx)[None, :]) * scale
    qdec = np.exp(log_g[:, None] * (idx + 1.0)[None, :])
    kdec = np.broadcast_to(kdec[:, :, None], (N_HEADS, c, D_HEAD))
    qdec = np.broadcast_to(qdec[:, :, None], (N_HEADS, c, D_HEAD))
    g_c = np.exp(log_g * c)
    as32 = lambda a: jnp.asarray(np.ascontiguousarray(a), F32)
    return as32(cos), as32(sin), as32(dmask), as32(qdec), as32(kdec), [float(v) for v in g_c]


def _ret_kernel(q_ref, k_ref, v_ref, g_ref, cos_ref, sin_ref, dm_ref, qd_ref, kd_ref, gn_ref, *rest,
                c, n, g_c, n_t, bb, has_state):
    if has_state:
        s0_ref, o_ref, sf_ref, s_scr, sst_scr = rest
    else:
        o_ref, sf_ref, s_scr, sst_scr = rest
    t = pl.program_id(1)
    heads = range(N_HEADS)
    lanes = [slice(h * D_HEAD, (h + 1) * D_HEAD) for h in heads]
    tt = n * c

    @pl.when(t == 0)
    def _():
        s_scr[...] = s0_ref[...] if has_state else jnp.zeros_like(s_scr)

    cos = jnp.concatenate([cos_ref[...]] * bb, axis=0)
    sin = jnp.concatenate([sin_ref[...]] * bb, axis=0)

    def rows(ref, h):
        return ref[:, :, lanes[h]].reshape(bb * tt, D_HEAD)

    def rot(x):
        return (x * cos + pltpu.roll(x, D_HEAD // 2, 1) * sin).reshape(bb * n, c, D_HEAD)

    q3 = [rot(rows(q_ref, h)) for h in heads]
    k3 = [rot(rows(k_ref, h)) for h in heads]
    v3 = [rows(v_ref, h).reshape(bb * n, c, D_HEAD) for h in heads]
    scores = [_bmm_nt(q3[h], k3[h]) * dm_ref[h] for h in heads]
    o_intra = [_bmm(scores[h], v3[h]) for h in heads]
    kdt = [jnp.swapaxes(k3[h] * kd_ref[h], 1, 2) for h in heads]
    upd = [_bmm(kdt[h], v3[h]) for h in heads]
    for b in range(bb):
        for h in heads:
            s = s_scr[b, h]
            for i in range(n):
                sst_scr[h, b * n + i] = s.astype(BF16)
                s = g_c[h] * s + upd[h][b * n + i]
            s_scr[b, h] = s
    for h in heads:
        o = o_intra[h] + _bmm(q3[h] * qd_ref[h], sst_scr[h])
        mu = jnp.mean(o, -1, keepdims=True)
        oc = o - mu
        var = jnp.mean(oc * oc, -1, keepdims=True)
        on = (oc * lax.rsqrt(var + EPS)).reshape(bb * tt, D_HEAD) * gn_ref[:, lanes[h]]
        o_ref[:, :, lanes[h]] = (on * _silu(rows(g_ref, h))).reshape(bb, tt, D_HEAD).astype(o_ref.dtype)

    @pl.when(t == n_t - 1)
    def _():
        sf_ref[...] = s_scr[...]


def _retention(proj3, state0, gn, pos, l, *, tt, bb):
    bsz, seq, _ = proj3.shape
    c = min(CHUNK, seq)
    cos, sin, dmask, qdec, kdec, g_c = _ret_tables(pos, c)
    n_t = seq // tt
    n = tt // c
    col = lambda j: pl.BlockSpec((bb, tt, D_GROUP), lambda b, t: (b, t, j))
    full3 = pl.BlockSpec((N_HEADS, c, c), lambda b, t: (0, 0, 0))
    full3d = pl.BlockSpec((N_HEADS, c, D_HEAD), lambda b, t: (0, 0, 0))
    st = pl.BlockSpec((bb, N_HEADS, D_HEAD, D_HEAD), lambda b, t: (b, 0, 0, 0))
    has_state = state0 is not None
    state_in = [state0] if has_state else []
    state_spec = ([pl.BlockSpec((None, bb, N_HEADS, D_HEAD, D_HEAD), lambda b, t: (l, b, 0, 0, 0))]
                  if has_state else [])
    return pl.pallas_call(
        functools.partial(_ret_kernel, c=c, n=n, g_c=g_c, n_t=n_t, bb=bb, has_state=has_state),
        grid=(bsz // bb, n_t),
        in_specs=[col(0), col(1), col(2), col(3),
                  pl.BlockSpec((tt, D_HEAD), lambda b, t: (t, 0)),
                  pl.BlockSpec((tt, D_HEAD), lambda b, t: (t, 0)),
                  full3, full3d, full3d,
                  pl.BlockSpec((None, 1, D_GROUP), lambda b, t: (l, 0, 0))] + state_spec,
        out_specs=[pl.BlockSpec((bb, tt, D_GROUP), lambda b, t: (b, t, 0)), st],
        out_shape=[jax.ShapeDtypeStruct((bsz, seq, D_GROUP), BF16),
                   jax.ShapeDtypeStruct((bsz, N_HEADS, D_HEAD, D_HEAD), F32)],
        scratch_shapes=[pltpu.VMEM((bb, N_HEADS, D_HEAD, D_HEAD), F32),
                        pltpu.VMEM((N_HEADS, bb * n, D_HEAD, D_HEAD), BF16)],
        compiler_params=_params("parallel", "arbitrary"),
        name="ret",
    )(proj3, proj3, proj3, proj3, cos, sin, dmask, qdec, kdec, gn, *state_in)


PAD = 8


def _conv_block(ext_ref, x, w_ref, tt):
    ext_ref[PAD:PAD + tt, :] = x
    out = x * w_ref[CONV_W - 1:CONV_W, :]
    for j in range(1, CONV_W):
        out = out + ext_ref[PAD - j:PAD - j + tt, :] * w_ref[CONV_W - 1 - j:CONV_W - j, :]
    ext_ref[PAD - (CONV_W - 1):PAD, :] = x[tt - (CONV_W - 1):, :]
    return out


def _lru_kernel(y_ref, x_ref, cs_ref, h0_ref, cw_ref, cb_ref, wa_ref, ba_ref, wx_ref, bx_ref, lam_ref,
                o_ref, hn_ref, cn_ref, ext_ref, a_scr, b_scr, h_scr, hc_ref, *, tt, n_t):
    t = pl.program_id(1)

    @pl.when(t == 0)
    def _():
        ext_ref[PAD - (CONV_W - 1):PAD, :] = cs_ref[0]
        hc_ref[...] = h0_ref[0]

    x = x_ref[0]
    xc = _conv_block(ext_ref, x, cw_ref, tt) + cb_ref[...]
    r = jax.nn.sigmoid(_mm(xc, wa_ref[...]) + ba_ref[...])
    i = jax.nn.sigmoid(_mm(xc, wx_ref[...]) + bx_ref[...])
    log_a = -LRU_C * r * jax.nn.softplus(-lam_ref[...])
    a_scr[...] = jnp.exp(log_a)
    th = jnp.tanh(log_a)
    one_minus_a2 = -2.0 * th / (1.0 - th)
    b_scr[...] = jnp.sqrt(one_minus_a2) * (i * xc)

    def step(s, h):
        row = pl.ds(s, 1)
        h = a_scr[row, :] * h + b_scr[row, :]
        h_scr[row, :] = h
        return h

    hc_ref[...] = lax.fori_loop(0, tt, step, hc_ref[...], unroll=8)
    o_ref[0] = (h_scr[...] * _gelu(y_ref[0])).astype(o_ref.dtype)

    @pl.when(t == n_t - 1)
    def _():
        hn_ref[0] = hc_ref[...]
        cn_ref[0] = x[tt - (CONV_W - 1):, :]


def _rglru(proj3, conv_state, h0, cw, cb, wa, ba, wx, bx, lam, l, *, tt):
    bsz, seq, _ = proj3.shape
    n_t = seq // tt
    col = lambda j: pl.BlockSpec((1, tt, D_GROUP), lambda b, t: (b, t, j))
    vec = pl.BlockSpec((None, 1, D_GROUP), lambda b, t: (l, 0, 0))
    sq = pl.BlockSpec((None, D_GROUP, D_GROUP), lambda b, t: (l, 0, 0))
    cs = pl.BlockSpec((1, CONV_W - 1, D_GROUP), lambda b, t: (b, 0, 0))
    hs = pl.BlockSpec((1, 1, D_GROUP), lambda b, t: (b, 0, 0))
    return pl.pallas_call(
        functools.partial(_lru_kernel, tt=tt, n_t=n_t),
        grid=(bsz, n_t),
        in_specs=[col(4), col(5), cs, hs,
                  pl.BlockSpec((None, CONV_W, D_GROUP), lambda b, t: (l, 0, 0)), vec, sq, vec, sq, vec, vec],
        out_specs=[pl.BlockSpec((1, tt, D_GROUP), lambda b, t: (b, t, 0)), hs, cs],
        out_shape=[jax.ShapeDtypeStruct((bsz, seq, D_GROUP), BF16),
                   jax.ShapeDtypeStruct((bsz, 1, D_GROUP), F32),
                   jax.ShapeDtypeStruct((bsz, CONV_W - 1, D_GROUP), F32)],
        scratch_shapes=[pltpu.VMEM((tt + PAD, D_GROUP), F32), pltpu.VMEM((tt, D_GROUP), F32),
                        pltpu.VMEM((tt, D_GROUP), F32), pltpu.VMEM((tt, D_GROUP), F32),
                        pltpu.VMEM((1, D_GROUP), F32)],
        compiler_params=_params("parallel", "arbitrary"),
        name="lru",
    )(proj3, proj3, conv_state, h0, cw, cb, wa, ba, wx, bx, lam)


def _sg_kernel(u_ref, v_ref, lg_ref, lb_ref, ws_ref, bs_ref, o_ref, *vn_ref, c, tt, bb):
    u = _gelu(u_ref[...])
    vn = _layer_norm(_gelu(v_ref[...]), lg_ref[...], lb_ref[...])
    if vn_ref:
        vn_ref[0][...] = vn
    row = lax.broadcasted_iota(jnp.int32, (c, c), 0)
    col = lax.broadcasted_iota(jnp.int32, (c, c), 1)
    mask = (col // CHUNK) <= (row // CHUNK)
    for h in range(N_HEADS):
        lanes = slice(h * D_HEAD, (h + 1) * D_HEAD)
        w = jnp.where(mask, ws_ref[h], 0.0).astype(BF16)
        bias = bs_ref[:, h:h + 1]
        for b in range(bb):
            for i in range(tt // c):
                rows = slice(i * c, (i + 1) * c)
                s = _mm(w, vn[b, rows, lanes]) + bias
                o_ref[b, rows, lanes] = (u[b, rows, lanes] * s).astype(o_ref.dtype)


def _spatial_gate(proj3, lg, lb, ws, bs_t, l, *, tt, bb, with_vn):
    bsz, seq, _ = proj3.shape
    c = min(SG_CHUNK, seq)
    col = lambda j: pl.BlockSpec((bb, tt, D_GROUP), lambda b, t: (b, t, j))
    vec = pl.BlockSpec((None, 1, D_GROUP), lambda b, t: (l, 0, 0))
    out = pl.BlockSpec((bb, tt, D_GROUP), lambda b, t: (b, t, 0))
    res = pl.pallas_call(
        functools.partial(_sg_kernel, c=c, tt=tt, bb=bb),
        grid=(bsz // bb, seq // tt),
        in_specs=[col(6), col(7), vec, vec,
                  pl.BlockSpec((None, N_HEADS, c, c), lambda b, t: (l, 0, 0, 0)),
                  pl.BlockSpec((None, c, N_HEADS), lambda b, t: (l, 0, 0))],
        out_specs=[out, out] if with_vn else [out],
        out_shape=[jax.ShapeDtypeStruct((bsz, seq, D_GROUP), BF16)]
                  + ([jax.ShapeDtypeStruct((bsz, seq, D_GROUP), F32)] if with_vn else []),
        compiler_params=_params("parallel", "parallel"),
        name="sg",
    )(proj3, proj3, lg, lb, ws, bs_t)
    return (res[0], res[1]) if with_vn else (res[0], None)


INV_BLOCK = 16


def _unit_lower_inverse(mats, c):
    row = lax.broadcasted_iota(jnp.int32, (c, c), 0)
    col = lax.broadcasted_iota(jnp.int32, (c, c), 1)
    eye = (row == col).astype(F32)
    same = row // INV_BLOCK == col // INV_BLOCK
    ps = [jnp.where(same, a, 0.0) for a in mats]
    xs = [eye - p for p in ps]
    ps = [_bmm(p, p) for p in ps]
    n_sq = int(math.log2(INV_BLOCK)) - 1
    for step in range(n_sq):
        last = step == n_sq - 1
        lhs = xs if last else [jnp.concatenate([x, p], axis=1) for x, p in zip(xs, ps)]
        prod = [_bmm(l, p) for l, p in zip(lhs, ps)]
        xs = [x + r[:, :c, :] for x, r in zip(xs, prod)]
        if not last:
            ps = [r[:, c:, :] for r in prod]
    size = INV_BLOCK
    while size < c:
        big = 2 * size
        sel = (row // big == col // big) & (row // size != col // size)
        offs = [jnp.where(sel, a, 0.0) for a in mats]
        tmp = [_bmm(o, x) for o, x in zip(offs, xs)]
        xs = [x - _bmm(x, t) for x, t in zip(xs, tmp)]
        size = big
    return xs


def _dn_kernel(q_ref, k_ref, v_ref, z_ref, ab_ref, cw_ref, al_ref, dt_ref, gn_ref, *rest,
               c, n, n_t, bb, has_state):
    if has_state:
        cs_ref, s0_ref = rest[:2]
        rest = rest[2:]
    o_ref, sf_ref, cn_ref, ext_ref, s_scr, u_scr, wq_scr, attn_scr, kdt_scr, egl_scr = rest
    t = pl.program_id(1)
    tt = n * c
    nb = bb * n
    heads = range(N_HEADS)

    @pl.when(t == 0)
    def _():
        for j in range(3):
            hist = (cs_ref[:, :, j * D_GROUP:(j + 1) * D_GROUP] if has_state
                    else jnp.zeros((bb, CONV_W - 1, D_GROUP), F32))
            ext_ref[j, :, PAD - (CONV_W - 1):PAD, :] = hist
        s_scr[...] = s0_ref[...] if has_state else jnp.zeros_like(s_scr)

    raw = (q_ref, k_ref, v_ref)
    qf, kf, vf = [
        jnp.concatenate([_silu(_conv_block(ext_ref.at[j, b], raw[j][b],
                                           cw_ref.at[:, j * D_GROUP:(j + 1) * D_GROUP], tt))
                         for b in range(bb)], axis=0)
        for j in range(3)]

    row = lax.broadcasted_iota(jnp.int32, (c, c), 0)
    col = lax.broadcasted_iota(jnp.int32, (c, c), 1)
    tri = row >= col
    strict = row > col
    tri_f = tri.astype(F32)
    scale = D_HEAD ** -0.5

    ab = ab_ref[...].reshape(bb * tt, 128)
    g_all = -jnp.exp(al_ref[...]) * jax.nn.softplus(ab + dt_ref[...])
    beta_all = jax.nn.sigmoid(ab)
    gc_all = jnp.concatenate(
        [jnp.dot(tri_f, g_all[i * c:(i + 1) * c], preferred_element_type=F32, precision=lax.Precision.HIGHEST)
         for i in range(nb)], axis=0)
    gc_t = gc_all.T

    def l2n(x):
        return x * lax.rsqrt(jnp.sum(x * x, -1, keepdims=True) + 1e-6)

    def split(x, h):
        return x[:, h * D_HEAD:(h + 1) * D_HEAD].reshape(nb, c, D_HEAD)

    q3 = [l2n(split(qf, h)) * scale for h in heads]
    k3 = [l2n(split(kf, h)) for h in heads]
    v3 = [split(vf, h) for h in heads]
    beta = [beta_all[:, N_HEADS + h:N_HEADS + h + 1].reshape(nb, c, 1) for h in heads]
    gcol = [gc_all[:, h:h + 1].reshape(nb, c, 1) for h in heads]
    grow = [jnp.stack([gc_t[h:h + 1, i * c:(i + 1) * c] for i in range(nb)]) for h in heads]
    glast = [g[:, c - 1:c, :] for g in gcol]
    decay = [jnp.where(tri, jnp.exp(gc - gr), 0.0) for gc, gr in zip(gcol, grow)]
    egc = [jnp.exp(g) for g in gcol]
    kb = [k * b for k, b in zip(k3, beta)]
    vb = [v * b for v, b in zip(v3, beta)]
    a_mat = [jnp.where(strict, _bmm_nt(x, k) * d, 0.0) for x, k, d in zip(kb, k3, decay)]
    tinv = _unit_lower_inverse(a_mat, c)
    uw = [_bmm(ti, jnp.concatenate([x, y * e], axis=-1)) for ti, x, y, e in zip(tinv, vb, kb, egc)]
    attn = [_bmm_nt(q, k) * d for q, k, d in zip(q3, k3, decay)]
    for h in heads:
        u_scr[h] = uw[h][:, :, :D_HEAD]
        wq_scr[h] = jnp.concatenate([uw[h][:, :, D_HEAD:], q3[h] * egc[h]], axis=1).astype(BF16)
        attn_scr[h] = attn[h].astype(BF16)
        kdt_scr[h] = jnp.swapaxes(k3[h] * jnp.exp(glast[h] - gcol[h]), 1, 2).astype(BF16)
        egl_scr[h] = jnp.broadcast_to(jnp.exp(glast[h]), (nb, 1, D_HEAD))

    pairs = [(b, h) for b in range(bb) for h in heads]

    def chunk(i, carry):
        rows = pl.ds(pl.multiple_of(i * c, c), c)
        s = [s_scr[b, h] for b, h in pairs]
        r = [_mm(wq_scr[h, b * n + i], s[p]) for p, (b, h) in enumerate(pairs)]
        v_new = [u_scr[h, b * n + i] - r[p][:c] for p, (b, h) in enumerate(pairs)]
        for p, (b, h) in enumerate(pairs):
            s_scr[b, h] = s[p] * egl_scr[h, b * n + i] + _mm(kdt_scr[h, b * n + i], v_new[p])
        o = [r[p][c:] + _mm(attn_scr[h, b * n + i], v_new[p]) for p, (b, h) in enumerate(pairs)]
        for p, (b, h) in enumerate(pairs):
            lanes = slice(h * D_HEAD, (h + 1) * D_HEAD)
            on = o[p] * lax.rsqrt(jnp.mean(o[p] * o[p], -1, keepdims=True) + EPS) * gn_ref[...]
            o_ref[b, rows, lanes] = (on * _silu(z_ref[b, rows, lanes])).astype(o_ref.dtype)
        return carry

    lax.fori_loop(0, n, chunk, 0)

    @pl.when(t == n_t - 1)
    def _():
        sf_ref[...] = s_scr[...]
        for j in range(3):
            cn_ref[:, :, j * D_GROUP:(j + 1) * D_GROUP] = raw[j][:, tt - (CONV_W - 1):, :]


def _deltanet(proj3, ab3, states, cw, a_log, dt_bias, gn, l, *, tt, bb):
    bsz, seq, _ = proj3.shape
    c = min(CHUNK, seq)
    n_t = seq // tt
    n = tt // c
    nb = bb * n
    col = lambda j: pl.BlockSpec((bb, tt, D_GROUP), lambda b, t: (b, t, j))
    st = pl.BlockSpec((bb, N_HEADS, D_HEAD, D_HEAD), lambda b, t: (b, 0, 0, 0))
    cs = pl.BlockSpec((bb, CONV_W - 1, 3 * D_GROUP), lambda b, t: (b, 0, 0))
    lane_row = pl.BlockSpec((None, 1, 128), lambda b, t: (l, 0, 0))
    has_state = states is not None
    state_in = list(states) if has_state else []
    state_spec = ([pl.BlockSpec((None, bb, CONV_W - 1, 3 * D_GROUP), lambda b, t: (l, b, 0, 0)),
                   pl.BlockSpec((None, bb, N_HEADS, D_HEAD, D_HEAD), lambda b, t: (l, b, 0, 0, 0))]
                  if has_state else [])
    return pl.pallas_call(
        functools.partial(_dn_kernel, c=c, n=n, n_t=n_t, bb=bb, has_state=has_state),
        grid=(bsz // bb, n_t),
        in_specs=[col(8), col(9), col(10), col(11),
                  pl.BlockSpec((bb, tt, 128), lambda b, t: (b, t, 0)),
                  pl.BlockSpec((None, CONV_W, 3 * D_GROUP), lambda b, t: (l, 0, 0)),
                  lane_row, lane_row, lane_row] + state_spec,
        out_specs=[pl.BlockSpec((bb, tt, D_GROUP), lambda b, t: (b, t, 0)), st, cs],
        out_shape=[jax.ShapeDtypeStruct((bsz, seq, D_GROUP), BF16),
                   jax.ShapeDtypeStruct((bsz, N_HEADS, D_HEAD, D_HEAD), F32),
                   jax.ShapeDtypeStruct((bsz, CONV_W - 1, 3 * D_GROUP), F32)],
        scratch_shapes=[pltpu.VMEM((3, bb, tt + PAD, D_GROUP), F32),
                        pltpu.VMEM((bb, N_HEADS, D_HEAD, D_HEAD), F32),
                        pltpu.VMEM((N_HEADS, nb, c, D_HEAD), F32),
                        pltpu.VMEM((N_HEADS, nb, 2 * c, D_HEAD), BF16),
                        pltpu.VMEM((N_HEADS, nb, c, c), BF16),
                        pltpu.VMEM((N_HEADS, nb, D_HEAD, c), BF16),
                        pltpu.VMEM((N_HEADS, nb, 1, D_HEAD), F32)],
        compiler_params=_params("parallel", "arbitrary"),
        name="dn",
    )(proj3, proj3, proj3, proj3, ab3, cw, a_log, dt_bias, gn, *state_in)


def _mixout_kernel(a_ref, b_ref, c_ref, d_ref, x_ref, w_ref, g_ref, bb_ref, o_ref, cat_ref):
    for j, part in enumerate((a_ref, b_ref, c_ref, d_ref)):
        cat_ref[:, j * D_GROUP:(j + 1) * D_GROUP] = part[...]
    acc = jnp.dot(cat_ref[...], w_ref[...], preferred_element_type=F32)
    o_ref[...] = _layer_norm(ALPHA * x_ref[...] + acc, g_ref[...], bb_ref[...])


def _mixout(a, b, c, d, x, w, g, bb, l, *, tm):
    m, dm = x.shape
    part = pl.BlockSpec((tm, D_GROUP), lambda i: (i, 0))
    vec = pl.BlockSpec((None, 1, dm), lambda i: (l, 0, 0))
    return pl.pallas_call(
        _mixout_kernel,
        grid=(m // tm,),
        in_specs=[part, part, part, part,
                  pl.BlockSpec((tm, dm), lambda i: (i, 0)),
                  pl.BlockSpec((None, 4 * D_GROUP, dm), lambda i: (l, 0, 0), pipeline_mode=pl.Buffered(1)),
                  vec, vec],
        out_specs=pl.BlockSpec((tm, dm), lambda i: (i, 0)),
        out_shape=jax.ShapeDtypeStruct((m, dm), F32),
        scratch_shapes=[pltpu.VMEM((tm, 4 * D_GROUP), BF16)],
        compiler_params=_params("parallel"),
        name="mixout",
    )(a, b, c, d, x, w, g, bb)


def _block_diag(w):
    dep, h, n, _ = w.shape
    eye = jnp.eye(h, dtype=w.dtype)
    return (eye[None, :, None, :, None] * w[:, :, :, None, :]).reshape(dep, h * n, h * n)


def _prep(p):
    row = lambda v: v.reshape(DEPTH, 1, -1)
    pad128 = lambda v: jnp.pad(row(v), ((0, 0), (0, 0), (0, 128 - v.shape[-1])))
    q = dict(p)
    q['w_mix_out'] = p['w_mix_out'].astype(BF16)
    w_in_t = jnp.swapaxes(p['w_mix_in'], 1, 2)
    q['w_mix_in'] = w_in_t.astype(BF16)
    q['w_mix_ab'] = jnp.pad(w_in_t[:, 12 * D_GROUP:, :], ((0, 0), (0, 128 - 2 * N_HEADS), (0, 0))).astype(BF16)
    for name in ('ln1_g', 'ln1_b', 'ln2_g', 'ln2_b', 'ln3_g', 'ln3_b', 'ret_norm_g', 'lru_conv_b',
                 'lru_b_a', 'lru_b_x', 'lru_lam', 'sg_ln_g', 'sg_ln_b', 'dn_norm_g'):
        q[name] = row(p[name])
    q['lru_w_a'] = _block_diag(p['lru_w_a']).astype(BF16)
    q['lru_w_x'] = _block_diag(p['lru_w_x']).astype(BF16)
    q['dn_a_log'] = pad128(p['dn_a_log'])
    q['dn_dt_bias'] = pad128(p['dn_dt_bias'])
    return q


def _token_mixers(x, st, p, l, s):
    bsz, seq, tm, tt, bb = s['bsz'], s['seq'], s['tm'], s['tt'], s['bb']
    c_sg = min(SG_CHUNK, seq)
    sg_w = p['sg_w'][:, :, :c_sg, :c_sg]
    sg_b_t = jnp.swapaxes(p['sg_b'][:, :, :c_sg], 1, 2)
    proj, ab = _proj(x, p['w_mix_in'], p['w_mix_ab'], l, tm=tm)
    proj3 = proj.reshape(bsz, seq, -1)
    ab3 = ab.reshape(bsz, seq, 128)
    if st is None:
        s_ret = dn_states = None
        s_lru, s_lru_conv = jnp.zeros((bsz, D_GROUP), F32), jnp.zeros((bsz, CONV_W - 1, D_GROUP), F32)
    else:
        s_ret, s_lru, s_lru_conv, s_dn, s_dn_conv = st
        s_lru, s_lru_conv = s_lru[l], s_lru_conv[l]
        dn_states = (s_dn_conv, s_dn)
    out_a, ret_new = _retention(proj3, s_ret, p['ret_norm_g'], s['pos'], l, tt=tt, bb=bb)
    out_b, lru_h_new, lru_conv_new = _rglru(
        proj3, s_lru_conv, s_lru.reshape(bsz, 1, D_GROUP), p['lru_conv_w'], p['lru_conv_b'],
        p['lru_w_a'], p['lru_b_a'], p['lru_w_x'], p['lru_b_x'], p['lru_lam'], l, tt=tt)
    out_c, sg_v = _spatial_gate(proj3, p['sg_ln_g'], p['sg_ln_b'], sg_w, sg_b_t, l,
                                tt=min(seq, 4 * SG_CHUNK), bb=bb, with_vn=s['keep_sg_v'])
    out_d, dn_new, dn_conv_new = _deltanet(proj3, ab3, dn_states, p['dn_conv_w'],
                                           p['dn_a_log'], p['dn_dt_bias'], p['dn_norm_g'], l, tt=tt, bb=bb)
    flat = lambda o: o.reshape(bsz * seq, D_GROUP)
    x = _mixout(flat(out_a), flat(out_b), flat(out_c), flat(out_d), x, p['w_mix_out'],
                p['ln2_g'], p['ln2_b'], l, tm=tm)
    return x, (ret_new, lru_h_new.reshape(bsz, D_GROUP), lru_conv_new, dn_new, dn_conv_new, sg_v)


def kernel(x_prompt, x_sample, state_ret, state_lru_h, state_lru_conv, state_dn, state_dn_conv, ffn1_w_in, ffn1_w_out, ln1_g, ln1_b, w_mix_in, ret_norm_g, lru_conv_w, lru_conv_b, lru_w_a, lru_b_a, lru_w_x, lru_b_x, lru_lam, sg_ln_g, sg_ln_b, sg_w, sg_b, dn_conv_w, dn_a_log, dn_dt_bias, dn_norm_g, w_mix_out, ln2_g, ln2_b, ffn2_w_in, ffn2_w_out, ln3_g, ln3_b):
    weights = dict(ffn1_w_in=ffn1_w_in, ffn1_w_out=ffn1_w_out, ln1_g=ln1_g, ln1_b=ln1_b, w_mix_in=w_mix_in,
                   ret_norm_g=ret_norm_g, lru_conv_w=lru_conv_w, lru_conv_b=lru_conv_b, lru_w_a=lru_w_a,
                   lru_b_a=lru_b_a, lru_w_x=lru_w_x, lru_b_x=lru_b_x, lru_lam=lru_lam, sg_ln_g=sg_ln_g,
                   sg_ln_b=sg_ln_b, sg_w=sg_w, sg_b=sg_b, dn_conv_w=dn_conv_w, dn_a_log=dn_a_log,
                   dn_dt_bias=dn_dt_bias, dn_norm_g=dn_norm_g, w_mix_out=w_mix_out, ln2_g=ln2_g, ln2_b=ln2_b,
                   ffn2_w_in=ffn2_w_in, ffn2_w_out=ffn2_w_out, ln3_g=ln3_g, ln3_b=ln3_b)
    p = _prep(weights)

    bp, lp, dm = x_prompt.shape
    bs, ls, _ = x_sample.shape
    prompt = dict(bsz=bp, seq=lp, tm=1024, tt=min(lp, 256), bb=1, pos=np.arange(lp), keep_sg_v=False)
    sample = dict(bsz=bs, seq=ls, tm=bs * ls, tt=ls, bb=min(bs, 8), pos=PAST_LEN + np.arange(ls), keep_sg_v=True)
    sample_states = (state_ret, state_lru_h, state_lru_conv, state_dn, state_dn_conv)
    xp = x_prompt.reshape(bp * lp, dm)
    xs = x_sample.reshape(bs * ls, dm)

    ffns = [(l, name) for l in range(DEPTH) for name in ('ffn1', 'ffn2')]
    ln_of = {'ffn1': ('ln1_g', 'ln1_b'), 'ffn2': ('ln3_g', 'ln3_b')}

    def ffn_pair(k, xp, xs, wb):
        l, name = ffns[k]
        g, b = p[ln_of[name][0]], p[ln_of[name][1]]
        nxt = None
        if k + 1 < len(ffns):
            ln, nname = ffns[k + 1]
            nxt = (weights[nname + '_w_in'], weights[nname + '_w_out'], ln)
        if wb is None:
            xs, wb = _ffn(xs, (weights[name + '_w_in'], weights[name + '_w_out']), g, b, l, tm=sample['tm'],
                          from_f32=True)
        else:
            xs, _ = _ffn(xs, wb, g, b, l, tm=sample['tm'])
        xp, wb_next = _ffn(xp, wb, g, b, l, tm=prompt['tm'], cast_next=nxt)
        return xp, xs, wb_next

    wb = None

    st_p, st_s = [], []
    for l in range(DEPTH):
        xp, xs, wb = ffn_pair(2 * l, xp, xs, wb)
        xp, new_p = _token_mixers(xp, None, p, l, prompt)
        xs, new_s = _token_mixers(xs, sample_states, p, l, sample)
        xp, xs, wb = ffn_pair(2 * l + 1, xp, xs, wb)
        st_p.append(new_p)
        st_s.append(new_s)

    stack = lambda sts, i: jnp.stack([s[i] for s in sts])
    return (xp.reshape(bp, lp, dm), xs.reshape(bs, ls, dm),
            stack(st_p, 0), stack(st_p, 1), stack(st_p, 2), stack(st_p, 3), stack(st_p, 4),
            stack(st_s, 0), stack(st_s, 1), stack(st_s, 2), stack(st_s, 3), stack(st_s, 4), stack(st_s, 5))
```

```python
import functools
import math

import numpy as np
import jax
import jax.numpy as jnp
from jax import lax
from jax.experimental import pallas as pl
from jax.experimental.pallas import tpu as pltpu

F32 = jnp.float32
BF16 = jnp.bfloat16

DEPTH = 2
PAST_LEN = 4096
CHUNK = 64
D_GROUP = 512
N_HEADS = 4
D_HEAD = D_GROUP // N_HEADS
H_LRU = 8
LRU_C = 8.0
CONV_W = 4
SG_CHUNK = 128
ROPE_BASE = 10000.0
ALPHA = (2.0 * DEPTH) ** 0.25
EPS = 1e-5

FFN_TF = 512
PROJ_TN = 1024
VMEM_LIMIT = 58 * 1024 * 1024


def _params(*sem):
    return pltpu.CompilerParams(dimension_semantics=sem, vmem_limit_bytes=VMEM_LIMIT)


def _mm(a, b):
    return jnp.dot(a.astype(BF16), b.astype(BF16), preferred_element_type=F32)


def _dot_nt(a, b):
    return lax.dot_general(a, b, (((1,), (1,)), ((), ())), preferred_element_type=F32)


def _bmm(a, b):
    return jnp.einsum('nij,njk->nik', a.astype(BF16), b.astype(BF16), preferred_element_type=F32)


def _bmm_nt(a, b):
    return jnp.einsum('nid,njd->nij', a.astype(BF16), b.astype(BF16), preferred_element_type=F32)


def _layer_norm(y, g, b):
    mu = jnp.mean(y, -1, keepdims=True)
    yc = y - mu
    var = jnp.mean(yc * yc, -1, keepdims=True)
    return yc * lax.rsqrt(var + EPS) * g + b


def _silu(x):
    return x * jax.nn.sigmoid(x)


def _gelu(x):
    return jax.nn.gelu(x, approximate=True)


def _ffn_kernel(x_hbm, wg_ref, wu_ref, wo_ref, g_ref, b_ref, *rest, nf, ni, n_row, mode):
    *rest, xb_ref, xf_ref, x_sem = rest
    if mode == 'cast_next':
        ng_ref, nu_ref, no_ref, o_ref, ngo_ref, nuo_ref, noo_ref = rest
        ngo_ref[...] = ng_ref[...].astype(BF16)
        nuo_ref[...] = nu_ref[...].astype(BF16)
        noo_ref[...] = no_ref[...].astype(BF16)
    elif mode == 'from_f32':
        o_ref, wgo_ref, wuo_ref, woo_ref = rest
    else:
        (o_ref,) = rest
    i = pl.program_id(0)
    f = pl.program_id(1)
    tm = xf_ref.shape[0]

    def x_copy(k):
        return pltpu.make_async_copy(x_hbm.at[pl.ds(k * tm, tm), :], xf_ref, x_sem)

    @pl.when(f == 0)
    def _():
        @pl.when(i == 0)
        def _():
            x_copy(0).start()

        x_copy(i).wait()
        x = xf_ref[...]
        xb_ref[...] = x.astype(BF16)
        o_ref[...] = (2.0 * ALPHA) * x

    @pl.when((f == 1) & (i + 1 < ni))
    def _():
        x_copy(i + 1).start()

    wg, wu, wo = wg_ref[...], wu_ref[...], wo_ref[...]
    if mode == 'from_f32':
        wg, wu, wo = wg.astype(BF16), wu.astype(BF16), wo.astype(BF16)
        wgo_ref[...] = wg
        wuo_ref[...] = wu
        woo_ref[...] = wo
    tr = o_ref.shape[0] // n_row
    for r in range(n_row):
        rows = slice(r * tr, (r + 1) * tr)
        xb = xb_ref[rows, :]
        gate = jnp.dot(xb, wg, preferred_element_type=F32)
        up = jnp.dot(xb, wu, preferred_element_type=F32)
        h = (_silu(gate) * up).astype(BF16)
        o_ref[rows, :] += jnp.dot(h, wo, preferred_element_type=F32)

    @pl.when(f == nf - 1)
    def _():
        o_ref[...] = _layer_norm(0.5 * o_ref[...], g_ref[...], b_ref[...])


def _ffn(x, w, g, b, l, *, tm, cast_next=None, from_f32=False):
    m, d = x.shape
    tf = FFN_TF
    d_ff = w[1].shape[1] if from_f32 else w[2].shape[0]
    nf = d_ff // tf
    ni = m // tm
    vec = pl.BlockSpec((None, 1, d), lambda i, f: (l, 0, 0))
    assert nf >= 2 or ni == 1
    x_spec = pl.BlockSpec(memory_space=pl.ANY)
    tile_g = pl.BlockSpec((d, tf), lambda i, f: (0, f))
    tile_o = pl.BlockSpec((tf, d), lambda i, f: (f, 0))
    w_shapes = [jax.ShapeDtypeStruct((d, d_ff), BF16), jax.ShapeDtypeStruct((d, d_ff), BF16),
                jax.ShapeDtypeStruct((d_ff, d), BF16)]
    out_specs = [pl.BlockSpec((tm, d), lambda i, f: (i, 0))]
    out_shape = [jax.ShapeDtypeStruct((m, d), F32)]
    mode = None
    if from_f32:
        assert ni == 1 and cast_next is None
        mode = 'from_f32'
        w_in, w_out = w
        in_specs = [x_spec,
                    pl.BlockSpec((None, d, tf), lambda i, f: (l, 0, f)),
                    pl.BlockSpec((None, d, tf), lambda i, f: (l, 0, f + nf)),
                    pl.BlockSpec((None, tf, d), lambda i, f: (l, f, 0)), vec, vec]
        args = [x, w_in, w_in, w_out, g, b]
        out_specs += [tile_g, tile_g, tile_o]
        out_shape += w_shapes
    else:
        in_specs = [x_spec, tile_g, tile_g, tile_o, vec, vec]
        args = [x, w[0], w[1], w[2], g, b]
    if cast_next is not None:
        mode = 'cast_next'
        nw_in, nw_out, ln = cast_next
        ri, ro = d // ni, d_ff // (ni * nf)
        assert ri * ni == d and ro * ni * nf == d_ff
        in_specs += [pl.BlockSpec((None, ri, tf), lambda i, f: (ln, i, f)),
                     pl.BlockSpec((None, ri, tf), lambda i, f: (ln, i, f + nf)),
                     pl.BlockSpec((None, ro, d), lambda i, f: (ln, i * nf + f, 0))]
        args += [nw_in, nw_in, nw_out]
        out_specs += [pl.BlockSpec((ri, tf), lambda i, f: (i, f)), pl.BlockSpec((ri, tf), lambda i, f: (i, f)),
                      pl.BlockSpec((ro, d), lambda i, f: (i * nf + f, 0))]
        out_shape += w_shapes
    res = pl.pallas_call(
        functools.partial(_ffn_kernel, nf=nf, ni=ni, n_row=2 if tm >= 1024 else 1, mode=mode),
        grid=(ni, nf),
        in_specs=in_specs,
        out_specs=out_specs,
        out_shape=out_shape,
        scratch_shapes=[pltpu.VMEM((tm, d), BF16), pltpu.VMEM((tm, d), F32), pltpu.SemaphoreType.DMA(())],
        compiler_params=_params("arbitrary", "arbitrary"),
        name="ffn",
    )(*args)
    return res[0], (tuple(res[1:]) if mode is not None else None)


def _proj_kernel(x_ref, w_ref, wab_ref, o_ref, ab_ref, xb_ref):
    j = pl.program_id(1)

    @pl.when(j == 0)
    def _():
        xb = x_ref[...].astype(BF16)
        xb_ref[...] = xb
        ab_ref[...] = _dot_nt(xb, wab_ref[...])

    o_ref[...] = _dot_nt(xb_ref[...], w_ref[...])


def _proj(x, w_t, wab_t, l, *, tm):
    m, d = x.shape
    tn = PROJ_TN
    n = 12 * D_GROUP
    return pl.pallas_call(
        _proj_kernel,
        grid=(m // tm, n // tn),
        in_specs=[
            pl.BlockSpec((tm, d), lambda i, j: (i, 0)),
            pl.BlockSpec((None, tn, d), lambda i, j: (l, j, 0)),
            pl.BlockSpec((None, 128, d), lambda i, j: (l, 0, 0)),
        ],
        out_specs=[
            pl.BlockSpec((tm, tn), lambda i, j: (i, j)),
            pl.BlockSpec((tm, 128), lambda i, j: (i, 0)),
        ],
        out_shape=[jax.ShapeDtypeStruct((m, n), F32), jax.ShapeDtypeStruct((m, 128), F32)],
        scratch_shapes=[pltpu.VMEM((tm, d), BF16)],
        compiler_params=_params("parallel", "arbitrary"),
        name="proj",
    )(x, w_t, wab_t)


def _ret_tables(pos, c):
    half = D_HEAD // 2
    inv = ROPE_BASE ** (-np.arange(half, dtype=np.float64) / half)
    ang = pos.astype(np.float64)[:, None] * inv[None, :]
    cos = np.concatenate([np.cos(ang), np.cos(ang)], -1)
    sin = np.concatenate([-np.sin(ang), np.sin(ang)], -1)
    log_g = np.log1p(-np.exp2(-5.0 - np.arange(N_HEADS, dtype=np.float64)))
    idx = np.arange(c, dtype=np.float64)
    diff = idx[:, None] - idx[None, :]
    scale = D_HEAD ** -0.5
    dmask = np.where(diff >= 0, np.exp(log_g[:, None, None] * np.maximum(diff, 0.0)), 0.0) * scale
    kdec = np.exp(log_g[:, None] * (c - 1.0 - idx)[None, :]) * scale
    qdec = np.exp(log_g[:, None] * (idx + 1.0)[None, :])
    kdec = np.broadcast_to(kdec[:, :, None], (N_HEADS, c, D_HEAD))
    qdec = np.broadcast_to(qdec[:, :, None], (N_HEADS, c, D_HEAD))
    g_c = np.exp(log_g * c)
    as32 = lambda a: jnp.asarray(np.ascontiguousarray(a), F32)
    return as32(cos), as32(sin), as32(dmask), as32(qdec), as32(kdec), [float(v) for v in g_c]


def _ret_kernel(q_ref, k_ref, v_ref, g_ref, cos_ref, sin_ref, dm_ref, qd_ref, kd_ref, gn_ref, *rest,
                c, n, g_c, n_t, bb, has_state):
    if has_state:
        s0_ref, o_ref, sf_ref, s_scr, sst_scr = rest
    else:
        o_ref, sf_ref, s_scr, sst_scr = rest
    t = pl.program_id(1)
    heads = range(N_HEADS)
    lanes = [slice(h * D_HEAD, (h + 1) * D_HEAD) for h in heads]
    tt = n * c

    @pl.when(t == 0)
    def _():
        s_scr[...] = s0_ref[...] if has_state else jnp.zeros_like(s_scr)

    cos = jnp.concatenate([cos_ref[...]] * bb, axis=0)
    sin = jnp.concatenate([sin_ref[...]] * bb, axis=0)

    def rows(ref, h):
        return ref[:, :, lanes[h]].reshape(bb * tt, D_HEAD)

    def rot(x):
        return (x * cos + pltpu.roll(x, D_HEAD // 2, 1) * sin).reshape(bb * n, c, D_HEAD)

    q3 = [rot(rows(q_ref, h)) for h in heads]
    k3 = [rot(rows(k_ref, h)) for h in heads]
    v3 = [rows(v_ref, h).reshape(bb * n, c, D_HEAD) for h in heads]
    scores = [_bmm_nt(q3[h], k3[h]) * dm_ref[h] for h in heads]
    o_intra = [_bmm(scores[h], v3[h]) for h in heads]
    kdt = [jnp.swapaxes(k3[h] * kd_ref[h], 1, 2) for h in heads]
    upd = [_bmm(kdt[h], v3[h]) for h in heads]
    for b in range(bb):
        for h in heads:
            s = s_scr[b, h]
            for i in range(n):
                sst_scr[h, b * n + i] = s.astype(BF16)
                s = g_c[h] * s + upd[h][b * n + i]
            s_scr[b, h] = s
    for h in heads:
        o = o_intra[h] + _bmm(q3[h] * qd_ref[h], sst_scr[h])
        mu = jnp.mean(o, -1, keepdims=True)
        oc = o - mu
        var = jnp.mean(oc * oc, -1, keepdims=True)
        on = (oc * lax.rsqrt(var + EPS)).reshape(bb * tt, D_HEAD) * gn_ref[:, lanes[h]]
        o_ref[:, :, lanes[h]] = (on * _silu(rows(g_ref, h))).reshape(bb, tt, D_HEAD).astype(o_ref.dtype)

    @pl.when(t == n_t - 1)
    def _():
        sf_ref[...] = s_scr[...]


def _retention(proj3, state0, gn, pos, l, *, tt, bb):
    bsz, seq, _ = proj3.shape
    c = min(CHUNK, seq)
    cos, sin, dmask, qdec, kdec, g_c = _ret_tables(pos, c)
    n_t = seq // tt
    n = tt // c
    col = lambda j: pl.BlockSpec((bb, tt, D_GROUP), lambda b, t: (b, t, j))
    full3 = pl.BlockSpec((N_HEADS, c, c), lambda b, t: (0, 0, 0))
    full3d = pl.BlockSpec((N_HEADS, c, D_HEAD), lambda b, t: (0, 0, 0))
    st = pl.BlockSpec((bb, N_HEADS, D_HEAD, D_HEAD), lambda b, t: (b, 0, 0, 0))
    has_state = state0 is not None
    state_in = [state0] if has_state else []
    state_spec = ([pl.BlockSpec((None, bb, N_HEADS, D_HEAD, D_HEAD), lambda b, t: (l, b, 0, 0, 0))]
                  if has_state else [])
    return pl.pallas_call(
        functools.partial(_ret_kernel, c=c, n=n, g_c=g_c, n_t=n_t, bb=bb, has_state=has_state),
        grid=(bsz // bb, n_t),
        in_specs=[col(0), col(1), col(2), col(3),
                  pl.BlockSpec((tt, D_HEAD), lambda b, t: (t, 0)),
                  pl.BlockSpec((tt, D_HEAD), lambda b, t: (t, 0)),
                  full3, full3d, full3d,
                  pl.BlockSpec((None, 1, D_GROUP), lambda b, t: (l, 0, 0))] + state_spec,
        out_specs=[pl.BlockSpec((bb, tt, D_GROUP), lambda b, t: (b, t, 0)), st],
        out_shape=[jax.ShapeDtypeStruct((bsz, seq, D_GROUP), BF16),
                   jax.ShapeDtypeStruct((bsz, N_HEADS, D_HEAD, D_HEAD), F32)],
        scratch_shapes=[pltpu.VMEM((bb, N_HEADS, D_HEAD, D_HEAD), F32),
                        pltpu.VMEM((N_HEADS, bb * n, D_HEAD, D_HEAD), BF16)],
        compiler_params=_params("parallel", "arbitrary"),
        name="ret",
    )(proj3, proj3, proj3, proj3, cos, sin, dmask, qdec, kdec, gn, *state_in)


PAD = 8


def _conv_block(ext_ref, x, w_ref, tt):
    ext_ref[PAD:PAD + tt, :] = x
    out = x * w_ref[CONV_W - 1:CONV_W, :]
    for j in range(1, CONV_W):
        out = out + ext_ref[PAD - j:PAD - j + tt, :] * w_ref[CONV_W - 1 - j:CONV_W - j, :]
    ext_ref[PAD - (CONV_W - 1):PAD, :] = x[tt - (CONV_W - 1):, :]
    return out


def _lru_kernel(y_ref, x_ref, cs_ref, h0_ref, cw_ref, cb_ref, wa_ref, ba_ref, wx_ref, bx_ref, lam_ref,
                o_ref, hn_ref, cn_ref, ext_ref, a_scr, b_scr, h_scr, hc_ref, *, tt, n_t):
    t = pl.program_id(1)

    @pl.when(t == 0)
    def _():
        ext_ref[PAD - (CONV_W - 1):PAD, :] = cs_ref[0]
        hc_ref[...] = h0_ref[0]

    x = x_ref[0]
    xc = _conv_block(ext_ref, x, cw_ref, tt) + cb_ref[...]
    r = jax.nn.sigmoid(_mm(xc, wa_ref[...]) + ba_ref[...])
    i = jax.nn.sigmoid(_mm(xc, wx_ref[...]) + bx_ref[...])
    log_a = -LRU_C * r * jax.nn.softplus(-lam_ref[...])
    a_scr[...] = jnp.exp(log_a)
    th = jnp.tanh(log_a)
    one_minus_a2 = -2.0 * th / (1.0 - th)
    b_scr[...] = jnp.sqrt(one_minus_a2) * (i * xc)

    def step(s, h):
        row = pl.ds(s, 1)
        h = a_scr[row, :] * h + b_scr[row, :]
        h_scr[row, :] = h
        return h

    hc_ref[...] = lax.fori_loop(0, tt, step, hc_ref[...], unroll=8)
    o_ref[0] = (h_scr[...] * _gelu(y_ref[0])).astype(o_ref.dtype)

    @pl.when(t == n_t - 1)
    def _():
        hn_ref[0] = hc_ref[...]
        cn_ref[0] = x[tt - (CONV_W - 1):, :]


def _rglru(proj3, conv_state, h0, cw, cb, wa, ba, wx, bx, lam, l, *, tt):
    bsz, seq, _ = proj3.shape
    n_t = seq // tt
    col = lambda j: pl.BlockSpec((1, tt, D_GROUP), lambda b, t: (b, t, j))
    vec = pl.BlockSpec((None, 1, D_GROUP), lambda b, t: (l, 0, 0))
    sq = pl.BlockSpec((None, D_GROUP, D_GROUP), lambda b, t: (l, 0, 0))
    cs = pl.BlockSpec((1, CONV_W - 1, D_GROUP), lambda b, t: (b, 0, 0))
    hs = pl.BlockSpec((1, 1, D_GROUP), lambda b, t: (b, 0, 0))
    return pl.pallas_call(
        functools.partial(_lru_kernel, tt=tt, n_t=n_t),
        grid=(bsz, n_t),
        in_specs=[col(4), col(5), cs, hs,
                  pl.BlockSpec((None, CONV_W, D_GROUP), lambda b, t: (l, 0, 0)), vec, sq, vec, sq, vec, vec],
        out_specs=[pl.BlockSpec((1, tt, D_GROUP), lambda b, t: (b, t, 0)), hs, cs],
        out_shape=[jax.ShapeDtypeStruct((bsz, seq, D_GROUP), BF16),
                   jax.ShapeDtypeStruct((bsz, 1, D_GROUP), F32),
                   jax.ShapeDtypeStruct((bsz, CONV_W - 1, D_GROUP), F32)],
        scratch_shapes=[pltpu.VMEM((tt + PAD, D_GROUP), F32), pltpu.VMEM((tt, D_GROUP), F32),
                        pltpu.VMEM((tt, D_GROUP), F32), pltpu.VMEM((tt, D_GROUP), F32),
                        pltpu.VMEM((1, D_GROUP), F32)],
        compiler_params=_params("parallel", "arbitrary"),
        name="lru",
    )(proj3, proj3, conv_state, h0, cw, cb, wa, ba, wx, bx, lam)


def _sg_kernel(u_ref, v_ref, lg_ref, lb_ref, ws_ref, bs_ref, o_ref, *vn_ref, c, tt, bb):
    u = _gelu(u_ref[...])
    vn = _layer_norm(_gelu(v_ref[...]), lg_ref[...], lb_ref[...])
    if vn_ref:
        vn_ref[0][...] = vn
    row = lax.broadcasted_iota(jnp.int32, (c, c), 0)
    col = lax.broadcasted_iota(jnp.int32, (c, c), 1)
    mask = (col // CHUNK) <= (row // CHUNK)
    for h in range(N_HEADS):
        lanes = slice(h * D_HEAD, (h + 1) * D_HEAD)
        w = jnp.where(mask, ws_ref[h], 0.0).astype(BF16)
        bias = bs_ref[:, h:h + 1]
        for b in range(bb):
            for i in range(tt // c):
                rows = slice(i * c, (i + 1) * c)
                s = _mm(w, vn[b, rows, lanes]) + bias
                o_ref[b, rows, lanes] = (u[b, rows, lanes] * s).astype(o_ref.dtype)


def _spatial_gate(proj3, lg, lb, ws, bs_t, l, *, tt, bb, with_vn):
    bsz, seq, _ = proj3.shape
    c = min(SG_CHUNK, seq)
    col = lambda j: pl.BlockSpec((bb, tt, D_GROUP), lambda b, t: (b, t, j))
    vec = pl.BlockSpec((None, 1, D_GROUP), lambda b, t: (l, 0, 0))
    out = pl.BlockSpec((bb, tt, D_GROUP), lambda b, t: (b, t, 0))
    res = pl.pallas_call(
        functools.partial(_sg_kernel, c=c, tt=tt, bb=bb),
        grid=(bsz // bb, seq // tt),
        in_specs=[col(6), col(7), vec, vec,
                  pl.BlockSpec((None, N_HEADS, c, c), lambda b, t: (l, 0, 0, 0)),
                  pl.BlockSpec((None, c, N_HEADS), lambda b, t: (l, 0, 0))],
        out_specs=[out, out] if with_vn else [out],
        out_shape=[jax.ShapeDtypeStruct((bsz, seq, D_GROUP), BF16)]
                  + ([jax.ShapeDtypeStruct((bsz, seq, D_GROUP), F32)] if with_vn else []),
        compiler_params=_params("parallel", "parallel"),
        name="sg",
    )(proj3, proj3, lg, lb, ws, bs_t)
    return (res[0], res[1]) if with_vn else (res[0], None)


INV_BLOCK = 16


def _unit_lower_inverse(mats, c):
    row = lax.broadcasted_iota(jnp.int32, (c, c), 0)
    col = lax.broadcasted_iota(jnp.int32, (c, c), 1)
    eye = (row == col).astype(F32)
    same = row // INV_BLOCK == col // INV_BLOCK
    ps = [jnp.where(same, a, 0.0) for a in mats]
    xs = [eye - p for p in ps]
    ps = [_bmm(p, p) for p in ps]
    n_sq = int(math.log2(INV_BLOCK)) - 1
    for step in range(n_sq):
        last = step == n_sq - 1
        lhs = xs if last else [jnp.concatenate([x, p], axis=1) for x, p in zip(xs, ps)]
        prod = [_bmm(l, p) for l, p in zip(lhs, ps)]
        xs = [x + r[:, :c, :] for x, r in zip(xs, prod)]
        if not last:
            ps = [r[:, c:, :] for r in prod]
    size = INV_BLOCK
    while size < c:
        big = 2 * size
        sel = (row // big == col // big) & (row // size != col // size)
        offs = [jnp.where(sel, a, 0.0) for a in mats]
        tmp = [_bmm(o, x) for o, x in zip(offs, xs)]
        xs = [x - _bmm(x, t) for x, t in zip(xs, tmp)]
        size = big
    return xs


def _dn_kernel(q_ref, k_ref, v_ref, z_ref, ab_ref, cw_ref, al_ref, dt_ref, gn_ref, *rest,
               c, n, n_t, bb, has_state):
    if has_state:
        cs_ref, s0_ref = rest[:2]
        rest = rest[2:]
    o_ref, sf_ref, cn_ref, ext_ref, s_scr, u_scr, wq_scr, attn_scr, kdt_scr, egl_scr = rest
    t = pl.program_id(1)
    tt = n * c
    nb = bb * n
    heads = range(N_HEADS)

    @pl.when(t == 0)
    def _():
        for j in range(3):
            hist = (cs_ref[:, :, j * D_GROUP:(j + 1) * D_GROUP] if has_state
                    else jnp.zeros((bb, CONV_W - 1, D_GROUP), F32))
            ext_ref[j, :, PAD - (CONV_W - 1):PAD, :] = hist
        s_scr[...] = s0_ref[...] if has_state else jnp.zeros_like(s_scr)

    raw = (q_ref, k_ref, v_ref)
    qf, kf, vf = [
        jnp.concatenate([_silu(_conv_block(ext_ref.at[j, b], raw[j][b],
                                           cw_ref.at[:, j * D_GROUP:(j + 1) * D_GROUP], tt))
                         for b in range(bb)], axis=0)
        for j in range(3)]

    row = lax.broadcasted_iota(jnp.int32, (c, c), 0)
    col = lax.broadcasted_iota(jnp.int32, (c, c), 1)
    tri = row >= col
    strict = row > col
    tri_f = tri.astype(F32)
    scale = D_HEAD ** -0.5

    ab = ab_ref[...].reshape(bb * tt, 128)
    g_all = -jnp.exp(al_ref[...]) * jax.nn.softplus(ab + dt_ref[...])
    beta_all = jax.nn.sigmoid(ab)
    gc_all = jnp.concatenate(
        [jnp.dot(tri_f, g_all[i * c:(i + 1) * c], preferred_element_type=F32, precision=lax.Precision.HIGHEST)
         for i in range(nb)], axis=0)
    gc_t = gc_all.T

    def l2n(x):
        return x * lax.rsqrt(jnp.sum(x * x, -1, keepdims=True) + 1e-6)

    def split(x, h):
        return x[:, h * D_HEAD:(h + 1) * D_HEAD].reshape(nb, c, D_HEAD)

    q3 = [l2n(split(qf, h)) * scale for h in heads]
    k3 = [l2n(split(kf, h)) for h in heads]
    v3 = [split(vf, h) for h in heads]
    beta = [beta_all[:, N_HEADS + h:N_HEADS + h + 1].reshape(nb, c, 1) for h in heads]
    gcol = [gc_all[:, h:h + 1].reshape(nb, c, 1) for h in heads]
    grow = [jnp.stack([gc_t[h:h + 1, i * c:(i + 1) * c] for i in range(nb)]) for h in heads]
    glast = [g[:, c - 1:c, :] for g in gcol]
    decay = [jnp.where(tri, jnp.exp(gc - gr), 0.0) for gc, gr in zip(gcol, grow)]
    egc = [jnp.exp(g) for g in gcol]
    kb = [k * b for k, b in zip(k3, beta)]
    vb = [v * b for v, b in zip(v3, beta)]
    a_mat = [jnp.where(strict, _bmm_nt(x, k) * d, 0.0) for x, k, d in zip(kb, k3, decay)]
    tinv = _unit_lower_inverse(a_mat, c)
    uw = [_bmm(ti, jnp.concatenate([x, y * e], axis=-1)) for ti, x, y, e in zip(tinv, vb, kb, egc)]
    attn = [_bmm_nt(q, k) * d for q, k, d in zip(q3, k3, decay)]
    for h in heads:
        u_scr[h] = uw[h][:, :, :D_HEAD]
        wq_scr[h] = jnp.concatenate([uw[h][:, :, D_HEAD:], q3[h] * egc[h]], axis=1).astype(BF16)
        attn_scr[h] = attn[h].astype(BF16)
        kdt_scr[h] = jnp.swapaxes(k3[h] * jnp.exp(glast[h] - gcol[h]), 1, 2).astype(BF16)
        egl_scr[h] = jnp.broadcast_to(jnp.exp(glast[h]), (nb, 1, D_HEAD))

    pairs = [(b, h) for b in range(bb) for h in heads]

    def chunk(i, carry):
        rows = pl.ds(pl.multiple_of(i * c, c), c)
        s = [s_scr[b, h] for b, h in pairs]
        r = [_mm(wq_scr[h, b * n + i], s[p]) for p, (b, h) in enumerate(pairs)]
        v_new = [u_scr[h, b * n + i] - r[p][:c] for p, (b, h) in enumerate(pairs)]
        for p, (b, h) in enumerate(pairs):
            s_scr[b, h] = s[p] * egl_scr[h, b * n + i] + _mm(kdt_scr[h, b * n + i], v_new[p])
        o = [r[p][c:] + _mm(attn_scr[h, b * n + i], v_new[p]) for p, (b, h) in enumerate(pairs)]
        for p, (b, h) in enumerate(pairs):
            lanes = slice(h * D_HEAD, (h + 1) * D_HEAD)
            on = o[p] * lax.rsqrt(jnp.mean(o[p] * o[p], -1, keepdims=True) + EPS) * gn_ref[...]
            o_ref[b, rows, lanes] = (on * _silu(z_ref[b, rows, lanes])).astype(o_ref.dtype)
        return carry

    lax.fori_loop(0, n, chunk, 0)

    @pl.when(t == n_t - 1)
    def _():
        sf_ref[...] = s_scr[...]
        for j in range(3):
            cn_ref[:, :, j * D_GROUP:(j + 1) * D_GROUP] = raw[j][:, tt - (CONV_W - 1):, :]


def _deltanet(proj3, ab3, states, cw, a_log, dt_bias, gn, l, *, tt, bb):
    bsz, seq, _ = proj3.shape
    c = min(CHUNK, seq)
    n_t = seq // tt
    n = tt // c
    nb = bb * n
    col = lambda j: pl.BlockSpec((bb, tt, D_GROUP), lambda b, t: (b, t, j))
    st = pl.BlockSpec((bb, N_HEADS, D_HEAD, D_HEAD), lambda b, t: (b, 0, 0, 0))
    cs = pl.BlockSpec((bb, CONV_W - 1, 3 * D_GROUP), lambda b, t: (b, 0, 0))
    lane_row = pl.BlockSpec((None, 1, 128), lambda b, t: (l, 0, 0))
    has_state = states is not None
    state_in = list(states) if has_state else []
    state_spec = ([pl.BlockSpec((None, bb, CONV_W - 1, 3 * D_GROUP), lambda b, t: (l, b, 0, 0)),
                   pl.BlockSpec((None, bb, N_HEADS, D_HEAD, D_HEAD), lambda b, t: (l, b, 0, 0, 0))]
                  if has_state else [])
    return pl.pallas_call(
        functools.partial(_dn_kernel, c=c, n=n, n_t=n_t, bb=bb, has_state=has_state),
        grid=(bsz // bb, n_t),
        in_specs=[col(8), col(9), col(10), col(11),
                  pl.BlockSpec((bb, tt, 128), lambda b, t: (b, t, 0)),
                  pl.BlockSpec((None, CONV_W, 3 * D_GROUP), lambda b, t: (l, 0, 0)),
                  lane_row, lane_row, lane_row] + state_spec,
        out_specs=[pl.BlockSpec((bb, tt, D_GROUP), lambda b, t: (b, t, 0)), st, cs],
        out_shape=[jax.ShapeDtypeStruct((bsz, seq, D_GROUP), BF16),
                   jax.ShapeDtypeStruct((bsz, N_HEADS, D_HEAD, D_HEAD), F32),
                   jax.ShapeDtypeStruct((bsz, CONV_W - 1, 3 * D_GROUP), F32)],
        scratch_shapes=[pltpu.VMEM((3, bb, tt + PAD, D_GROUP), F32),
                        pltpu.VMEM((bb, N_HEADS, D_HEAD, D_HEAD), F32),
                        pltpu.VMEM((N_HEADS, nb, c, D_HEAD), F32),
                        pltpu.VMEM((N_HEADS, nb, 2 * c, D_HEAD), BF16),
                        pltpu.VMEM((N_HEADS, nb, c, c), BF16),
                        pltpu.VMEM((N_HEADS, nb, D_HEAD, c), BF16),
                        pltpu.VMEM((N_HEADS, nb, 1, D_HEAD), F32)],
        compiler_params=_params("parallel", "arbitrary"),
        name="dn",
    )(proj3, proj3, proj3, proj3, ab3, cw, a_log, dt_bias, gn, *state_in)


def _mixout_kernel(a_ref, b_ref, c_ref, d_ref, x_ref, w_ref, g_ref, bb_ref, o_ref, cat_ref):
    for j, part in enumerate((a_ref, b_ref, c_ref, d_ref)):
        cat_ref[:, j * D_GROUP:(j + 1) * D_GROUP] = part[...]
    acc = jnp.dot(cat_ref[...], w_ref[...], preferred_element_type=F32)
    o_ref[...] = _layer_norm(ALPHA * x_ref[...] + acc, g_ref[...], bb_ref[...])


def _mixout(a, b, c, d, x, w, g, bb, l, *, tm):
    m, dm = x.shape
    part = pl.BlockSpec((tm, D_GROUP), lambda i: (i, 0))
    vec = pl.BlockSpec((None, 1, dm), lambda i: (l, 0, 0))
    return pl.pallas_call(
        _mixout_kernel,
        grid=(m // tm,),
        in_specs=[part, part, part, part,
                  pl.BlockSpec((tm, dm), lambda i: (i, 0)),
                  pl.BlockSpec((None, 4 * D_GROUP, dm), lambda i: (l, 0, 0), pipeline_mode=pl.Buffered(1)),
                  vec, vec],
        out_specs=pl.BlockSpec((tm, dm), lambda i: (i, 0)),
        out_shape=jax.ShapeDtypeStruct((m, dm), F32),
        scratch_shapes=[pltpu.VMEM((tm, 4 * D_GROUP), BF16)],
        compiler_params=_params("parallel"),
        name="mixout",
    )(a, b, c, d, x, w, g, bb)


def _block_diag(w):
    dep, h, n, _ = w.shape
    eye = jnp.eye(h, dtype=w.dtype)
    return (eye[None, :, None, :, None] * w[:, :, :, None, :]).reshape(dep, h * n, h * n)


def _prep(p):
    row = lambda v: v.reshape(DEPTH, 1, -1)
    pad128 = lambda v: jnp.pad(row(v), ((0, 0), (0, 0), (0, 128 - v.shape[-1])))
    q = dict(p)
    q['w_mix_out'] = p['w_mix_out'].astype(BF16)
    w_in_t = jnp.swapaxes(p['w_mix_in'], 1, 2)
    q['w_mix_in'] = w_in_t.astype(BF16)
    q['w_mix_ab'] = jnp.pad(w_in_t[:, 12 * D_GROUP:, :], ((0, 0), (0, 128 - 2 * N_HEADS), (0, 0))).astype(BF16)
    for name in ('ln1_g', 'ln1_b', 'ln2_g', 'ln2_b', 'ln3_g', 'ln3_b', 'ret_norm_g', 'lru_conv_b',
                 'lru_b_a', 'lru_b_x', 'lru_lam', 'sg_ln_g', 'sg_ln_b', 'dn_norm_g'):
        q[name] = row(p[name])
    q['lru_w_a'] = _block_diag(p['lru_w_a']).astype(BF16)
    q['lru_w_x'] = _block_diag(p['lru_w_x']).astype(BF16)
    q['dn_a_log'] = pad128(p['dn_a_log'])
    q['dn_dt_bias'] = pad128(p['dn_dt_bias'])
    return q


def _token_mixers(x, st, p, l, s):
    bsz, seq, tm, tt, bb = s['bsz'], s['seq'], s['tm'], s['tt'], s['bb']
    c_sg = min(SG_CHUNK, seq)
    sg_w = p['sg_w'][:, :, :c_sg, :c_sg]
    sg_b_t = jnp.swapaxes(p['sg_b'][:, :, :c_sg], 1, 2)
    proj, ab = _proj(x, p['w_mix_in'], p['w_mix_ab'], l, tm=tm)
    proj3 = proj.reshape(bsz, seq, -1)
    ab3 = ab.reshape(bsz, seq, 128)
    if st is None:
        s_ret = dn_states = None
        s_lru, s_lru_conv = jnp.zeros((bsz, D_GROUP), F32), jnp.zeros((bsz, CONV_W - 1, D_GROUP), F32)
    else:
        s_ret, s_lru, s_lru_conv, s_dn, s_dn_conv = st
        s_lru, s_lru_conv = s_lru[l], s_lru_conv[l]
        dn_states = (s_dn_conv, s_dn)
    out_a, ret_new = _retention(proj3, s_ret, p['ret_norm_g'], s['pos'], l, tt=tt, bb=bb)
    out_b, lru_h_new, lru_conv_new = _rglru(
        proj3, s_lru_conv, s_lru.reshape(bsz, 1, D_GROUP), p['lru_conv_w'], p['lru_conv_b'],
        p['lru_w_a'], p['lru_b_a'], p['lru_w_x'], p['lru_b_x'], p['lru_lam'], l, tt=tt)
    out_c, sg_v = _spatial_gate(proj3, p['sg_ln_g'], p['sg_ln_b'], sg_w, sg_b_t, l,
                                tt=min(seq, 4 * SG_CHUNK), bb=bb, with_vn=s['keep_sg_v'])
    out_d, dn_new, dn_conv_new = _deltanet(proj3, ab3, dn_states, p['dn_conv_w'],
                                           p['dn_a_log'], p['dn_dt_bias'], p['dn_norm_g'], l, tt=tt, bb=bb)
    flat = lambda o: o.reshape(bsz * seq, D_GROUP)
    x = _mixout(flat(out_a), flat(out_b), flat(out_c), flat(out_d), x, p['w_mix_out'],
                p['ln2_g'], p['ln2_b'], l, tm=tm)
    return x, (ret_new, lru_h_new.reshape(bsz, D_GROUP), lru_conv_new, dn_new, dn_conv_new, sg_v)


def kernel(x_prompt, x_sample, state_ret, state_lru_h, state_lru_conv, state_dn, state_dn_conv, ffn1_w_in, ffn1_w_out, ln1_g, ln1_b, w_mix_in, ret_norm_g, lru_conv_w, lru_conv_b, lru_w_a, lru_b_a, lru_w_x, lru_b_x, lru_lam, sg_ln_g, sg_ln_b, sg_w, sg_b, dn_conv_w, dn_a_log, dn_dt_bias, dn_norm_g, w_mix_out, ln2_g, ln2_b, ffn2_w_in, ffn2_w_out, ln3_g, ln3_b):
    weights = dict(ffn1_w_in=ffn1_w_in, ffn1_w_out=ffn1_w_out, ln1_g=ln1_g, ln1_b=ln1_b, w_mix_in=w_mix_in,
                   ret_norm_g=ret_norm_g, lru_conv_w=lru_conv_w, lru_conv_b=lru_conv_b, lru_w_a=lru_w_a,
                   lru_b_a=lru_b_a, lru_w_x=lru_w_x, lru_b_x=lru_b_x, lru_lam=lru_lam, sg_ln_g=sg_ln_g,
                   sg_ln_b=sg_ln_b, sg_w=sg_w, sg_b=sg_b, dn_conv_w=dn_conv_w, dn_a_log=dn_a_log,
                   dn_dt_bias=dn_dt_bias, dn_norm_g=dn_norm_g, w_mix_out=w_mix_out, ln2_g=ln2_g, ln2_b=ln2_b,
                   ffn2_w_in=ffn2_w_in, ffn2_w_out=ffn2_w_out, ln3_g=ln3_g, ln3_b=ln3_b)
    p = _prep(weights)

    bp, lp, dm = x_prompt.shape
    bs, ls, _ = x_sample.shape
    prompt = dict(bsz=bp, seq=lp, tm=1024, tt=min(lp, 512), bb=1, pos=np.arange(lp), keep_sg_v=False)
    sample = dict(bsz=bs, seq=ls, tm=bs * ls, tt=ls, bb=min(bs, 8), pos=PAST_LEN + np.arange(ls), keep_sg_v=True)
    sample_states = (state_ret, state_lru_h, state_lru_conv, state_dn, state_dn_conv)
    xp = x_prompt.reshape(bp * lp, dm)
    xs = x_sample.reshape(bs * ls, dm)

    ffns = [(l, name) for l in range(DEPTH) for name in ('ffn1', 'ffn2')]
    ln_of = {'ffn1': ('ln1_g', 'ln1_b'), 'ffn2': ('ln3_g', 'ln3_b')}

    def ffn_pair(k, xp, xs, wb):
        l, name = ffns[k]
        g, b = p[ln_of[name][0]], p[ln_of[name][1]]
        nxt = None
        if k + 1 < len(ffns):
            ln, nname = ffns[k + 1]
            nxt = (weights[nname + '_w_in'], weights[nname + '_w_out'], ln)
        if wb is None:
            xs, wb = _ffn(xs, (weights[name + '_w_in'], weights[name + '_w_out']), g, b, l, tm=sample['tm'],
                          from_f32=True)
        else:
            xs, _ = _ffn(xs, wb, g, b, l, tm=sample['tm'])
        xp, wb_next = _ffn(xp, wb, g, b, l, tm=prompt['tm'], cast_next=nxt)
        return xp, xs, wb_next

    wb = None

    st_p, st_s = [], []
    for l in range(DEPTH):
        xp, xs, wb = ffn_pair(2 * l, xp, xs, wb)
        xp, new_p = _token_mixers(xp, None, p, l, prompt)
        xs, new_s = _token_mixers(xs, sample_states, p, l, sample)
        xp, xs, wb = ffn_pair(2 * l + 1, xp, xs, wb)
        st_p.append(new_p)
        st_s.append(new_s)

    stack = lambda sts, i: jnp.stack([s[i] for s in sts])
    return (xp.reshape(bp, lp, dm), xs.reshape(bs, ls, dm),
            stack(st_p, 0), stack(st_p, 1), stack(st_p, 2), stack(st_p, 3), stack(st_p, 4),
            stack(st_s, 0), stack(st_s, 1), stack(st_s, 2), stack(st_s, 3), stack(st_s, 4), stack(st_s, 5))
```

```python
import functools
import math

import numpy as np
import jax
import jax.numpy as jnp
from jax import lax
from jax.experimental import pallas as pl
from jax.experimental.pallas import tpu as pltpu

F32 = jnp.float32
BF16 = jnp.bfloat16

DEPTH = 2
PAST_LEN = 4096
CHUNK = 64
D_GROUP = 512
N_HEADS = 4
D_HEAD = D_GROUP // N_HEADS
H_LRU = 8
LRU_C = 8.0
CONV_W = 4
SG_CHUNK = 128
ROPE_BASE = 10000.0
ALPHA = (2.0 * DEPTH) ** 0.25
EPS = 1e-5

FFN_TF = 512
FFN_TF_ONE_ROW_TILE = 1408
PROJ_TN = 2048
VMEM_LIMIT = 58 * 1024 * 1024


def _params(*sem):
    return pltpu.CompilerParams(dimension_semantics=sem, vmem_limit_bytes=VMEM_LIMIT)


def _mm(a, b):
    return jnp.dot(a.astype(BF16), b.astype(BF16), preferred_element_type=F32)


def _dot_nt(a, b):
    return lax.dot_general(a, b, (((1,), (1,)), ((), ())), preferred_element_type=F32)


def _bmm(a, b):
    return jnp.einsum('nij,njk->nik', a.astype(BF16), b.astype(BF16), preferred_element_type=F32)


def _bmm_nt(a, b):
    return jnp.einsum('nid,njd->nij', a.astype(BF16), b.astype(BF16), preferred_element_type=F32)


def _layer_norm(y, g, b):
    mu = jnp.mean(y, -1, keepdims=True)
    yc = y - mu
    var = jnp.mean(yc * yc, -1, keepdims=True)
    return yc * lax.rsqrt(var + EPS) * g + b


def _silu(x):
    return x * jax.nn.sigmoid(x)


def _gelu(x):
    return jax.nn.gelu(x, approximate=True)


def _ffn_kernel(x_hbm, wg_ref, wu_ref, wo_ref, g_ref, b_ref, *rest, nf, ni, n_row, mode):
    *rest, xb_ref, xf_ref, x_sem = rest
    if mode == 'cast_next':
        ng_ref, nu_ref, no_ref, o_ref, ngo_ref, nuo_ref, noo_ref = rest
        ngo_ref[...] = ng_ref[...].astype(BF16)
        nuo_ref[...] = nu_ref[...].astype(BF16)
        noo_ref[...] = no_ref[...].astype(BF16)
    elif mode == 'from_f32':
        o_ref, wgo_ref, wuo_ref, woo_ref = rest
    else:
        (o_ref,) = rest
    i = pl.program_id(0)
    f = pl.program_id(1)
    tm = xf_ref.shape[0]

    def x_copy(k):
        return pltpu.make_async_copy(x_hbm.at[pl.ds(k * tm, tm), :], xf_ref, x_sem)

    @pl.when(f == 0)
    def _():
        @pl.when(i == 0)
        def _():
            x_copy(0).start()

        x_copy(i).wait()
        x = xf_ref[...]
        xb_ref[...] = x.astype(BF16)
        o_ref[...] = (2.0 * ALPHA) * x

    @pl.when((f == 1) & (i + 1 < ni))
    def _():
        x_copy(i + 1).start()

    wg, wu, wo = wg_ref[...], wu_ref[...], wo_ref[...]
    if mode == 'from_f32':
        wg, wu, wo = wg.astype(BF16), wu.astype(BF16), wo.astype(BF16)
        wgo_ref[...] = wg
        wuo_ref[...] = wu
        woo_ref[...] = wo
    tr = o_ref.shape[0] // n_row

    for r in range(n_row):
        rows = slice(r * tr, (r + 1) * tr)
        xb = xb_ref[rows, :]
        gate = jnp.dot(xb, wg, preferred_element_type=F32)
        up = jnp.dot(xb, wu, preferred_element_type=F32)
        h = (_silu(gate) * up).astype(BF16)
        o_ref[rows, :] += jnp.dot(h, wo, preferred_element_type=F32)

    @pl.when(f == nf - 1)
    def _():
        o_ref[...] = _layer_norm(0.5 * o_ref[...], g_ref[...], b_ref[...])


def _ffn(x, w, g, b, l, *, tm, tf=FFN_TF, cast_next=None, from_f32=False):
    m, d = x.shape
    d_ff = w[1].shape[1] if from_f32 else w[2].shape[0]
    nf = d_ff // tf
    assert nf * tf == d_ff
    ni = m // tm
    vec = pl.BlockSpec((None, 1, d), lambda i, f: (l, 0, 0))
    assert nf >= 2 or ni == 1
    x_spec = pl.BlockSpec(memory_space=pl.ANY)
    tile_g = pl.BlockSpec((d, tf), lambda i, f: (0, f))
    tile_o = pl.BlockSpec((tf, d), lambda i, f: (f, 0))
    w_shapes = [jax.ShapeDtypeStruct((d, d_ff), BF16), jax.ShapeDtypeStruct((d, d_ff), BF16),
                jax.ShapeDtypeStruct((d_ff, d), BF16)]
    out_specs = [pl.BlockSpec((tm, d), lambda i, f: (i, 0))]
    out_shape = [jax.ShapeDtypeStruct((m, d), F32)]
    mode = None
    if from_f32:
        assert ni == 1 and cast_next is None
        mode = 'from_f32'
        w_in, w_out = w
        in_specs = [x_spec,
                    pl.BlockSpec((None, d, tf), lambda i, f: (l, 0, f)),
                    pl.BlockSpec((None, d, tf), lambda i, f: (l, 0, f + nf)),
                    pl.BlockSpec((None, tf, d), lambda i, f: (l, f, 0)), vec, vec]
        args = [x, w_in, w_in, w_out, g, b]
        out_specs += [tile_g, tile_g, tile_o]
        out_shape += w_shapes
    else:
        in_specs = [x_spec, tile_g, tile_g, tile_o, vec, vec]
        args = [x, w[0], w[1], w[2], g, b]
    if cast_next is not None:
        mode = 'cast_next'
        nw_in, nw_out, ln = cast_next
        ri, ro = d // ni, d_ff // (ni * nf)
        assert ri * ni == d and ro * ni * nf == d_ff
        in_specs += [pl.BlockSpec((None, ri, tf), lambda i, f: (ln, i, f)),
                     pl.BlockSpec((None, ri, tf), lambda i, f: (ln, i, f + nf)),
                     pl.BlockSpec((None, ro, d), lambda i, f: (ln, i * nf + f, 0))]
        args += [nw_in, nw_in, nw_out]
        out_specs += [pl.BlockSpec((ri, tf), lambda i, f: (i, f)), pl.BlockSpec((ri, tf), lambda i, f: (i, f)),
                      pl.BlockSpec((ro, d), lambda i, f: (i * nf + f, 0))]
        out_shape += w_shapes
    res = pl.pallas_call(
        functools.partial(_ffn_kernel, nf=nf, ni=ni, n_row=2, mode=mode),
        grid=(ni, nf),
        in_specs=in_specs,
        out_specs=out_specs,
        out_shape=out_shape,
        scratch_shapes=[pltpu.VMEM((tm, d), BF16), pltpu.VMEM((tm, d), F32), pltpu.SemaphoreType.DMA(())],
        compiler_params=_params("arbitrary", "arbitrary"),
        name="ffn",
    )(*args)
    return res[0], (tuple(res[1:]) if mode is not None else None)


def _proj_kernel(x_ref, w_ref, wab_ref, o_ref, ab_ref, xb_ref):
    j = pl.program_id(1)

    @pl.when(j == 0)
    def _():
        xb = x_ref[...].astype(BF16)
        xb_ref[...] = xb
        ab_ref[...] = _dot_nt(xb, wab_ref[...])

    o_ref[...] = _dot_nt(xb_ref[...], w_ref[...])


def _proj(x, w_t, wab_t, l, *, tm):
    m, d = x.shape
    tn = PROJ_TN
    n = 12 * D_GROUP
    return pl.pallas_call(
        _proj_kernel,
        grid=(m // tm, n // tn),
        in_specs=[
            pl.BlockSpec((tm, d), lambda i, j: (i, 0)),
            pl.BlockSpec((None, tn, d), lambda i, j: (l, j, 0)),
            pl.BlockSpec((None, 128, d), lambda i, j: (l, 0, 0)),
        ],
        out_specs=[
            pl.BlockSpec((tm, tn), lambda i, j: (i, j)),
            pl.BlockSpec((tm, 128), lambda i, j: (i, 0)),
        ],
        out_shape=[jax.ShapeDtypeStruct((m, n), F32), jax.ShapeDtypeStruct((m, 128), F32)],
        scratch_shapes=[pltpu.VMEM((tm, d), BF16)],
        compiler_params=_params("parallel", "arbitrary"),
        name="proj",
    )(x, w_t, wab_t)


def _ret_tables(pos, c):
    half = D_HEAD // 2
    inv = ROPE_BASE ** (-np.arange(half, dtype=np.float64) / half)
    ang = pos.astype(np.float64)[:, None] * inv[None, :]
    cos = np.concatenate([np.cos(ang), np.cos(ang)], -1)
    sin = np.concatenate([-np.sin(ang), np.sin(ang)], -1)
    log_g = np.log1p(-np.exp2(-5.0 - np.arange(N_HEADS, dtype=np.float64)))
    idx = np.arange(c, dtype=np.float64)
    diff = idx[:, None] - idx[None, :]
    scale = D_HEAD ** -0.5
    dmask = np.where(diff >= 0, np.exp(log_g[:, None, None] * np.maximum(diff, 0.0)), 0.0) * scale
    kdec = np.exp(log_g[:, None] * (c - 1.0 - idx)[None, :]) * scale
    qdec = np.exp(log_g[:, None] * (idx + 1.0)[None, :])
    kdec = np.broadcast_to(kdec[:, :, None], (N_HEADS, c, D_HEAD))
    qdec = np.broadcast_to(qdec[:, :, None], (N_HEADS, c, D_HEAD))
    g_c = np.exp(log_g * c)
    as32 = lambda a: jnp.asarray(np.ascontiguousarray(a), F32)
    return as32(cos), as32(sin), as32(dmask), as32(qdec), as32(kdec), [float(v) for v in g_c]


def _ret_kernel(q_ref, k_ref, v_ref, g_ref, cos_ref, sin_ref, dm_ref, qd_ref, kd_ref, gn_ref, *rest,
                c, n, g_c, n_t, bb, has_state):
    if has_state:
        s0_ref, o_ref, sf_ref, s_scr, sst_scr = rest
    else:
        o_ref, sf_ref, s_scr, sst_scr = rest
    t = pl.program_id(1)
    heads = range(N_HEADS)
    lanes = [slice(h * D_HEAD, (h + 1) * D_HEAD) for h in heads]
    tt = n * c

    @pl.when(t == 0)
    def _():
        s_scr[...] = s0_ref[...] if has_state else jnp.zeros_like(s_scr)

    cos = jnp.concatenate([cos_ref[...]] * bb, axis=0)
    sin = jnp.concatenate([sin_ref[...]] * bb, axis=0)

    def rows(ref, h):
        return ref[:, :, lanes[h]].reshape(bb * tt, D_HEAD)

    def rot(x):
        return (x * cos + pltpu.roll(x, D_HEAD // 2, 1) * sin).reshape(bb * n, c, D_HEAD)

    q3 = [rot(rows(q_ref, h)) for h in heads]
    k3 = [rot(rows(k_ref, h)) for h in heads]
    v3 = [rows(v_ref, h).reshape(bb * n, c, D_HEAD) for h in heads]
    scores = [_bmm_nt(q3[h], k3[h]) * dm_ref[h] for h in heads]
    o_intra = [_bmm(scores[h], v3[h]) for h in heads]
    kdt = [jnp.swapaxes(k3[h] * kd_ref[h], 1, 2) for h in heads]
    upd = [_bmm(kdt[h], v3[h]) for h in heads]
    for b in range(bb):
        for h in heads:
            s = s_scr[b, h]
            for i in range(n):
                sst_scr[h, b * n + i] = s.astype(BF16)
                s = g_c[h] * s + upd[h][b * n + i]
            s_scr[b, h] = s
    for h in heads:
        o = o_intra[h] + _bmm(q3[h] * qd_ref[h], sst_scr[h])
        mu = jnp.mean(o, -1, keepdims=True)
        oc = o - mu
        var = jnp.mean(oc * oc, -1, keepdims=True)
        on = (oc * lax.rsqrt(var + EPS)).reshape(bb * tt, D_HEAD) * gn_ref[:, lanes[h]]
        o_ref[:, :, lanes[h]] = (on * _silu(rows(g_ref, h))).reshape(bb, tt, D_HEAD).astype(o_ref.dtype)

    @pl.when(t == n_t - 1)
    def _():
        sf_ref[...] = s_scr[...]


def _retention(proj3, state0, gn, pos, l, *, tt, bb):
    bsz, seq, _ = proj3.shape
    c = min(CHUNK, seq)
    cos, sin, dmask, qdec, kdec, g_c = _ret_tables(pos, c)
    n_t = seq // tt
    n = tt // c
    col = lambda j: pl.BlockSpec((bb, tt, D_GROUP), lambda b, t: (b, t, j))
    full3 = pl.BlockSpec((N_HEADS, c, c), lambda b, t: (0, 0, 0))
    full3d = pl.BlockSpec((N_HEADS, c, D_HEAD), lambda b, t: (0, 0, 0))
    st = pl.BlockSpec((bb, N_HEADS, D_HEAD, D_HEAD), lambda b, t: (b, 0, 0, 0))
    has_state = state0 is not None
    state_in = [state0] if has_state else []
    state_spec = ([pl.BlockSpec((None, bb, N_HEADS, D_HEAD, D_HEAD), lambda b, t: (l, b, 0, 0, 0))]
                  if has_state else [])
    return pl.pallas_call(
        functools.partial(_ret_kernel, c=c, n=n, g_c=g_c, n_t=n_t, bb=bb, has_state=has_state),
        grid=(bsz // bb, n_t),
        in_specs=[col(0), col(1), col(2), col(3),
                  pl.BlockSpec((tt, D_HEAD), lambda b, t: (t, 0)),
                  pl.BlockSpec((tt, D_HEAD), lambda b, t: (t, 0)),
                  full3, full3d, full3d,
                  pl.BlockSpec((None, 1, D_GROUP), lambda b, t: (l, 0, 0))] + state_spec,
        out_specs=[pl.BlockSpec((bb, tt, D_GROUP), lambda b, t: (b, t, 0)), st],
        out_shape=[jax.ShapeDtypeStruct((bsz, seq, D_GROUP), BF16),
                   jax.ShapeDtypeStruct((bsz, N_HEADS, D_HEAD, D_HEAD), F32)],
        scratch_shapes=[pltpu.VMEM((bb, N_HEADS, D_HEAD, D_HEAD), F32),
                        pltpu.VMEM((N_HEADS, bb * n, D_HEAD, D_HEAD), BF16)],
        compiler_params=_params("parallel", "arbitrary"),
        name="ret",
    )(proj3, proj3, proj3, proj3, cos, sin, dmask, qdec, kdec, gn, *state_in)


PAD = 8


def _conv_block(ext_ref, x, w_ref, tt):
    ext_ref[PAD:PAD + tt, :] = x
    out = x * w_ref[CONV_W - 1:CONV_W, :]
    for j in range(1, CONV_W):
        out = out + ext_ref[PAD - j:PAD - j + tt, :] * w_ref[CONV_W - 1 - j:CONV_W - j, :]
    ext_ref[PAD - (CONV_W - 1):PAD, :] = x[tt - (CONV_W - 1):, :]
    return out


def _lru_kernel(y_ref, x_ref, cs_ref, h0_ref, cw_ref, cb_ref, wa_ref, ba_ref, wx_ref, bx_ref, lam_ref,
                o_ref, hn_ref, cn_ref, ext_ref, a_scr, b_scr, h_scr, hc_ref, *, tt, n_t):
    t = pl.program_id(1)

    @pl.when(t == 0)
    def _():
        ext_ref[PAD - (CONV_W - 1):PAD, :] = cs_ref[0]
        hc_ref[...] = h0_ref[0]

    x = x_ref[0]
    xc = _conv_block(ext_ref, x, cw_ref, tt) + cb_ref[...]
    r = jax.nn.sigmoid(_mm(xc, wa_ref[...]) + ba_ref[...])
    i = jax.nn.sigmoid(_mm(xc, wx_ref[...]) + bx_ref[...])
    log_a = -LRU_C * r * jax.nn.softplus(-lam_ref[...])
    a_scr[...] = jnp.exp(log_a)
    th = jnp.tanh(log_a)
    one_minus_a2 = -2.0 * th / (1.0 - th)
    b_scr[...] = jnp.sqrt(one_minus_a2) * (i * xc)

    def step(s, h):
        row = pl.ds(s, 1)
        h = a_scr[row, :] * h + b_scr[row, :]
        h_scr[row, :] = h
        return h

    hc_ref[...] = lax.fori_loop(0, tt, step, hc_ref[...], unroll=8)
    o_ref[0] = (h_scr[...] * _gelu(y_ref[0])).astype(o_ref.dtype)

    @pl.when(t == n_t - 1)
    def _():
        hn_ref[0] = hc_ref[...]
        cn_ref[0] = x[tt - (CONV_W - 1):, :]


def _rglru(proj3, conv_state, h0, cw, cb, wa, ba, wx, bx, lam, l, *, tt):
    bsz, seq, _ = proj3.shape
    n_t = seq // tt
    col = lambda j: pl.BlockSpec((1, tt, D_GROUP), lambda b, t: (b, t, j))
    vec = pl.BlockSpec((None, 1, D_GROUP), lambda b, t: (l, 0, 0))
    sq = pl.BlockSpec((None, D_GROUP, D_GROUP), lambda b, t: (l, 0, 0))
    cs = pl.BlockSpec((1, CONV_W - 1, D_GROUP), lambda b, t: (b, 0, 0))
    hs = pl.BlockSpec((1, 1, D_GROUP), lambda b, t: (b, 0, 0))
    return pl.pallas_call(
        functools.partial(_lru_kernel, tt=tt, n_t=n_t),
        grid=(bsz, n_t),
        in_specs=[col(4), col(5), cs, hs,
                  pl.BlockSpec((None, CONV_W, D_GROUP), lambda b, t: (l, 0, 0)), vec, sq, vec, sq, vec, vec],
        out_specs=[pl.BlockSpec((1, tt, D_GROUP), lambda b, t: (b, t, 0)), hs, cs],
        out_shape=[jax.ShapeDtypeStruct((bsz, seq, D_GROUP), BF16),
                   jax.ShapeDtypeStruct((bsz, 1, D_GROUP), F32),
                   jax.ShapeDtypeStruct((bsz, CONV_W - 1, D_GROUP), F32)],
        scratch_shapes=[pltpu.VMEM((tt + PAD, D_GROUP), F32), pltpu.VMEM((tt, D_GROUP), F32),
                        pltpu.VMEM((tt, D_GROUP), F32), pltpu.VMEM((tt, D_GROUP), F32),
                        pltpu.VMEM((1, D_GROUP), F32)],
        compiler_params=_params("parallel", "arbitrary"),
        name="lru",
    )(proj3, proj3, conv_state, h0, cw, cb, wa, ba, wx, bx, lam)


def _sg_kernel(u_ref, v_ref, lg_ref, lb_ref, ws_ref, bs_ref, o_ref, *vn_ref, c, tt, bb):
    u = _gelu(u_ref[...])
    vn = _layer_norm(_gelu(v_ref[...]), lg_ref[...], lb_ref[...])
    if vn_ref:
        vn_ref[0][...] = vn
    row = lax.broadcasted_iota(jnp.int32, (c, c), 0)
    col = lax.broadcasted_iota(jnp.int32, (c, c), 1)
    mask = (col // CHUNK) <= (row // CHUNK)
    for h in range(N_HEADS):
        lanes = slice(h * D_HEAD, (h + 1) * D_HEAD)
        w = jnp.where(mask, ws_ref[h], 0.0).astype(BF16)
        bias = bs_ref[:, h:h + 1]
        for b in range(bb):
            for i in range(tt // c):
                rows = slice(i * c, (i + 1) * c)
                s = _mm(w, vn[b, rows, lanes]) + bias
                o_ref[b, rows, lanes] = (u[b, rows, lanes] * s).astype(o_ref.dtype)


def _spatial_gate(proj3, lg, lb, ws, bs_t, l, *, tt, bb, with_vn):
    bsz, seq, _ = proj3.shape
    c = min(SG_CHUNK, seq)
    col = lambda j: pl.BlockSpec((bb, tt, D_GROUP), lambda b, t: (b, t, j))
    vec = pl.BlockSpec((None, 1, D_GROUP), lambda b, t: (l, 0, 0))
    out = pl.BlockSpec((bb, tt, D_GROUP), lambda b, t: (b, t, 0))
    res = pl.pallas_call(
        functools.partial(_sg_kernel, c=c, tt=tt, bb=bb),
        grid=(bsz // bb, seq // tt),
        in_specs=[col(6), col(7), vec, vec,
                  pl.BlockSpec((None, N_HEADS, c, c), lambda b, t: (l, 0, 0, 0)),
                  pl.BlockSpec((None, c, N_HEADS), lambda b, t: (l, 0, 0))],
        out_specs=[out, out] if with_vn else [out],
        out_shape=[jax.ShapeDtypeStruct((bsz, seq, D_GROUP), BF16)]
                  + ([jax.ShapeDtypeStruct((bsz, seq, D_GROUP), F32)] if with_vn else []),
        compiler_params=_params("parallel", "parallel"),
        name="sg",
    )(proj3, proj3, lg, lb, ws, bs_t)
    return (res[0], res[1]) if with_vn else (res[0], None)


INV_BLOCK = 16


def _unit_lower_inverse(mats, c):
    row = lax.broadcasted_iota(jnp.int32, (c, c), 0)
    col = lax.broadcasted_iota(jnp.int32, (c, c), 1)
    eye = (row == col).astype(F32)
    same = row // INV_BLOCK == col // INV_BLOCK
    ps = [jnp.where(same, a, 0.0) for a in mats]
    xs = [eye - p for p in ps]
    ps = [_bmm(p, p) for p in ps]
    n_sq = int(math.log2(INV_BLOCK)) - 1
    for step in range(n_sq):
        last = step == n_sq - 1
        lhs = xs if last else [jnp.concatenate([x, p], axis=1) for x, p in zip(xs, ps)]
        prod = [_bmm(l, p) for l, p in zip(lhs, ps)]
        xs = [x + r[:, :c, :] for x, r in zip(xs, prod)]
        if not last:
            ps = [r[:, c:, :] for r in prod]
    size = INV_BLOCK
    while size < c:
        big = 2 * size
        sel = (row // big == col // big) & (row // size != col // size)
        offs = [jnp.where(sel, a, 0.0) for a in mats]
        tmp = [_bmm(o, x) for o, x in zip(offs, xs)]
        xs = [x - _bmm(x, t) for x, t in zip(xs, tmp)]
        size = big
    return xs


def _dn_kernel(q_ref, k_ref, v_ref, z_ref, ab_ref, cw_ref, al_ref, dt_ref, gn_ref, *rest,
               c, n, n_t, bb, has_state):
    if has_state:
        cs_ref, s0_ref = rest[:2]
        rest = rest[2:]
    o_ref, sf_ref, cn_ref, ext_ref, s_scr, u_scr, wq_scr, attn_scr, kdt_scr, egl_scr = rest
    t = pl.program_id(1)
    tt = n * c
    nb = bb * n
    heads = range(N_HEADS)

    @pl.when(t == 0)
    def _():
        for j in range(3):
            hist = (cs_ref[:, :, j * D_GROUP:(j + 1) * D_GROUP] if has_state
                    else jnp.zeros((bb, CONV_W - 1, D_GROUP), F32))
            ext_ref[j, :, PAD - (CONV_W - 1):PAD, :] = hist
        s_scr[...] = s0_ref[...] if has_state else jnp.zeros_like(s_scr)

    raw = (q_ref, k_ref, v_ref)
    qf, kf, vf = [
        jnp.concatenate([_silu(_conv_block(ext_ref.at[j, b], raw[j][b],
                                           cw_ref.at[:, j * D_GROUP:(j + 1) * D_GROUP], tt))
                         for b in range(bb)], axis=0)
        for j in range(3)]

    row = lax.broadcasted_iota(jnp.int32, (c, c), 0)
    col = lax.broadcasted_iota(jnp.int32, (c, c), 1)
    tri = row >= col
    strict = row > col
    tri_f = tri.astype(F32)
    scale = D_HEAD ** -0.5

    ab = ab_ref[...].reshape(bb * tt, 128)
    g_all = -jnp.exp(al_ref[...]) * jax.nn.softplus(ab + dt_ref[...])
    beta_all = jax.nn.sigmoid(ab)
    gc_all = jnp.concatenate(
        [jnp.dot(tri_f, g_all[i * c:(i + 1) * c], preferred_element_type=F32, precision=lax.Precision.HIGHEST)
         for i in range(nb)], axis=0)
    gc_t = gc_all.T

    def l2n(x):
        return x * lax.rsqrt(jnp.sum(x * x, -1, keepdims=True) + 1e-6)

    def split(x, h):
        return x[:, h * D_HEAD:(h + 1) * D_HEAD].reshape(nb, c, D_HEAD)

    q3 = [l2n(split(qf, h)) * scale for h in heads]
    k3 = [l2n(split(kf, h)) for h in heads]
    v3 = [split(vf, h) for h in heads]
    beta = [beta_all[:, N_HEADS + h:N_HEADS + h + 1].reshape(nb, c, 1) for h in heads]
    gcol = [gc_all[:, h:h + 1].reshape(nb, c, 1) for h in heads]
    grow = [jnp.stack([gc_t[h:h + 1, i * c:(i + 1) * c] for i in range(nb)]) for h in heads]
    glast = [g[:, c - 1:c, :] for g in gcol]
    decay = [jnp.where(tri, jnp.exp(gc - gr), 0.0) for gc, gr in zip(gcol, grow)]
    egc = [jnp.exp(g) for g in gcol]
    kb = [k * b for k, b in zip(k3, beta)]
    vb = [v * b for v, b in zip(v3, beta)]
    a_mat = [jnp.where(strict, _bmm_nt(x, k) * d, 0.0) for x, k, d in zip(kb, k3, decay)]
    tinv = _unit_lower_inverse(a_mat, c)
    uw = [_bmm(ti, jnp.concatenate([x, y * e], axis=-1)) for ti, x, y, e in zip(tinv, vb, kb, egc)]
    attn = [_bmm_nt(q, k) * d for q, k, d in zip(q3, k3, decay)]
    for h in heads:
        u_scr[h] = uw[h][:, :, :D_HEAD]
        wq_scr[h] = jnp.concatenate([uw[h][:, :, D_HEAD:], q3[h] * egc[h]], axis=1).astype(BF16)
        attn_scr[h] = attn[h].astype(BF16)
        kdt_scr[h] = jnp.swapaxes(k3[h] * jnp.exp(glast[h] - gcol[h]), 1, 2).astype(BF16)
        egl_scr[h] = jnp.broadcast_to(jnp.exp(glast[h]), (nb, 1, D_HEAD))

    pairs = [(b, h) for b in range(bb) for h in heads]

    def chunk(i, carry):
        rows = pl.ds(pl.multiple_of(i * c, c), c)
        s = [s_scr[b, h] for b, h in pairs]
        r = [_mm(wq_scr[h, b * n + i], s[p]) for p, (b, h) in enumerate(pairs)]
        v_new = [u_scr[h, b * n + i] - r[p][:c] for p, (b, h) in enumerate(pairs)]
        for p, (b, h) in enumerate(pairs):
            s_scr[b, h] = s[p] * egl_scr[h, b * n + i] + _mm(kdt_scr[h, b * n + i], v_new[p])
        o = [r[p][c:] + _mm(attn_scr[h, b * n + i], v_new[p]) for p, (b, h) in enumerate(pairs)]
        for p, (b, h) in enumerate(pairs):
            lanes = slice(h * D_HEAD, (h + 1) * D_HEAD)
            on = o[p] * lax.rsqrt(jnp.mean(o[p] * o[p], -1, keepdims=True) + EPS) * gn_ref[...]
            o_ref[b, rows, lanes] = (on * _silu(z_ref[b, rows, lanes])).astype(o_ref.dtype)
        return carry

    lax.fori_loop(0, n, chunk, 0)

    @pl.when(t == n_t - 1)
    def _():
        sf_ref[...] = s_scr[...]
        for j in range(3):
            cn_ref[:, :, j * D_GROUP:(j + 1) * D_GROUP] = raw[j][:, tt - (CONV_W - 1):, :]


def _deltanet(proj3, ab3, states, cw, a_log, dt_bias, gn, l, *, tt, bb):
    bsz, seq, _ = proj3.shape
    c = min(CHUNK, seq)
    n_t = seq // tt
    n = tt // c
    nb = bb * n
    col = lambda j: pl.BlockSpec((bb, tt, D_GROUP), lambda b, t: (b, t, j))
    st = pl.BlockSpec((bb, N_HEADS, D_HEAD, D_HEAD), lambda b, t: (b, 0, 0, 0))
    cs = pl.BlockSpec((bb, CONV_W - 1, 3 * D_GROUP), lambda b, t: (b, 0, 0))
    lane_row = pl.BlockSpec((None, 1, 128), lambda b, t: (l, 0, 0))
    has_state = states is not None
    state_in = list(states) if has_state else []
    state_spec = ([pl.BlockSpec((None, bb, CONV_W - 1, 3 * D_GROUP), lambda b, t: (l, b, 0, 0)),
                   pl.BlockSpec((None, bb, N_HEADS, D_HEAD, D_HEAD), lambda b, t: (l, b, 0, 0, 0))]
                  if has_state else [])
    return pl.pallas_call(
        functools.partial(_dn_kernel, c=c, n=n, n_t=n_t, bb=bb, has_state=has_state),
        grid=(bsz // bb, n_t),
        in_specs=[col(8), col(9), col(10), col(11),
                  pl.BlockSpec((bb, tt, 128), lambda b, t: (b, t, 0)),
                  pl.BlockSpec((None, CONV_W, 3 * D_GROUP), lambda b, t: (l, 0, 0)),
                  lane_row, lane_row, lane_row] + state_spec,
        out_specs=[pl.BlockSpec((bb, tt, D_GROUP), lambda b, t: (b, t, 0)), st, cs],
        out_shape=[jax.ShapeDtypeStruct((bsz, seq, D_GROUP), BF16),
                   jax.ShapeDtypeStruct((bsz, N_HEADS, D_HEAD, D_HEAD), F32),
                   jax.ShapeDtypeStruct((bsz, CONV_W - 1, 3 * D_GROUP), F32)],
        scratch_shapes=[pltpu.VMEM((3, bb, tt + PAD, D_GROUP), F32),
                        pltpu.VMEM((bb, N_HEADS, D_HEAD, D_HEAD), F32),
                        pltpu.VMEM((N_HEADS, nb, c, D_HEAD), F32),
                        pltpu.VMEM((N_HEADS, nb, 2 * c, D_HEAD), BF16),
                        pltpu.VMEM((N_HEADS, nb, c, c), BF16),
                        pltpu.VMEM((N_HEADS, nb, D_HEAD, c), BF16),
                        pltpu.VMEM((N_HEADS, nb, 1, D_HEAD), F32)],
        compiler_params=_params("parallel", "arbitrary"),
        name="dn",
    )(proj3, proj3, proj3, proj3, ab3, cw, a_log, dt_bias, gn, *state_in)


def _mixout_kernel(a_ref, b_ref, c_ref, d_ref, x_ref, w_ref, g_ref, bb_ref, o_ref, cat_ref):
    for j, part in enumerate((a_ref, b_ref, c_ref, d_ref)):
        cat_ref[:, j * D_GROUP:(j + 1) * D_GROUP] = part[...]
    tr = o_ref.shape[0] // 2
    for r in range(2):
        rows = slice(r * tr, (r + 1) * tr)
        acc = jnp.dot(cat_ref[rows, :], w_ref[...], preferred_element_type=F32)
        o_ref[rows, :] = _layer_norm(ALPHA * x_ref[rows, :] + acc, g_ref[...], bb_ref[...])


def _mixout(a, b, c, d, x, w, g, bb, l, *, tm):
    m, dm = x.shape
    part = pl.BlockSpec((tm, D_GROUP), lambda i: (i, 0))
    vec = pl.BlockSpec((None, 1, dm), lambda i: (l, 0, 0))
    return pl.pallas_call(
        _mixout_kernel,
        grid=(m // tm,),
        in_specs=[part, part, part, part,
                  pl.BlockSpec((tm, dm), lambda i: (i, 0)),
                  pl.BlockSpec((None, 4 * D_GROUP, dm), lambda i: (l, 0, 0), pipeline_mode=pl.Buffered(1)),
                  vec, vec],
        out_specs=pl.BlockSpec((tm, dm), lambda i: (i, 0)),
        out_shape=jax.ShapeDtypeStruct((m, dm), F32),
        scratch_shapes=[pltpu.VMEM((tm, 4 * D_GROUP), BF16)],
        compiler_params=_params("parallel"),
        name="mixout",
    )(a, b, c, d, x, w, g, bb)


def _block_diag(w):
    dep, h, n, _ = w.shape
    eye = jnp.eye(h, dtype=w.dtype)
    return (eye[None, :, None, :, None] * w[:, :, :, None, :]).reshape(dep, h * n, h * n)


def _prep(p):
    row = lambda v: v.reshape(DEPTH, 1, -1)
    pad128 = lambda v: jnp.pad(row(v), ((0, 0), (0, 0), (0, 128 - v.shape[-1])))
    q = dict(p)
    q['w_mix_out'] = p['w_mix_out'].astype(BF16)
    w_in_t = jnp.swapaxes(p['w_mix_in'], 1, 2)
    q['w_mix_in'] = w_in_t.astype(BF16)
    q['w_mix_ab'] = jnp.pad(w_in_t[:, 12 * D_GROUP:, :], ((0, 0), (0, 128 - 2 * N_HEADS), (0, 0))).astype(BF16)
    for name in ('ln1_g', 'ln1_b', 'ln2_g', 'ln2_b', 'ln3_g', 'ln3_b', 'ret_norm_g', 'lru_conv_b',
                 'lru_b_a', 'lru_b_x', 'lru_lam', 'sg_ln_g', 'sg_ln_b', 'dn_norm_g'):
        q[name] = row(p[name])
    q['lru_w_a'] = _block_diag(p['lru_w_a']).astype(BF16)
    q['lru_w_x'] = _block_diag(p['lru_w_x']).astype(BF16)
    q['dn_a_log'] = pad128(p['dn_a_log'])
    q['dn_dt_bias'] = pad128(p['dn_dt_bias'])
    return q


def _token_mixers(x, st, p, l, s):
    bsz, seq, tm, tt, bb = s['bsz'], s['seq'], s['tm'], s['tt'], s['bb']
    c_sg = min(SG_CHUNK, seq)
    sg_w = p['sg_w'][:, :, :c_sg, :c_sg]
    sg_b_t = jnp.swapaxes(p['sg_b'][:, :, :c_sg], 1, 2)
    proj, ab = _proj(x, p['w_mix_in'], p['w_mix_ab'], l, tm=tm)
    proj3 = proj.reshape(bsz, seq, -1)
    ab3 = ab.reshape(bsz, seq, 128)
    if st is None:
        s_ret = dn_states = None
        s_lru, s_lru_conv = jnp.zeros((bsz, D_GROUP), F32), jnp.zeros((bsz, CONV_W - 1, D_GROUP), F32)
    else:
        s_ret, s_lru, s_lru_conv, s_dn, s_dn_conv = st
        s_lru, s_lru_conv = s_lru[l], s_lru_conv[l]
        dn_states = (s_dn_conv, s_dn)
    out_a, ret_new = _retention(proj3, s_ret, p['ret_norm_g'], s['pos'], l, tt=tt, bb=bb)
    out_b, lru_h_new, lru_conv_new = _rglru(
        proj3, s_lru_conv, s_lru.reshape(bsz, 1, D_GROUP), p['lru_conv_w'], p['lru_conv_b'],
        p['lru_w_a'], p['lru_b_a'], p['lru_w_x'], p['lru_b_x'], p['lru_lam'], l, tt=tt)
    out_c, sg_v = _spatial_gate(proj3, p['sg_ln_g'], p['sg_ln_b'], sg_w, sg_b_t, l,
                                tt=min(seq, 4 * SG_CHUNK), bb=bb, with_vn=s['keep_sg_v'])
    out_d, dn_new, dn_conv_new = _deltanet(proj3, ab3, dn_states, p['dn_conv_w'],
                                           p['dn_a_log'], p['dn_dt_bias'], p['dn_norm_g'], l, tt=tt, bb=bb)
    flat = lambda o: o.reshape(bsz * seq, D_GROUP)
    x = _mixout(flat(out_a), flat(out_b), flat(out_c), flat(out_d), x, p['w_mix_out'],
                p['ln2_g'], p['ln2_b'], l, tm=tm)
    return x, (ret_new, lru_h_new.reshape(bsz, D_GROUP), lru_conv_new, dn_new, dn_conv_new, sg_v)


def kernel(x_prompt, x_sample, state_ret, state_lru_h, state_lru_conv, state_dn, state_dn_conv, ffn1_w_in, ffn1_w_out, ln1_g, ln1_b, w_mix_in, ret_norm_g, lru_conv_w, lru_conv_b, lru_w_a, lru_b_a, lru_w_x, lru_b_x, lru_lam, sg_ln_g, sg_ln_b, sg_w, sg_b, dn_conv_w, dn_a_log, dn_dt_bias, dn_norm_g, w_mix_out, ln2_g, ln2_b, ffn2_w_in, ffn2_w_out, ln3_g, ln3_b):
    weights = dict(ffn1_w_in=ffn1_w_in, ffn1_w_out=ffn1_w_out, ln1_g=ln1_g, ln1_b=ln1_b, w_mix_in=w_mix_in,
                   ret_norm_g=ret_norm_g, lru_conv_w=lru_conv_w, lru_conv_b=lru_conv_b, lru_w_a=lru_w_a,
                   lru_b_a=lru_b_a, lru_w_x=lru_w_x, lru_b_x=lru_b_x, lru_lam=lru_lam, sg_ln_g=sg_ln_g,
                   sg_ln_b=sg_ln_b, sg_w=sg_w, sg_b=sg_b, dn_conv_w=dn_conv_w, dn_a_log=dn_a_log,
                   dn_dt_bias=dn_dt_bias, dn_norm_g=dn_norm_g, w_mix_out=w_mix_out, ln2_g=ln2_g, ln2_b=ln2_b,
                   ffn2_w_in=ffn2_w_in, ffn2_w_out=ffn2_w_out, ln3_g=ln3_g, ln3_b=ln3_b)
    p = _prep(weights)

    bp, lp, dm = x_prompt.shape
    bs, ls, _ = x_sample.shape
    prompt = dict(bsz=bp, seq=lp, tm=1024, tt=min(lp, 512), bb=1, pos=np.arange(lp), keep_sg_v=False)
    sample = dict(bsz=bs, seq=ls, tm=bs * ls, tt=ls, bb=min(bs, 8), pos=PAST_LEN + np.arange(ls), keep_sg_v=True)
    sample_states = (state_ret, state_lru_h, state_lru_conv, state_dn, state_dn_conv)
    xp = x_prompt.reshape(bp * lp, dm)
    xs = x_sample.reshape(bs * ls, dm)

    ffns = [(l, name) for l in range(DEPTH) for name in ('ffn1', 'ffn2')]
    ln_of = {'ffn1': ('ln1_g', 'ln1_b'), 'ffn2': ('ln3_g', 'ln3_b')}

    def ffn_pair(k, xp, xs, wb):
        l, name = ffns[k]
        g, b = p[ln_of[name][0]], p[ln_of[name][1]]
        nxt = None
        if k + 1 < len(ffns):
            ln, nname = ffns[k + 1]
            nxt = (weights[nname + '_w_in'], weights[nname + '_w_out'], ln)
        if wb is None:
            xs, wb = _ffn(xs, (weights[name + '_w_in'], weights[name + '_w_out']), g, b, l, tm=sample['tm'],
                          from_f32=True)
        else:
            xs, _ = _ffn(xs, wb, g, b, l, tm=sample['tm'], tf=FFN_TF_ONE_ROW_TILE)
        xp, wb_next = _ffn(xp, wb, g, b, l, tm=prompt['tm'], cast_next=nxt)
        return xp, xs, wb_next

    wb = None

    st_p, st_s = [], []
    for l in range(DEPTH):
        xp, xs, wb = ffn_pair(2 * l, xp, xs, wb)
        xp, new_p = _token_mixers(xp, None, p, l, prompt)
        xs, new_s = _token_mixers(xs, sample_states, p, l, sample)
        xp, xs, wb = ffn_pair(2 * l + 1, xp, xs, wb)
        st_p.append(new_p)
        st_s.append(new_s)

    stack = lambda sts, i: jnp.stack([s[i] for s in sts])
    return (xp.reshape(bp, lp, dm), xs.reshape(bs, ls, dm),
            stack(st_p, 0), stack(st_p, 1), stack(st_p, 2), stack(st_p, 3), stack(st_p, 4),
            stack(st_s, 0), stack(st_s, 1), stack(st_s, 2), stack(st_s, 3), stack(st_s, 4), stack(st_s, 5))
```

```python
import functools
import math

import numpy as np
import jax
import jax.numpy as jnp
from jax import lax
from jax.experimental import pallas as pl
from jax.experimental.pallas import tpu as pltpu

F32 = jnp.float32
BF16 = jnp.bfloat16

DEPTH = 2
PAST_LEN = 4096
CHUNK = 64
D_GROUP = 512
N_HEADS = 4
D_HEAD = D_GROUP // N_HEADS
H_LRU = 8
LRU_C = 8.0
CONV_W = 4
SG_CHUNK = 128
ROPE_BASE = 10000.0
ALPHA = (2.0 * DEPTH) ** 0.25
EPS = 1e-5

FFN_TF = 512
PROJ_TN = 2048
VMEM_LIMIT = 58 * 1024 * 1024


def _params(*sem):
    return pltpu.CompilerParams(dimension_semantics=sem, vmem_limit_bytes=VMEM_LIMIT)


def _mm(a, b):
    return jnp.dot(a.astype(BF16), b.astype(BF16), preferred_element_type=F32)


def _dot_nt(a, b):
    return lax.dot_general(a, b, (((1,), (1,)), ((), ())), preferred_element_type=F32)


def _bmm(a, b):
    return jnp.einsum('nij,njk->nik', a.astype(BF16), b.astype(BF16), preferred_element_type=F32)


def _bmm_nt(a, b):
    return jnp.einsum('nid,njd->nij', a.astype(BF16), b.astype(BF16), preferred_element_type=F32)


def _layer_norm(y, g, b):
    mu = jnp.mean(y, -1, keepdims=True)
    yc = y - mu
    var = jnp.mean(yc * yc, -1, keepdims=True)
    return yc * lax.rsqrt(var + EPS) * g + b


def _silu(x):
    return x * jax.nn.sigmoid(x)


def _gelu(x):
    return jax.nn.gelu(x, approximate=True)


def _ffn_kernel(x_hbm, wg_ref, wu_ref, wo_ref, g_ref, b_ref, *rest, nf, ni, n_row, mode):
    *rest, xb_ref, xf_ref, x_sem = rest
    if mode == 'cast_next':
        ng_ref, nu_ref, no_ref, o_ref, ngo_ref, nuo_ref, noo_ref = rest
        ngo_ref[...] = ng_ref[...].astype(BF16)
        nuo_ref[...] = nu_ref[...].astype(BF16)
        noo_ref[...] = no_ref[...].astype(BF16)
    elif mode == 'from_f32':
        o_ref, wgo_ref, wuo_ref, woo_ref = rest
    else:
        (o_ref,) = rest
    i = pl.program_id(0)
    f = pl.program_id(1)
    tm = xf_ref.shape[0]

    def x_copy(k):
        return pltpu.make_async_copy(x_hbm.at[pl.ds(k * tm, tm), :], xf_ref, x_sem)

    @pl.when(f == 0)
    def _():
        @pl.when(i == 0)
        def _():
            x_copy(0).start()

        x_copy(i).wait()
        x = xf_ref[...]
        xb_ref[...] = x.astype(BF16)
        o_ref[...] = (2.0 * ALPHA) * x

    @pl.when((f == 1) & (i + 1 < ni))
    def _():
        x_copy(i + 1).start()

    wg, wu, wo = wg_ref[...], wu_ref[...], wo_ref[...]
    if mode == 'from_f32':
        wg, wu, wo = wg.astype(BF16), wu.astype(BF16), wo.astype(BF16)
        wgo_ref[...] = wg
        wuo_ref[...] = wu
        woo_ref[...] = wo
    tr = o_ref.shape[0] // n_row

    for r in range(n_row):
        rows = slice(r * tr, (r + 1) * tr)
        xb = xb_ref[rows, :]
        gate = jnp.dot(xb, wg, preferred_element_type=F32)
        up = jnp.dot(xb, wu, preferred_element_type=F32)
        h = (_silu(gate) * up).astype(BF16)
        o_ref[rows, :] += jnp.dot(h, wo, preferred_element_type=F32)

    @pl.when(f == nf - 1)
    def _():
        o_ref[...] = _layer_norm(0.5 * o_ref[...], g_ref[...], b_ref[...])


def _ffn(x, w, g, b, l, *, tm, tf=FFN_TF, cast_next=None, from_f32=False):
    m, d = x.shape
    d_ff = w[1].shape[1] if from_f32 else w[2].shape[0]
    nf = d_ff // tf
    assert nf * tf == d_ff
    ni = m // tm
    vec = pl.BlockSpec((None, 1, d), lambda i, f: (l, 0, 0))
    assert nf >= 2 or ni == 1
    x_spec = pl.BlockSpec(memory_space=pl.ANY)
    tile_g = pl.BlockSpec((d, tf), lambda i, f: (0, f))
    tile_o = pl.BlockSpec((tf, d), lambda i, f: (f, 0))
    w_shapes = [jax.ShapeDtypeStruct((d, d_ff), BF16), jax.ShapeDtypeStruct((d, d_ff), BF16),
                jax.ShapeDtypeStruct((d_ff, d), BF16)]
    out_specs = [pl.BlockSpec((tm, d), lambda i, f: (i, 0))]
    out_shape = [jax.ShapeDtypeStruct((m, d), F32)]
    mode = None
    if from_f32:
        assert ni == 1 and cast_next is None
        mode = 'from_f32'
        w_in, w_out = w
        in_specs = [x_spec,
                    pl.BlockSpec((None, d, tf), lambda i, f: (l, 0, f)),
                    pl.BlockSpec((None, d, tf), lambda i, f: (l, 0, f + nf)),
                    pl.BlockSpec((None, tf, d), lambda i, f: (l, f, 0)), vec, vec]
        args = [x, w_in, w_in, w_out, g, b]
        out_specs += [tile_g, tile_g, tile_o]
        out_shape += w_shapes
    else:
        in_specs = [x_spec, tile_g, tile_g, tile_o, vec, vec]
        args = [x, w[0], w[1], w[2], g, b]
    if cast_next is not None:
        mode = 'cast_next'
        nw_in, nw_out, ln = cast_next
        ri, ro = d // ni, d_ff // (ni * nf)
        assert ri * ni == d and ro * ni * nf == d_ff
        in_specs += [pl.BlockSpec((None, ri, tf), lambda i, f: (ln, i, f)),
                     pl.BlockSpec((None, ri, tf), lambda i, f: (ln, i, f + nf)),
                     pl.BlockSpec((None, ro, d), lambda i, f: (ln, i * nf + f, 0))]
        args += [nw_in, nw_in, nw_out]
        out_specs += [pl.BlockSpec((ri, tf), lambda i, f: (i, f)), pl.BlockSpec((ri, tf), lambda i, f: (i, f)),
                      pl.BlockSpec((ro, d), lambda i, f: (i * nf + f, 0))]
        out_shape += w_shapes
    res = pl.pallas_call(
        functools.partial(_ffn_kernel, nf=nf, ni=ni, n_row=2, mode=mode),
        grid=(ni, nf),
        in_specs=in_specs,
        out_specs=out_specs,
        out_shape=out_shape,
        scratch_shapes=[pltpu.VMEM((tm, d), BF16), pltpu.VMEM((tm, d), F32), pltpu.SemaphoreType.DMA(())],
        compiler_params=_params("arbitrary", "arbitrary"),
        name="ffn",
    )(*args)
    return res[0], (tuple(res[1:]) if mode is not None else None)


def _proj_kernel(x_ref, w_ref, wab_ref, o_ref, ab_ref, xb_ref):
    j = pl.program_id(1)

    @pl.when(j == 0)
    def _():
        xb = x_ref[...].astype(BF16)
        xb_ref[...] = xb
        ab_ref[...] = _dot_nt(xb, wab_ref[...])

    o_ref[...] = _dot_nt(xb_ref[...], w_ref[...])


def _proj(x, w_t, wab_t, l, *, tm):
    m, d = x.shape
    tn = PROJ_TN
    n = 12 * D_GROUP
    return pl.pallas_call(
        _proj_kernel,
        grid=(m // tm, n // tn),
        in_specs=[
            pl.BlockSpec((tm, d), lambda i, j: (i, 0)),
            pl.BlockSpec((None, tn, d), lambda i, j: (l, j, 0)),
            pl.BlockSpec((None, 128, d), lambda i, j: (l, 0, 0)),
        ],
        out_specs=[
            pl.BlockSpec((tm, tn), lambda i, j: (i, j)),
            pl.BlockSpec((tm, 128), lambda i, j: (i, 0)),
        ],
        out_shape=[jax.ShapeDtypeStruct((m, n), F32), jax.ShapeDtypeStruct((m, 128), F32)],
        scratch_shapes=[pltpu.VMEM((tm, d), BF16)],
        compiler_params=_params("parallel", "arbitrary"),
        name="proj",
    )(x, w_t, wab_t)


def _ret_tables(pos, c):
    half = D_HEAD // 2
    inv = ROPE_BASE ** (-np.arange(half, dtype=np.float64) / half)
    ang = pos.astype(np.float64)[:, None] * inv[None, :]
    cos = np.concatenate([np.cos(ang), np.cos(ang)], -1)
    sin = np.concatenate([-np.sin(ang), np.sin(ang)], -1)
    log_g = np.log1p(-np.exp2(-5.0 - np.arange(N_HEADS, dtype=np.float64)))
    idx = np.arange(c, dtype=np.float64)
    diff = idx[:, None] - idx[None, :]
    scale = D_HEAD ** -0.5
    dmask = np.where(diff >= 0, np.exp(log_g[:, None, None] * np.maximum(diff, 0.0)), 0.0) * scale
    kdec = np.exp(log_g[:, None] * (c - 1.0 - idx)[None, :]) * scale
    qdec = np.exp(log_g[:, None] * (idx + 1.0)[None, :])
    kdec = np.broadcast_to(kdec[:, :, None], (N_HEADS, c, D_HEAD))
    qdec = np.broadcast_to(qdec[:, :, None], (N_HEADS, c, D_HEAD))
    g_c = np.exp(log_g * c)
    as32 = lambda a: jnp.asarray(np.ascontiguousarray(a), F32)
    return as32(cos), as32(sin), as32(dmask), as32(qdec), as32(kdec), [float(v) for v in g_c]


def _ret_kernel(q_ref, k_ref, v_ref, g_ref, cos_ref, sin_ref, dm_ref, qd_ref, kd_ref, gn_ref, *rest,
                c, n, g_c, n_t, bb, has_state):
    if has_state:
        s0_ref, o_ref, sf_ref, s_scr, sst_scr = rest
    else:
        o_ref, sf_ref, s_scr, sst_scr = rest
    t = pl.program_id(1)
    heads = range(N_HEADS)
    lanes = [slice(h * D_HEAD, (h + 1) * D_HEAD) for h in heads]
    tt = n * c

    @pl.when(t == 0)
    def _():
        s_scr[...] = s0_ref[...] if has_state else jnp.zeros_like(s_scr)

    cos = jnp.concatenate([cos_ref[...]] * bb, axis=0)
    sin = jnp.concatenate([sin_ref[...]] * bb, axis=0)

    def rows(ref, h):
        return ref[:, :, lanes[h]].reshape(bb * tt, D_HEAD)

    def rot(x):
        return (x * cos + pltpu.roll(x, D_HEAD // 2, 1) * sin).reshape(bb * n, c, D_HEAD)

    q3 = [rot(rows(q_ref, h)) for h in heads]
    k3 = [rot(rows(k_ref, h)) for h in heads]
    v3 = [rows(v_ref, h).reshape(bb * n, c, D_HEAD) for h in heads]
    scores = [_bmm_nt(q3[h], k3[h]) * dm_ref[h] for h in heads]
    o_intra = [_bmm(scores[h], v3[h]) for h in heads]
    kdt = [jnp.swapaxes(k3[h] * kd_ref[h], 1, 2) for h in heads]
    upd = [_bmm(kdt[h], v3[h]) for h in heads]
    for b in range(bb):
        for h in heads:
            s = s_scr[b, h]
            for i in range(n):
                sst_scr[h, b * n + i] = s.astype(BF16)
                s = g_c[h] * s + upd[h][b * n + i]
            s_scr[b, h] = s
    for h in heads:
        o = o_intra[h] + _bmm(q3[h] * qd_ref[h], sst_scr[h])
        mu = jnp.mean(o, -1, keepdims=True)
        oc = o - mu
        var = jnp.mean(oc * oc, -1, keepdims=True)
        on = (oc * lax.rsqrt(var + EPS)).reshape(bb * tt, D_HEAD) * gn_ref[:, lanes[h]]
        o_ref[:, :, lanes[h]] = (on * _silu(rows(g_ref, h))).reshape(bb, tt, D_HEAD).astype(o_ref.dtype)

    @pl.when(t == n_t - 1)
    def _():
        sf_ref[...] = s_scr[...]


def _retention(proj3, state0, gn, pos, l, *, tt, bb):
    bsz, seq, _ = proj3.shape
    c = min(CHUNK, seq)
    cos, sin, dmask, qdec, kdec, g_c = _ret_tables(pos, c)
    n_t = seq // tt
    n = tt // c
    col = lambda j: pl.BlockSpec((bb, tt, D_GROUP), lambda b, t: (b, t, j))
    full3 = pl.BlockSpec((N_HEADS, c, c), lambda b, t: (0, 0, 0))
    full3d = pl.BlockSpec((N_HEADS, c, D_HEAD), lambda b, t: (0, 0, 0))
    st = pl.BlockSpec((bb, N_HEADS, D_HEAD, D_HEAD), lambda b, t: (b, 0, 0, 0))
    has_state = state0 is not None
    state_in = [state0] if has_state else []
    state_spec = ([pl.BlockSpec((None, bb, N_HEADS, D_HEAD, D_HEAD), lambda b, t: (l, b, 0, 0, 0))]
                  if has_state else [])
    return pl.pallas_call(
        functools.partial(_ret_kernel, c=c, n=n, g_c=g_c, n_t=n_t, bb=bb, has_state=has_state),
        grid=(bsz // bb, n_t),
        in_specs=[col(0), col(1), col(2), col(3),
                  pl.BlockSpec((tt, D_HEAD), lambda b, t: (t, 0)),
                  pl.BlockSpec((tt, D_HEAD), lambda b, t: (t, 0)),
                  full3, full3d, full3d,
                  pl.BlockSpec((None, 1, D_GROUP), lambda b, t: (l, 0, 0))] + state_spec,
        out_specs=[pl.BlockSpec((bb, tt, D_GROUP), lambda b, t: (b, t, 0)), st],
        out_shape=[jax.ShapeDtypeStruct((bsz, seq, D_GROUP), BF16),
                   jax.ShapeDtypeStruct((bsz, N_HEADS, D_HEAD, D_HEAD), F32)],
        scratch_shapes=[pltpu.VMEM((bb, N_HEADS, D_HEAD, D_HEAD), F32),
                        pltpu.VMEM((N_HEADS, bb * n, D_HEAD, D_HEAD), BF16)],
        compiler_params=_params("parallel", "arbitrary"),
        name="ret",
    )(proj3, proj3, proj3, proj3, cos, sin, dmask, qdec, kdec, gn, *state_in)


PAD = 8


def _conv_block(ext_ref, x, w_ref, tt):
    ext_ref[PAD:PAD + tt, :] = x
    out = x * w_ref[CONV_W - 1:CONV_W, :]
    for j in range(1, CONV_W):
        out = out + ext_ref[PAD - j:PAD - j + tt, :] * w_ref[CONV_W - 1 - j:CONV_W - j, :]
    ext_ref[PAD - (CONV_W - 1):PAD, :] = x[tt - (CONV_W - 1):, :]
    return out


def _lru_kernel(y_ref, x_ref, cs_ref, h0_ref, cw_ref, cb_ref, wa_ref, ba_ref, wx_ref, bx_ref, lam_ref,
                o_ref, hn_ref, cn_ref, ext_ref, a_scr, b_scr, h_scr, hc_ref, *, tt, n_t):
    t = pl.program_id(1)

    @pl.when(t == 0)
    def _():
        ext_ref[PAD - (CONV_W - 1):PAD, :] = cs_ref[0]
        hc_ref[...] = h0_ref[0]

    x = x_ref[0]
    xc = _conv_block(ext_ref, x, cw_ref, tt) + cb_ref[...]
    r = jax.nn.sigmoid(_mm(xc, wa_ref[...]) + ba_ref[...])
    i = jax.nn.sigmoid(_mm(xc, wx_ref[...]) + bx_ref[...])
    log_a = -LRU_C * r * jax.nn.softplus(-lam_ref[...])
    a_scr[...] = jnp.exp(log_a)
    th = jnp.tanh(log_a)
    one_minus_a2 = -2.0 * th / (1.0 - th)
    b_scr[...] = jnp.sqrt(one_minus_a2) * (i * xc)

    def step(s, h):
        row = pl.ds(s, 1)
        h = a_scr[row, :] * h + b_scr[row, :]
        h_scr[row, :] = h
        return h

    hc_ref[...] = lax.fori_loop(0, tt, step, hc_ref[...], unroll=8)
    o_ref[0] = (h_scr[...] * _gelu(y_ref[0])).astype(o_ref.dtype)

    @pl.when(t == n_t - 1)
    def _():
        hn_ref[0] = hc_ref[...]
        cn_ref[0] = x[tt - (CONV_W - 1):, :]


def _rglru(proj3, conv_state, h0, cw, cb, wa, ba, wx, bx, lam, l, *, tt):
    bsz, seq, _ = proj3.shape
    n_t = seq // tt
    col = lambda j: pl.BlockSpec((1, tt, D_GROUP), lambda b, t: (b, t, j))
    vec = pl.BlockSpec((None, 1, D_GROUP), lambda b, t: (l, 0, 0))
    sq = pl.BlockSpec((None, D_GROUP, D_GROUP), lambda b, t: (l, 0, 0))
    cs = pl.BlockSpec((1, CONV_W - 1, D_GROUP), lambda b, t: (b, 0, 0))
    hs = pl.BlockSpec((1, 1, D_GROUP), lambda b, t: (b, 0, 0))
    return pl.pallas_call(
        functools.partial(_lru_kernel, tt=tt, n_t=n_t),
        grid=(bsz, n_t),
        in_specs=[col(4), col(5), cs, hs,
                  pl.BlockSpec((None, CONV_W, D_GROUP), lambda b, t: (l, 0, 0)), vec, sq, vec, sq, vec, vec],
        out_specs=[pl.BlockSpec((1, tt, D_GROUP), lambda b, t: (b, t, 0)), hs, cs],
        out_shape=[jax.ShapeDtypeStruct((bsz, seq, D_GROUP), BF16),
                   jax.ShapeDtypeStruct((bsz, 1, D_GROUP), F32),
                   jax.ShapeDtypeStruct((bsz, CONV_W - 1, D_GROUP), F32)],
        scratch_shapes=[pltpu.VMEM((tt + PAD, D_GROUP), F32), pltpu.VMEM((tt, D_GROUP), F32),
                        pltpu.VMEM((tt, D_GROUP), F32), pltpu.VMEM((tt, D_GROUP), F32),
                        pltpu.VMEM((1, D_GROUP), F32)],
        compiler_params=_params("parallel", "arbitrary"),
        name="lru",
    )(proj3, proj3, conv_state, h0, cw, cb, wa, ba, wx, bx, lam)


def _sg_kernel(u_ref, v_ref, lg_ref, lb_ref, ws_ref, bs_ref, o_ref, *vn_ref, c, tt, bb):
    u = _gelu(u_ref[...])
    vn = _layer_norm(_gelu(v_ref[...]), lg_ref[...], lb_ref[...])
    if vn_ref:
        vn_ref[0][...] = vn
    row = lax.broadcasted_iota(jnp.int32, (c, c), 0)
    col = lax.broadcasted_iota(jnp.int32, (c, c), 1)
    mask = (col // CHUNK) <= (row // CHUNK)
    for h in range(N_HEADS):
        lanes = slice(h * D_HEAD, (h + 1) * D_HEAD)
        w = jnp.where(mask, ws_ref[h], 0.0).astype(BF16)
        bias = bs_ref[:, h:h + 1]
        for b in range(bb):
            for i in range(tt // c):
                rows = slice(i * c, (i + 1) * c)
                s = _mm(w, vn[b, rows, lanes]) + bias
                o_ref[b, rows, lanes] = (u[b, rows, lanes] * s).astype(o_ref.dtype)


def _spatial_gate(proj3, lg, lb, ws, bs_t, l, *, tt, bb, with_vn):
    bsz, seq, _ = proj3.shape
    c = min(SG_CHUNK, seq)
    col = lambda j: pl.BlockSpec((bb, tt, D_GROUP), lambda b, t: (b, t, j))
    vec = pl.BlockSpec((None, 1, D_GROUP), lambda b, t: (l, 0, 0))
    out = pl.BlockSpec((bb, tt, D_GROUP), lambda b, t: (b, t, 0))
    res = pl.pallas_call(
        functools.partial(_sg_kernel, c=c, tt=tt, bb=bb),
        grid=(bsz // bb, seq // tt),
        in_specs=[col(6), col(7), vec, vec,
                  pl.BlockSpec((None, N_HEADS, c, c), lambda b, t: (l, 0, 0, 0)),
                  pl.BlockSpec((None, c, N_HEADS), lambda b, t: (l, 0, 0))],
        out_specs=[out, out] if with_vn else [out],
        out_shape=[jax.ShapeDtypeStruct((bsz, seq, D_GROUP), BF16)]
                  + ([jax.ShapeDtypeStruct((bsz, seq, D_GROUP), F32)] if with_vn else []),
        compiler_params=_params("parallel", "parallel"),
        name="sg",
    )(proj3, proj3, lg, lb, ws, bs_t)
    return (res[0], res[1]) if with_vn else (res[0], None)


INV_BLOCK = 16


def _unit_lower_inverse(mats, c):
    row = lax.broadcasted_iota(jnp.int32, (c, c), 0)
    col = lax.broadcasted_iota(jnp.int32, (c, c), 1)
    eye = (row == col).astype(F32)
    same = row // INV_BLOCK == col // INV_BLOCK
    ps = [jnp.where(same, a, 0.0) for a in mats]
    xs = [eye - p for p in ps]
    ps = [_bmm(p, p) for p in ps]
    n_sq = int(math.log2(INV_BLOCK)) - 1
    for step in range(n_sq):
        last = step == n_sq - 1
        lhs = xs if last else [jnp.concatenate([x, p], axis=1) for x, p in zip(xs, ps)]
        prod = [_bmm(l, p) for l, p in zip(lhs, ps)]
        xs = [x + r[:, :c, :] for x, r in zip(xs, prod)]
        if not last:
            ps = [r[:, c:, :] for r in prod]
    size = INV_BLOCK
    while size < c:
        big = 2 * size
        sel = (row // big == col // big) & (row // size != col // size)
        offs = [jnp.where(sel, a, 0.0) for a in mats]
        tmp = [_bmm(o, x) for o, x in zip(offs, xs)]
        xs = [x - _bmm(x, t) for x, t in zip(xs, tmp)]
        size = big
    return xs


def _dn_kernel(q_ref, k_ref, v_ref, z_ref, ab_ref, cw_ref, al_ref, dt_ref, gn_ref, *rest,
               c, n, n_t, bb, has_state):
    if has_state:
        cs_ref, s0_ref = rest[:2]
        rest = rest[2:]
    o_ref, sf_ref, cn_ref, ext_ref, s_scr, u_scr, wq_scr, attn_scr, kdt_scr, egl_scr = rest
    t = pl.program_id(1)
    tt = n * c
    nb = bb * n
    heads = range(N_HEADS)

    @pl.when(t == 0)
    def _():
        for j in range(3):
            hist = (cs_ref[:, :, j * D_GROUP:(j + 1) * D_GROUP] if has_state
                    else jnp.zeros((bb, CONV_W - 1, D_GROUP), F32))
            ext_ref[j, :, PAD - (CONV_W - 1):PAD, :] = hist
        s_scr[...] = s0_ref[...] if has_state else jnp.zeros_like(s_scr)

    raw = (q_ref, k_ref, v_ref)
    qf, kf, vf = [
        jnp.concatenate([_silu(_conv_block(ext_ref.at[j, b], raw[j][b],
                                           cw_ref.at[:, j * D_GROUP:(j + 1) * D_GROUP], tt))
                         for b in range(bb)], axis=0)
        for j in range(3)]

    row = lax.broadcasted_iota(jnp.int32, (c, c), 0)
    col = lax.broadcasted_iota(jnp.int32, (c, c), 1)
    tri = row >= col
    strict = row > col
    tri_f = tri.astype(F32)
    scale = D_HEAD ** -0.5

    ab = ab_ref[...].reshape(bb * tt, 128)
    g_all = -jnp.exp(al_ref[...]) * jax.nn.softplus(ab + dt_ref[...])
    beta_all = jax.nn.sigmoid(ab)
    gc_all = jnp.concatenate(
        [jnp.dot(tri_f, g_all[i * c:(i + 1) * c], preferred_element_type=F32, precision=lax.Precision.HIGHEST)
         for i in range(nb)], axis=0)
    gc_t = gc_all.T

    def l2n(x):
        return x * lax.rsqrt(jnp.sum(x * x, -1, keepdims=True) + 1e-6)

    def split(x, h, k0, k1):
        return x[k0 * c:k1 * c, h * D_HEAD:(h + 1) * D_HEAD].reshape(k1 - k0, c, D_HEAD)

    def chunk_local(k0, k1):
        m = k1 - k0
        rows = slice(k0 * c, k1 * c)
        q3 = [l2n(split(qf, h, k0, k1)) * scale for h in heads]
        k3 = [l2n(split(kf, h, k0, k1)) for h in heads]
        v3 = [split(vf, h, k0, k1) for h in heads]
        beta = [beta_all[rows, N_HEADS + h:N_HEADS + h + 1].reshape(m, c, 1) for h in heads]
        gcol = [gc_all[rows, h:h + 1].reshape(m, c, 1) for h in heads]
        grow = [jnp.stack([gc_t[h:h + 1, i * c:(i + 1) * c] for i in range(k0, k1)]) for h in heads]
        glast = [g[:, c - 1:c, :] for g in gcol]
        decay = [jnp.where(tri, jnp.exp(gc - gr), 0.0) for gc, gr in zip(gcol, grow)]
        egc = [jnp.exp(g) for g in gcol]
        kb = [k * b for k, b in zip(k3, beta)]
        vb = [v * b for v, b in zip(v3, beta)]
        a_mat = [jnp.where(strict, _bmm_nt(x, k) * d, 0.0) for x, k, d in zip(kb, k3, decay)]
        tinv = _unit_lower_inverse(a_mat, c)
        uw = [_bmm(ti, jnp.concatenate([x, y * e], axis=-1)) for ti, x, y, e in zip(tinv, vb, kb, egc)]
        attn = [_bmm_nt(q, k) * d for q, k, d in zip(q3, k3, decay)]
        for h in heads:
            u_scr[h, k0:k1] = uw[h][:, :, :D_HEAD]
            wq_scr[h, k0:k1] = jnp.concatenate([uw[h][:, :, D_HEAD:], q3[h] * egc[h]], axis=1).astype(BF16)
            attn_scr[h, k0:k1] = attn[h].astype(BF16)
            kdt_scr[h, k0:k1] = jnp.swapaxes(k3[h] * jnp.exp(glast[h] - gcol[h]), 1, 2).astype(BF16)
            egl_scr[h, k0:k1] = jnp.broadcast_to(jnp.exp(glast[h]), (m, 1, D_HEAD))

    def sequential(k0, k1, state):
        for k in range(k0, k1):
            b, i = divmod(k, n)
            rows = slice(i * c, (i + 1) * c)
            s = [state[b, h] for h in heads]
            r = [_mm(wq_scr[h, k], s[h]) for h in heads]
            v_new = [u_scr[h, k] - r[h][:c] for h in heads]
            for h in heads:
                state[b, h] = s[h] * egl_scr[h, k] + _mm(kdt_scr[h, k], v_new[h])
            o = [r[h][c:] + _mm(attn_scr[h, k], v_new[h]) for h in heads]
            for h in heads:
                lanes = slice(h * D_HEAD, (h + 1) * D_HEAD)
                on = o[h] * lax.rsqrt(jnp.mean(o[h] * o[h], -1, keepdims=True) + EPS) * gn_ref[...]
                o_ref[b, rows, lanes] = (on * _silu(z_ref[b, rows, lanes])).astype(o_ref.dtype)

    groups = 2 if nb % 2 == 0 else 1
    bounds = [(g * nb // groups, (g + 1) * nb // groups) for g in range(groups)]
    for k0, k1 in bounds:
        chunk_local(k0, k1)
    state = {(b, h): s_scr[b, h] for b in range(bb) for h in heads}
    for k0, k1 in bounds:
        sequential(k0, k1, state)
    for (b, h), s in state.items():
        s_scr[b, h] = s

    @pl.when(t == n_t - 1)
    def _():
        sf_ref[...] = s_scr[...]
        for j in range(3):
            cn_ref[:, :, j * D_GROUP:(j + 1) * D_GROUP] = raw[j][:, tt - (CONV_W - 1):, :]


def _deltanet(proj3, ab3, states, cw, a_log, dt_bias, gn, l, *, tt, bb):
    bsz, seq, _ = proj3.shape
    c = min(CHUNK, seq)
    n_t = seq // tt
    n = tt // c
    nb = bb * n
    col = lambda j: pl.BlockSpec((bb, tt, D_GROUP), lambda b, t: (b, t, j))
    st = pl.BlockSpec((bb, N_HEADS, D_HEAD, D_HEAD), lambda b, t: (b, 0, 0, 0))
    cs = pl.BlockSpec((bb, CONV_W - 1, 3 * D_GROUP), lambda b, t: (b, 0, 0))
    lane_row = pl.BlockSpec((None, 1, 128), lambda b, t: (l, 0, 0))
    has_state = states is not None
    state_in = list(states) if has_state else []
    state_spec = ([pl.BlockSpec((None, bb, CONV_W - 1, 3 * D_GROUP), lambda b, t: (l, b, 0, 0)),
                   pl.BlockSpec((None, bb, N_HEADS, D_HEAD, D_HEAD), lambda b, t: (l, b, 0, 0, 0))]
                  if has_state else [])
    return pl.pallas_call(
        functools.partial(_dn_kernel, c=c, n=n, n_t=n_t, bb=bb, has_state=has_state),
        grid=(bsz // bb, n_t),
        in_specs=[col(8), col(9), col(10), col(11),
                  pl.BlockSpec((bb, tt, 128), lambda b, t: (b, t, 0)),
                  pl.BlockSpec((None, CONV_W, 3 * D_GROUP), lambda b, t: (l, 0, 0)),
                  lane_row, lane_row, lane_row] + state_spec,
        out_specs=[pl.BlockSpec((bb, tt, D_GROUP), lambda b, t: (b, t, 0)), st, cs],
        out_shape=[jax.ShapeDtypeStruct((bsz, seq, D_GROUP), BF16),
                   jax.ShapeDtypeStruct((bsz, N_HEADS, D_HEAD, D_HEAD), F32),
                   jax.ShapeDtypeStruct((bsz, CONV_W - 1, 3 * D_GROUP), F32)],
        scratch_shapes=[pltpu.VMEM((3, bb, tt + PAD, D_GROUP), F32),
                        pltpu.VMEM((bb, N_HEADS, D_HEAD, D_HEAD), F32),
                        pltpu.VMEM((N_HEADS, nb, c, D_HEAD), F32),
                        pltpu.VMEM((N_HEADS, nb, 2 * c, D_HEAD), BF16),
                        pltpu.VMEM((N_HEADS, nb, c, c), BF16),
                        pltpu.VMEM((N_HEADS, nb, D_HEAD, c), BF16),
                        pltpu.VMEM((N_HEADS, nb, 1, D_HEAD), F32)],
        compiler_params=_params("parallel", "arbitrary"),
        name="dn",
    )(proj3, proj3, proj3, proj3, ab3, cw, a_log, dt_bias, gn, *state_in)


def _mixout_kernel(a_ref, b_ref, c_ref, d_ref, x_ref, w_ref, g_ref, bb_ref, o_ref, cat_ref):
    for j, part in enumerate((a_ref, b_ref, c_ref, d_ref)):
        cat_ref[:, j * D_GROUP:(j + 1) * D_GROUP] = part[...]
    n_row = 4 if o_ref.shape[0] >= 1024 else 2
    tr = o_ref.shape[0] // n_row
    for r in range(n_row):
        rows = slice(r * tr, (r + 1) * tr)
        acc = jnp.dot(cat_ref[rows, :], w_ref[...], preferred_element_type=F32)
        o_ref[rows, :] = _layer_norm(ALPHA * x_ref[rows, :] + acc, g_ref[...], bb_ref[...])


def _mixout(a, b, c, d, x, w, g, bb, l, *, tm):
    m, dm = x.shape
    part = pl.BlockSpec((tm, D_GROUP), lambda i: (i, 0))
    vec = pl.BlockSpec((None, 1, dm), lambda i: (l, 0, 0))
    return pl.pallas_call(
        _mixout_kernel,
        grid=(m // tm,),
        in_specs=[part, part, part, part,
                  pl.BlockSpec((tm, dm), lambda i: (i, 0)),
                  pl.BlockSpec((None, 4 * D_GROUP, dm), lambda i: (l, 0, 0), pipeline_mode=pl.Buffered(1)),
                  vec, vec],
        out_specs=pl.BlockSpec((tm, dm), lambda i: (i, 0)),
        out_shape=jax.ShapeDtypeStruct((m, dm), F32),
        scratch_shapes=[pltpu.VMEM((tm, 4 * D_GROUP), BF16)],
        compiler_params=_params("parallel"),
        name="mixout",
    )(a, b, c, d, x, w, g, bb)


def _block_diag(w):
    dep, h, n, _ = w.shape
    eye = jnp.eye(h, dtype=w.dtype)
    return (eye[None, :, None, :, None] * w[:, :, :, None, :]).reshape(dep, h * n, h * n)


def _prep(p):
    row = lambda v: v.reshape(DEPTH, 1, -1)
    pad128 = lambda v: jnp.pad(row(v), ((0, 0), (0, 0), (0, 128 - v.shape[-1])))
    q = dict(p)
    q['w_mix_out'] = p['w_mix_out'].astype(BF16)
    w_in_t = jnp.swapaxes(p['w_mix_in'], 1, 2)
    q['w_mix_in'] = w_in_t.astype(BF16)
    q['w_mix_ab'] = jnp.pad(w_in_t[:, 12 * D_GROUP:, :], ((0, 0), (0, 128 - 2 * N_HEADS), (0, 0))).astype(BF16)
    for name in ('ln1_g', 'ln1_b', 'ln2_g', 'ln2_b', 'ln3_g', 'ln3_b', 'ret_norm_g', 'lru_conv_b',
                 'lru_b_a', 'lru_b_x', 'lru_lam', 'sg_ln_g', 'sg_ln_b', 'dn_norm_g'):
        q[name] = row(p[name])
    q['lru_w_a'] = _block_diag(p['lru_w_a']).astype(BF16)
    q['lru_w_x'] = _block_diag(p['lru_w_x']).astype(BF16)
    q['dn_a_log'] = pad128(p['dn_a_log'])
    q['dn_dt_bias'] = pad128(p['dn_dt_bias'])
    return q


def _token_mixers(x, st, p, l, s):
    bsz, seq, tm, tt, bb = s['bsz'], s['seq'], s['tm'], s['tt'], s['bb']
    c_sg = min(SG_CHUNK, seq)
    sg_w = p['sg_w'][:, :, :c_sg, :c_sg]
    sg_b_t = jnp.swapaxes(p['sg_b'][:, :, :c_sg], 1, 2)
    proj, ab = _proj(x, p['w_mix_in'], p['w_mix_ab'], l, tm=tm)
    proj3 = proj.reshape(bsz, seq, -1)
    ab3 = ab.reshape(bsz, seq, 128)
    if st is None:
        s_ret = dn_states = None
        s_lru, s_lru_conv = jnp.zeros((bsz, D_GROUP), F32), jnp.zeros((bsz, CONV_W - 1, D_GROUP), F32)
    else:
        s_ret, s_lru, s_lru_conv, s_dn, s_dn_conv = st
        s_lru, s_lru_conv = s_lru[l], s_lru_conv[l]
        dn_states = (s_dn_conv, s_dn)
    out_a, ret_new = _retention(proj3, s_ret, p['ret_norm_g'], s['pos'], l, tt=tt, bb=bb)
    out_b, lru_h_new, lru_conv_new = _rglru(
        proj3, s_lru_conv, s_lru.reshape(bsz, 1, D_GROUP), p['lru_conv_w'], p['lru_conv_b'],
        p['lru_w_a'], p['lru_b_a'], p['lru_w_x'], p['lru_b_x'], p['lru_lam'], l, tt=tt)
    out_c, sg_v = _spatial_gate(proj3, p['sg_ln_g'], p['sg_ln_b'], sg_w, sg_b_t, l,
                                tt=min(seq, 4 * SG_CHUNK), bb=bb, with_vn=s['keep_sg_v'])
    out_d, dn_new, dn_conv_new = _deltanet(proj3, ab3, dn_states, p['dn_conv_w'],
                                           p['dn_a_log'], p['dn_dt_bias'], p['dn_norm_g'], l, tt=tt, bb=bb)
    flat = lambda o: o.reshape(bsz * seq, D_GROUP)
    x = _mixout(flat(out_a), flat(out_b), flat(out_c), flat(out_d), x, p['w_mix_out'],
                p['ln2_g'], p['ln2_b'], l, tm=tm)
    return x, (ret_new, lru_h_new.reshape(bsz, D_GROUP), lru_conv_new, dn_new, dn_conv_new, sg_v)


def kernel(x_prompt, x_sample, state_ret, state_lru_h, state_lru_conv, state_dn, state_dn_conv, ffn1_w_in, ffn1_w_out, ln1_g, ln1_b, w_mix_in, ret_norm_g, lru_conv_w, lru_conv_b, lru_w_a, lru_b_a, lru_w_x, lru_b_x, lru_lam, sg_ln_g, sg_ln_b, sg_w, sg_b, dn_conv_w, dn_a_log, dn_dt_bias, dn_norm_g, w_mix_out, ln2_g, ln2_b, ffn2_w_in, ffn2_w_out, ln3_g, ln3_b):
    weights = dict(ffn1_w_in=ffn1_w_in, ffn1_w_out=ffn1_w_out, ln1_g=ln1_g, ln1_b=ln1_b, w_mix_in=w_mix_in,
                   ret_norm_g=ret_norm_g, lru_conv_w=lru_conv_w, lru_conv_b=lru_conv_b, lru_w_a=lru_w_a,
                   lru_b_a=lru_b_a, lru_w_x=lru_w_x, lru_b_x=lru_b_x, lru_lam=lru_lam, sg_ln_g=sg_ln_g,
                   sg_ln_b=sg_ln_b, sg_w=sg_w, sg_b=sg_b, dn_conv_w=dn_conv_w, dn_a_log=dn_a_log,
                   dn_dt_bias=dn_dt_bias, dn_norm_g=dn_norm_g, w_mix_out=w_mix_out, ln2_g=ln2_g, ln2_b=ln2_b,
                   ffn2_w_in=ffn2_w_in, ffn2_w_out=ffn2_w_out, ln3_g=ln3_g, ln3_b=ln3_b)
    p = _prep(weights)

    bp, lp, dm = x_prompt.shape
    bs, ls, _ = x_sample.shape
    prompt = dict(bsz=bp, seq=lp, tm=1024, tt=min(lp, 512), bb=1, pos=np.arange(lp), keep_sg_v=False)
    sample = dict(bsz=bs, seq=ls, tm=bs * ls, tt=ls, bb=min(bs, 8), pos=PAST_LEN + np.arange(ls), keep_sg_v=True)
    sample_states = (state_ret, state_lru_h, state_lru_conv, state_dn, state_dn_conv)
    xp = x_prompt.reshape(bp * lp, dm)
    xs = x_sample.reshape(bs * ls, dm)

    ffns = [(l, name) for l in range(DEPTH) for name in ('ffn1', 'ffn2')]
    ln_of = {'ffn1': ('ln1_g', 'ln1_b'), 'ffn2': ('ln3_g', 'ln3_b')}

    def ffn_pair(k, xp, xs, wb):
        l, name = ffns[k]
        g, b = p[ln_of[name][0]], p[ln_of[name][1]]
        nxt = None
        if k + 1 < len(ffns):
            ln, nname = ffns[k + 1]
            nxt = (weights[nname + '_w_in'], weights[nname + '_w_out'], ln)
        if wb is None:
            xs, wb = _ffn(xs, (weights[name + '_w_in'], weights[name + '_w_out']), g, b, l, tm=sample['tm'],
                          from_f32=True)
        else:
            xs, _ = _ffn(xs, wb, g, b, l, tm=sample['tm'])
        xp, wb_next = _ffn(xp, wb, g, b, l, tm=prompt['tm'], cast_next=nxt)
        return xp, xs, wb_next

    wb = None

    st_p, st_s = [], []
    for l in range(DEPTH):
        xp, xs, wb = ffn_pair(2 * l, xp, xs, wb)
        xp, new_p = _token_mixers(xp, None, p, l, prompt)
        xs, new_s = _token_mixers(xs, sample_states, p, l, sample)
        xp, xs, wb = ffn_pair(2 * l + 1, xp, xs, wb)
        st_p.append(new_p)
        st_s.append(new_s)

    stack = lambda sts, i: jnp.stack([s[i] for s in sts])
    return (xp.reshape(bp, lp, dm), xs.reshape(bs, ls, dm),
            stack(st_p, 0), stack(st_p, 1), stack(st_p, 2), stack(st_p, 3), stack(st_p, 4),
            stack(st_s, 0), stack(st_s, 1), stack(st_s, 2), stack(st_s, 3), stack(st_s, 4), stack(st_s, 5))
```

```python
import functools
import math

import numpy as np
import jax
import jax.numpy as jnp
from jax import lax
from jax.experimental import pallas as pl
from jax.experimental.pallas import tpu as pltpu

F32 = jnp.float32
BF16 = jnp.bfloat16

DEPTH = 2
PAST_LEN = 4096
CHUNK = 64
D_GROUP = 512
N_HEADS = 4
D_HEAD = D_GROUP // N_HEADS
H_LRU = 8
LRU_C = 8.0
CONV_W = 4
SG_CHUNK = 128
ROPE_BASE = 10000.0
ALPHA = (2.0 * DEPTH) ** 0.25
EPS = 1e-5

FFN_TF = 512
PROJ_TN = 2048
VMEM_LIMIT = 58 * 1024 * 1024


def _params(*sem):
    return pltpu.CompilerParams(dimension_semantics=sem, vmem_limit_bytes=VMEM_LIMIT)


def _mm(a, b):
    return jnp.dot(a.astype(BF16), b.astype(BF16), preferred_element_type=F32)


def _dot_nt(a, b):
    return lax.dot_general(a, b, (((1,), (1,)), ((), ())), preferred_element_type=F32)


def _bmm(a, b):
    return jnp.einsum('nij,njk->nik', a.astype(BF16), b.astype(BF16), preferred_element_type=F32)


def _bmm_nt(a, b):
    return jnp.einsum('nid,njd->nij', a.astype(BF16), b.astype(BF16), preferred_element_type=F32)


def _layer_norm(y, g, b):
    mu = jnp.mean(y, -1, keepdims=True)
    yc = y - mu
    var = jnp.mean(yc * yc, -1, keepdims=True)
    return yc * lax.rsqrt(var + EPS) * g + b


def _silu(x):
    return x * jax.nn.sigmoid(x)


def _gelu(x):
    return jax.nn.gelu(x, approximate=True)


def _ffn_kernel(x_hbm, wg_ref, wu_ref, wo_ref, g_ref, b_ref, *rest, nf, ni, n_row, mode, l):
    *rest, xb_ref, xf_ref, x_sem = rest
    if mode == 'cast_next':
        ng_ref, nu_ref, no_ref, o_ref, ngo_ref, nuo_ref, noo_ref = rest
        ngo_ref[...] = ng_ref[...].astype(BF16)
        nuo_ref[...] = nu_ref[...].astype(BF16)
        noo_ref[...] = no_ref[...].astype(BF16)
    elif mode == 'from_f32':
        o_ref, wgo_ref, wuo_ref, woo_ref = rest
    else:
        (o_ref,) = rest
    i = pl.program_id(0)
    f = pl.program_id(1)
    tm = xf_ref.shape[0]

    def x_copy(k):
        return pltpu.make_async_copy(x_hbm.at[pl.ds(k * tm, tm), :], xf_ref, x_sem)

    @pl.when(f == 0)
    def _():
        @pl.when(i == 0)
        def _():
            x_copy(0).start()

        x_copy(i).wait()
        x = xf_ref[...]
        xb_ref[...] = x.astype(BF16)
        o_ref[...] = (2.0 * ALPHA) * x

    @pl.when((f == 1) & (i + 1 < ni))
    def _():
        x_copy(i + 1).start()

    wg, wu, wo = wg_ref[...], wu_ref[...], wo_ref[...]
    if mode == 'from_f32':
        wg, wu, wo = wg.astype(BF16), wu.astype(BF16), wo.astype(BF16)
        wgo_ref[...] = wg
        wuo_ref[...] = wu
        woo_ref[...] = wo
    tr = o_ref.shape[0] // n_row

    for r in range(n_row):
        rows = slice(r * tr, (r + 1) * tr)
        xb = xb_ref[rows, :]
        gate = jnp.dot(xb, wg, preferred_element_type=F32)
        up = jnp.dot(xb, wu, preferred_element_type=F32)
        h = (_silu(gate) * up).astype(BF16)
        o_ref[rows, :] += jnp.dot(h, wo, preferred_element_type=F32)

    @pl.when(f == nf - 1)
    def _():
        o_ref[...] = _layer_norm(0.5 * o_ref[...], g_ref[l:l + 1, :], b_ref[l:l + 1, :])


def _ffn(x, w, g, b, l, *, tm, tf=FFN_TF, cast_next=None, from_f32=False):
    m, d = x.shape
    d_ff = w[1].shape[1] if from_f32 else w[2].shape[0]
    nf = d_ff // tf
    assert nf * tf == d_ff
    ni = m // tm
    vec = pl.BlockSpec((DEPTH, d), lambda i, f: (0, 0))
    assert nf >= 2 or ni == 1
    x_spec = pl.BlockSpec(memory_space=pl.ANY)
    tile_g = pl.BlockSpec((d, tf), lambda i, f: (0, f))
    tile_o = pl.BlockSpec((tf, d), lambda i, f: (f, 0))
    w_shapes = [jax.ShapeDtypeStruct((d, d_ff), BF16), jax.ShapeDtypeStruct((d, d_ff), BF16),
                jax.ShapeDtypeStruct((d_ff, d), BF16)]
    out_specs = [pl.BlockSpec((tm, d), lambda i, f: (i, 0))]
    out_shape = [jax.ShapeDtypeStruct((m, d), F32)]
    mode = None
    if from_f32:
        assert ni == 1 and cast_next is None
        mode = 'from_f32'
        w_in, w_out = w
        in_specs = [x_spec,
                    pl.BlockSpec((None, d, tf), lambda i, f: (l, 0, f)),
                    pl.BlockSpec((None, d, tf), lambda i, f: (l, 0, f + nf)),
                    pl.BlockSpec((None, tf, d), lambda i, f: (l, f, 0)), vec, vec]
        args = [x, w_in, w_in, w_out, g, b]
        out_specs += [tile_g, tile_g, tile_o]
        out_shape += w_shapes
    else:
        in_specs = [x_spec, tile_g, tile_g, tile_o, vec, vec]
        args = [x, w[0], w[1], w[2], g, b]
    if cast_next is not None:
        mode = 'cast_next'
        nw_in, nw_out, ln = cast_next
        ri, ro = d // ni, d_ff // (ni * nf)
        assert ri * ni == d and ro * ni * nf == d_ff
        in_specs += [pl.BlockSpec((None, ri, tf), lambda i, f: (ln, i, f)),
                     pl.BlockSpec((None, ri, tf), lambda i, f: (ln, i, f + nf)),
                     pl.BlockSpec((None, ro, d), lambda i, f: (ln, i * nf + f, 0))]
        args += [nw_in, nw_in, nw_out]
        out_specs += [pl.BlockSpec((ri, tf), lambda i, f: (i, f)), pl.BlockSpec((ri, tf), lambda i, f: (i, f)),
                      pl.BlockSpec((ro, d), lambda i, f: (i * nf + f, 0))]
        out_shape += w_shapes
    res = pl.pallas_call(
        functools.partial(_ffn_kernel, nf=nf, ni=ni, n_row=2 if tm >= 1024 else 1, mode=mode, l=l),
        grid=(ni, nf),
        in_specs=in_specs,
        out_specs=out_specs,
        out_shape=out_shape,
        scratch_shapes=[pltpu.VMEM((tm, d), BF16), pltpu.VMEM((tm, d), F32), pltpu.SemaphoreType.DMA(())],
        compiler_params=_params("arbitrary", "arbitrary"),
        name="ffn",
    )(*args)
    return res[0], (tuple(res[1:]) if mode is not None else None)


def _proj_kernel(x_ref, w_ref, wab_ref, o_ref, ab_ref, xb_ref):
    j = pl.program_id(1)

    @pl.when(j == 0)
    def _():
        xb = x_ref[...].astype(BF16)
        xb_ref[...] = xb
        ab_ref[...] = _dot_nt(xb, wab_ref[...])

    o_ref[...] = _dot_nt(xb_ref[...], w_ref[...])


def _proj(x, w_t, wab_t, l, *, tm):
    m, d = x.shape
    tn = PROJ_TN
    n = 12 * D_GROUP
    return pl.pallas_call(
        _proj_kernel,
        grid=(m // tm, n // tn),
        in_specs=[
            pl.BlockSpec((tm, d), lambda i, j: (i, 0)),
            pl.BlockSpec((None, tn, d), lambda i, j: (l, j, 0)),
            pl.BlockSpec((None, 128, d), lambda i, j: (l, 0, 0)),
        ],
        out_specs=[
            pl.BlockSpec((tm, tn), lambda i, j: (i, j)),
            pl.BlockSpec((tm, 128), lambda i, j: (i, 0)),
        ],
        out_shape=[jax.ShapeDtypeStruct((m, n), F32), jax.ShapeDtypeStruct((m, 128), F32)],
        scratch_shapes=[pltpu.VMEM((tm, d), BF16)],
        compiler_params=_params("parallel", "arbitrary"),
        name="proj",
    )(x, w_t, wab_t)


def _ret_tables(pos, c):
    half = D_HEAD // 2
    inv = ROPE_BASE ** (-np.arange(half, dtype=np.float64) / half)
    ang = pos.astype(np.float64)[:, None] * inv[None, :]
    cos = np.concatenate([np.cos(ang), np.cos(ang)], -1)
    sin = np.concatenate([-np.sin(ang), np.sin(ang)], -1)
    log_g = np.log1p(-np.exp2(-5.0 - np.arange(N_HEADS, dtype=np.float64)))
    idx = np.arange(c, dtype=np.float64)
    diff = idx[:, None] - idx[None, :]
    scale = D_HEAD ** -0.5
    dmask = np.where(diff >= 0, np.exp(log_g[:, None, None] * np.maximum(diff, 0.0)), 0.0) * scale
    kdec = np.exp(log_g[:, None] * (c - 1.0 - idx)[None, :]) * scale
    qdec = np.exp(log_g[:, None] * (idx + 1.0)[None, :])
    kdec = np.broadcast_to(kdec[:, :, None], (N_HEADS, c, D_HEAD))
    qdec = np.broadcast_to(qdec[:, :, None], (N_HEADS, c, D_HEAD))
    g_c = np.exp(log_g * c)
    as32 = lambda a: jnp.asarray(np.ascontiguousarray(a), F32)
    return as32(cos), as32(sin), as32(dmask), as32(qdec), as32(kdec), [float(v) for v in g_c]


def _ret_kernel(q_ref, k_ref, v_ref, g_ref, cos_ref, sin_ref, dm_ref, qd_ref, kd_ref, gn_ref, *rest,
                c, n, g_c, n_t, bb, has_state, l):
    if has_state:
        s0_ref, o_ref, sf_ref, s_scr, sst_scr = rest
    else:
        o_ref, sf_ref, s_scr, sst_scr = rest
    t = pl.program_id(1)
    heads = range(N_HEADS)
    lanes = [slice(h * D_HEAD, (h + 1) * D_HEAD) for h in heads]
    tt = n * c

    @pl.when(t == 0)
    def _():
        s_scr[...] = s0_ref[...] if has_state else jnp.zeros_like(s_scr)

    cos = jnp.concatenate([cos_ref[...]] * bb, axis=0)
    sin = jnp.concatenate([sin_ref[...]] * bb, axis=0)

    def rows(ref, h):
        return ref[:, :, lanes[h]].reshape(bb * tt, D_HEAD)

    def rot(x):
        return (x * cos + pltpu.roll(x, D_HEAD // 2, 1) * sin).reshape(bb * n, c, D_HEAD)

    q3 = [rot(rows(q_ref, h)) for h in heads]
    k3 = [rot(rows(k_ref, h)) for h in heads]
    v3 = [rows(v_ref, h).reshape(bb * n, c, D_HEAD) for h in heads]
    scores = [_bmm_nt(q3[h], k3[h]) * dm_ref[h] for h in heads]
    o_intra = [_bmm(scores[h], v3[h]) for h in heads]
    kdt = [jnp.swapaxes(k3[h] * kd_ref[h], 1, 2) for h in heads]
    upd = [_bmm(kdt[h], v3[h]) for h in heads]
    for b in range(bb):
        for h in heads:
            s = s_scr[b, h]
            for i in range(n):
                sst_scr[h, b * n + i] = s.astype(BF16)
                s = g_c[h] * s + upd[h][b * n + i]
            s_scr[b, h] = s
    for h in heads:
        o = o_intra[h] + _bmm(q3[h] * qd_ref[h], sst_scr[h])
        mu = jnp.mean(o, -1, keepdims=True)
        oc = o - mu
        var = jnp.mean(oc * oc, -1, keepdims=True)
        on = (oc * lax.rsqrt(var + EPS)).reshape(bb * tt, D_HEAD) * gn_ref[l:l + 1, lanes[h]]
        o_ref[:, :, lanes[h]] = (on * _silu(rows(g_ref, h))).reshape(bb, tt, D_HEAD).astype(o_ref.dtype)

    @pl.when(t == n_t - 1)
    def _():
        sf_ref[...] = s_scr[...]


def _retention(proj3, state0, gn, pos, l, *, tt, bb):
    bsz, seq, _ = proj3.shape
    c = min(CHUNK, seq)
    cos, sin, dmask, qdec, kdec, g_c = _ret_tables(pos, c)
    n_t = seq // tt
    n = tt // c
    col = lambda j: pl.BlockSpec((bb, tt, D_GROUP), lambda b, t: (b, t, j))
    full3 = pl.BlockSpec((N_HEADS, c, c), lambda b, t: (0, 0, 0))
    full3d = pl.BlockSpec((N_HEADS, c, D_HEAD), lambda b, t: (0, 0, 0))
    st = pl.BlockSpec((bb, N_HEADS, D_HEAD, D_HEAD), lambda b, t: (b, 0, 0, 0))
    has_state = state0 is not None
    state_in = [state0] if has_state else []
    state_spec = ([pl.BlockSpec((None, bb, N_HEADS, D_HEAD, D_HEAD), lambda b, t: (l, b, 0, 0, 0))]
                  if has_state else [])
    return pl.pallas_call(
        functools.partial(_ret_kernel, c=c, n=n, g_c=g_c, n_t=n_t, bb=bb, has_state=has_state, l=l),
        grid=(bsz // bb, n_t),
        in_specs=[col(0), col(1), col(2), col(3),
                  pl.BlockSpec((tt, D_HEAD), lambda b, t: (t, 0)),
                  pl.BlockSpec((tt, D_HEAD), lambda b, t: (t, 0)),
                  full3, full3d, full3d,
                  pl.BlockSpec((DEPTH, D_GROUP), lambda b, t: (0, 0))] + state_spec,
        out_specs=[pl.BlockSpec((bb, tt, D_GROUP), lambda b, t: (b, t, 0)), st],
        out_shape=[jax.ShapeDtypeStruct((bsz, seq, D_GROUP), BF16),
                   jax.ShapeDtypeStruct((bsz, N_HEADS, D_HEAD, D_HEAD), F32)],
        scratch_shapes=[pltpu.VMEM((bb, N_HEADS, D_HEAD, D_HEAD), F32),
                        pltpu.VMEM((N_HEADS, bb * n, D_HEAD, D_HEAD), BF16)],
        compiler_params=_params("parallel", "arbitrary"),
        name="ret",
    )(proj3, proj3, proj3, proj3, cos, sin, dmask, qdec, kdec, gn, *state_in)


PAD = 8


def _conv_block(ext_ref, x, w_ref, tt):
    ext_ref[PAD:PAD + tt, :] = x
    out = x * w_ref[CONV_W - 1:CONV_W, :]
    for j in range(1, CONV_W):
        out = out + ext_ref[PAD - j:PAD - j + tt, :] * w_ref[CONV_W - 1 - j:CONV_W - j, :]
    ext_ref[PAD - (CONV_W - 1):PAD, :] = x[tt - (CONV_W - 1):, :]
    return out


def _lru_kernel(y_ref, x_ref, cs_ref, h0_ref, cw_ref, cb_ref, wa_ref, ba_ref, wx_ref, bx_ref, lam_ref,
                o_ref, hn_ref, cn_ref, ext_ref, a_scr, b_scr, h_scr, hc_ref, *, tt, n_t, l):
    t = pl.program_id(1)
    row = slice(l, l + 1)

    @pl.when(t == 0)
    def _():
        ext_ref[PAD - (CONV_W - 1):PAD, :] = cs_ref[0]
        hc_ref[...] = h0_ref[0]

    x = x_ref[0]
    xc = _conv_block(ext_ref, x, cw_ref, tt) + cb_ref[row, :]
    r = jax.nn.sigmoid(_mm(xc, wa_ref[...]) + ba_ref[row, :])
    i = jax.nn.sigmoid(_mm(xc, wx_ref[...]) + bx_ref[row, :])
    log_a = -LRU_C * r * jax.nn.softplus(-lam_ref[row, :])
    a_scr[...] = jnp.exp(log_a)
    th = jnp.tanh(log_a)
    one_minus_a2 = -2.0 * th / (1.0 - th)
    b_scr[...] = jnp.sqrt(one_minus_a2) * (i * xc)

    def step(s, h):
        row = pl.ds(s, 1)
        h = a_scr[row, :] * h + b_scr[row, :]
        h_scr[row, :] = h
        return h

    hc_ref[...] = lax.fori_loop(0, tt, step, hc_ref[...], unroll=8)
    o_ref[0] = (h_scr[...] * _gelu(y_ref[0])).astype(o_ref.dtype)

    @pl.when(t == n_t - 1)
    def _():
        hn_ref[0] = hc_ref[...]
        cn_ref[0] = x[tt - (CONV_W - 1):, :]


def _rglru(proj3, conv_state, h0, cw, cb, wa, ba, wx, bx, lam, l, *, tt):
    bsz, seq, _ = proj3.shape
    n_t = seq // tt
    col = lambda j: pl.BlockSpec((1, tt, D_GROUP), lambda b, t: (b, t, j))
    vec = pl.BlockSpec((DEPTH, D_GROUP), lambda b, t: (0, 0))
    sq = pl.BlockSpec((None, D_GROUP, D_GROUP), lambda b, t: (l, 0, 0))
    cs = pl.BlockSpec((1, CONV_W - 1, D_GROUP), lambda b, t: (b, 0, 0))
    hs = pl.BlockSpec((1, 1, D_GROUP), lambda b, t: (b, 0, 0))
    return pl.pallas_call(
        functools.partial(_lru_kernel, tt=tt, n_t=n_t, l=l),
        grid=(bsz, n_t),
        in_specs=[col(4), col(5), cs, hs,
                  pl.BlockSpec((None, CONV_W, D_GROUP), lambda b, t: (l, 0, 0)), vec, sq, vec, sq, vec, vec],
        out_specs=[pl.BlockSpec((1, tt, D_GROUP), lambda b, t: (b, t, 0)), hs, cs],
        out_shape=[jax.ShapeDtypeStruct((bsz, seq, D_GROUP), BF16),
                   jax.ShapeDtypeStruct((bsz, 1, D_GROUP), F32),
                   jax.ShapeDtypeStruct((bsz, CONV_W - 1, D_GROUP), F32)],
        scratch_shapes=[pltpu.VMEM((tt + PAD, D_GROUP), F32), pltpu.VMEM((tt, D_GROUP), F32),
                        pltpu.VMEM((tt, D_GROUP), F32), pltpu.VMEM((tt, D_GROUP), F32),
                        pltpu.VMEM((1, D_GROUP), F32)],
        compiler_params=_params("parallel", "arbitrary"),
        name="lru",
    )(proj3, proj3, conv_state, h0, cw, cb, wa, ba, wx, bx, lam)


def _sg_kernel(u_ref, v_ref, lg_ref, lb_ref, ws_ref, bs_ref, o_ref, *vn_ref, c, tt, bb, l):
    u = _gelu(u_ref[...])
    vn = _layer_norm(_gelu(v_ref[...]), lg_ref[l:l + 1, :], lb_ref[l:l + 1, :])
    if vn_ref:
        vn_ref[0][...] = vn
    row = lax.broadcasted_iota(jnp.int32, (c, c), 0)
    col = lax.broadcasted_iota(jnp.int32, (c, c), 1)
    mask = (col // CHUNK) <= (row // CHUNK)
    for h in range(N_HEADS):
        lanes = slice(h * D_HEAD, (h + 1) * D_HEAD)
        w = jnp.where(mask, ws_ref[h], 0.0).astype(BF16)
        bias = bs_ref[:, h:h + 1]
        for b in range(bb):
            for i in range(tt // c):
                rows = slice(i * c, (i + 1) * c)
                s = _mm(w, vn[b, rows, lanes]) + bias
                o_ref[b, rows, lanes] = (u[b, rows, lanes] * s).astype(o_ref.dtype)


def _spatial_gate(proj3, lg, lb, ws, bs_t, l, *, tt, bb, with_vn):
    bsz, seq, _ = proj3.shape
    c = min(SG_CHUNK, seq)
    col = lambda j: pl.BlockSpec((bb, tt, D_GROUP), lambda b, t: (b, t, j))
    vec = pl.BlockSpec((DEPTH, D_GROUP), lambda b, t: (0, 0))
    out = pl.BlockSpec((bb, tt, D_GROUP), lambda b, t: (b, t, 0))
    res = pl.pallas_call(
        functools.partial(_sg_kernel, c=c, tt=tt, bb=bb, l=l),
        grid=(bsz // bb, seq // tt),
        in_specs=[col(6), col(7), vec, vec,
                  pl.BlockSpec((None, N_HEADS, c, c), lambda b, t: (l, 0, 0, 0)),
                  pl.BlockSpec((None, c, N_HEADS), lambda b, t: (l, 0, 0))],
        out_specs=[out, out] if with_vn else [out],
        out_shape=[jax.ShapeDtypeStruct((bsz, seq, D_GROUP), BF16)]
                  + ([jax.ShapeDtypeStruct((bsz, seq, D_GROUP), F32)] if with_vn else []),
        compiler_params=_params("parallel", "parallel"),
        name="sg",
    )(proj3, proj3, lg, lb, ws, bs_t)
    return (res[0], res[1]) if with_vn else (res[0], None)


INV_BLOCK = 16


def _unit_lower_inverse(mats, c):
    row = lax.broadcasted_iota(jnp.int32, (c, c), 0)
    col = lax.broadcasted_iota(jnp.int32, (c, c), 1)
    eye = (row == col).astype(F32)
    same = row // INV_BLOCK == col // INV_BLOCK
    ps = [jnp.where(same, a, 0.0) for a in mats]
    xs = [eye - p for p in ps]
    ps = [_bmm(p, p) for p in ps]
    n_sq = int(math.log2(INV_BLOCK)) - 1
    for step in range(n_sq):
        last = step == n_sq - 1
        lhs = xs if last else [jnp.concatenate([x, p], axis=1) for x, p in zip(xs, ps)]
        prod = [_bmm(l, p) for l, p in zip(lhs, ps)]
        xs = [x + r[:, :c, :] for x, r in zip(xs, prod)]
        if not last:
            ps = [r[:, c:, :] for r in prod]
    size = INV_BLOCK
    while size < c:
        big = 2 * size
        sel = (row // big == col // big) & (row // size != col // size)
        offs = [jnp.where(sel, a, 0.0) for a in mats]
        tmp = [_bmm(o, x) for o, x in zip(offs, xs)]
        xs = [x - _bmm(x, t) for x, t in zip(xs, tmp)]
        size = big
    return xs


def _dn_kernel(q_ref, k_ref, v_ref, z_ref, ab_ref, cw_ref, al_ref, dt_ref, gn_ref, *rest,
               c, n, n_t, bb, has_state, l):
    lrow = slice(l, l + 1)
    if has_state:
        cs_ref, s0_ref = rest[:2]
        rest = rest[2:]
    o_ref, sf_ref, cn_ref, ext_ref, s_scr, u_scr, wq_scr, attn_scr, kdt_scr, egl_scr = rest
    t = pl.program_id(1)
    tt = n * c
    nb = bb * n
    heads = range(N_HEADS)

    @pl.when(t == 0)
    def _():
        for j in range(3):
            hist = (cs_ref[:, :, j * D_GROUP:(j + 1) * D_GROUP] if has_state
                    else jnp.zeros((bb, CONV_W - 1, D_GROUP), F32))
            ext_ref[j, :, PAD - (CONV_W - 1):PAD, :] = hist
        s_scr[...] = s0_ref[...] if has_state else jnp.zeros_like(s_scr)

    raw = (q_ref, k_ref, v_ref)
    qf, kf, vf = [
        jnp.concatenate([_silu(_conv_block(ext_ref.at[j, b], raw[j][b],
                                           cw_ref.at[:, j * D_GROUP:(j + 1) * D_GROUP], tt))
                         for b in range(bb)], axis=0)
        for j in range(3)]

    row = lax.broadcasted_iota(jnp.int32, (c, c), 0)
    col = lax.broadcasted_iota(jnp.int32, (c, c), 1)
    tri = row >= col
    strict = row > col
    tri_f = tri.astype(F32)
    scale = D_HEAD ** -0.5

    ab = ab_ref[...].reshape(bb * tt, 128)
    g_all = -jnp.exp(al_ref[lrow, :]) * jax.nn.softplus(ab + dt_ref[lrow, :])
    beta_all = jax.nn.sigmoid(ab)
    gc_all = jnp.concatenate(
        [jnp.dot(tri_f, g_all[i * c:(i + 1) * c], preferred_element_type=F32, precision=lax.Precision.HIGHEST)
         for i in range(nb)], axis=0)
    gc_t = gc_all.T

    def l2n(x):
        return x * lax.rsqrt(jnp.sum(x * x, -1, keepdims=True) + 1e-6)

    def split(x, h, k0, k1):
        return x[k0 * c:k1 * c, h * D_HEAD:(h + 1) * D_HEAD].reshape(k1 - k0, c, D_HEAD)

    def chunk_local(k0, k1):
        m = k1 - k0
        rows = slice(k0 * c, k1 * c)
        q3 = [l2n(split(qf, h, k0, k1)) * scale for h in heads]
        k3 = [l2n(split(kf, h, k0, k1)) for h in heads]
        v3 = [split(vf, h, k0, k1) for h in heads]
        beta = [beta_all[rows, N_HEADS + h:N_HEADS + h + 1].reshape(m, c, 1) for h in heads]
        gcol = [gc_all[rows, h:h + 1].reshape(m, c, 1) for h in heads]
        grow = [jnp.stack([gc_t[h:h + 1, i * c:(i + 1) * c] for i in range(k0, k1)]) for h in heads]
        glast = [g[:, c - 1:c, :] for g in gcol]
        decay = [jnp.where(tri, jnp.exp(gc - gr), 0.0) for gc, gr in zip(gcol, grow)]
        egc = [jnp.exp(g) for g in gcol]
        kb = [k * b for k, b in zip(k3, beta)]
        vb = [v * b for v, b in zip(v3, beta)]
        a_mat = [jnp.where(strict, _bmm_nt(x, k) * d, 0.0) for x, k, d in zip(kb, k3, decay)]
        tinv = _unit_lower_inverse(a_mat, c)
        uw = [_bmm(ti, jnp.concatenate([x, y * e], axis=-1)) for ti, x, y, e in zip(tinv, vb, kb, egc)]
        attn = [_bmm_nt(q, k) * d for q, k, d in zip(q3, k3, decay)]
        for h in heads:
            u_scr[h, k0:k1] = uw[h][:, :, :D_HEAD]
            wq_scr[h, k0:k1] = jnp.concatenate([uw[h][:, :, D_HEAD:], q3[h] * egc[h]], axis=1).astype(BF16)
            attn_scr[h, k0:k1] = attn[h].astype(BF16)
            kdt_scr[h, k0:k1] = jnp.swapaxes(k3[h] * jnp.exp(glast[h] - gcol[h]), 1, 2).astype(BF16)
            egl_scr[h, k0:k1] = jnp.broadcast_to(jnp.exp(glast[h]), (m, 1, D_HEAD))

    def sequential(k0, k1, state):
        for k in range(k0, k1):
            b, i = divmod(k, n)
            rows = slice(i * c, (i + 1) * c)
            s = [state[b, h] for h in heads]
            r = [_mm(wq_scr[h, k], s[h]) for h in heads]
            v_new = [u_scr[h, k] - r[h][:c] for h in heads]
            for h in heads:
                state[b, h] = s[h] * egl_scr[h, k] + _mm(kdt_scr[h, k], v_new[h])
            o = [r[h][c:] + _mm(attn_scr[h, k], v_new[h]) for h in heads]
            for h in heads:
                lanes = slice(h * D_HEAD, (h + 1) * D_HEAD)
                on = o[h] * lax.rsqrt(jnp.mean(o[h] * o[h], -1, keepdims=True) + EPS) * gn_ref[lrow, :]
                o_ref[b, rows, lanes] = (on * _silu(z_ref[b, rows, lanes])).astype(o_ref.dtype)

    groups = 2 if nb % 2 == 0 else 1
    bounds = [(g * nb // groups, (g + 1) * nb // groups) for g in range(groups)]
    for k0, k1 in bounds:
        chunk_local(k0, k1)
    state = {(b, h): s_scr[b, h] for b in range(bb) for h in heads}
    for k0, k1 in bounds:
        sequential(k0, k1, state)
    for (b, h), s in state.items():
        s_scr[b, h] = s

    @pl.when(t == n_t - 1)
    def _():
        sf_ref[...] = s_scr[...]
        for j in range(3):
            cn_ref[:, :, j * D_GROUP:(j + 1) * D_GROUP] = raw[j][:, tt - (CONV_W - 1):, :]


def _deltanet(proj3, ab3, states, cw, a_log, dt_bias, gn, l, *, tt, bb):
    bsz, seq, _ = proj3.shape
    c = min(CHUNK, seq)
    n_t = seq // tt
    n = tt // c
    nb = bb * n
    col = lambda j: pl.BlockSpec((bb, tt, D_GROUP), lambda b, t: (b, t, j))
    st = pl.BlockSpec((bb, N_HEADS, D_HEAD, D_HEAD), lambda b, t: (b, 0, 0, 0))
    cs = pl.BlockSpec((bb, CONV_W - 1, 3 * D_GROUP), lambda b, t: (b, 0, 0))
    lane_row = pl.BlockSpec((DEPTH, 128), lambda b, t: (0, 0))
    has_state = states is not None
    state_in = list(states) if has_state else []
    state_spec = ([pl.BlockSpec((None, bb, CONV_W - 1, 3 * D_GROUP), lambda b, t: (l, b, 0, 0)),
                   pl.BlockSpec((None, bb, N_HEADS, D_HEAD, D_HEAD), lambda b, t: (l, b, 0, 0, 0))]
                  if has_state else [])
    return pl.pallas_call(
        functools.partial(_dn_kernel, c=c, n=n, n_t=n_t, bb=bb, has_state=has_state, l=l),
        grid=(bsz // bb, n_t),
        in_specs=[col(8), col(9), col(10), col(11),
                  pl.BlockSpec((bb, tt, 128), lambda b, t: (b, t, 0)),
                  pl.BlockSpec((None, CONV_W, 3 * D_GROUP), lambda b, t: (l, 0, 0)),
                  lane_row, lane_row, lane_row] + state_spec,
        out_specs=[pl.BlockSpec((bb, tt, D_GROUP), lambda b, t: (b, t, 0)), st, cs],
        out_shape=[jax.ShapeDtypeStruct((bsz, seq, D_GROUP), BF16),
                   jax.ShapeDtypeStruct((bsz, N_HEADS, D_HEAD, D_HEAD), F32),
                   jax.ShapeDtypeStruct((bsz, CONV_W - 1, 3 * D_GROUP), F32)],
        scratch_shapes=[pltpu.VMEM((3, bb, tt + PAD, D_GROUP), F32),
                        pltpu.VMEM((bb, N_HEADS, D_HEAD, D_HEAD), F32),
                        pltpu.VMEM((N_HEADS, nb, c, D_HEAD), F32),
                        pltpu.VMEM((N_HEADS, nb, 2 * c, D_HEAD), BF16),
                        pltpu.VMEM((N_HEADS, nb, c, c), BF16),
                        pltpu.VMEM((N_HEADS, nb, D_HEAD, c), BF16),
                        pltpu.VMEM((N_HEADS, nb, 1, D_HEAD), F32)],
        compiler_params=_params("parallel", "arbitrary"),
        name="dn",
    )(proj3, proj3, proj3, proj3, ab3, cw, a_log, dt_bias, gn, *state_in)


def _mixout_kernel(a_ref, b_ref, c_ref, d_ref, x_ref, w_ref, g_ref, bb_ref, o_ref, cat_ref, *, l):
    for j, part in enumerate((a_ref, b_ref, c_ref, d_ref)):
        cat_ref[:, j * D_GROUP:(j + 1) * D_GROUP] = part[...]
    n_row = 4 if o_ref.shape[0] >= 1024 else 2
    tr = o_ref.shape[0] // n_row
    for r in range(n_row):
        rows = slice(r * tr, (r + 1) * tr)
        acc = jnp.dot(cat_ref[rows, :], w_ref[...], preferred_element_type=F32)
        o_ref[rows, :] = _layer_norm(ALPHA * x_ref[rows, :] + acc, g_ref[l:l + 1, :], bb_ref[l:l + 1, :])


def _mixout(a, b, c, d, x, w, g, bb, l, *, tm):
    m, dm = x.shape
    part = pl.BlockSpec((tm, D_GROUP), lambda i: (i, 0))
    vec = pl.BlockSpec((DEPTH, dm), lambda i: (0, 0))
    return pl.pallas_call(
        functools.partial(_mixout_kernel, l=l),
        grid=(m // tm,),
        in_specs=[part, part, part, part,
                  pl.BlockSpec((tm, dm), lambda i: (i, 0)),
                  pl.BlockSpec((None, 4 * D_GROUP, dm), lambda i: (l, 0, 0), pipeline_mode=pl.Buffered(1)),
                  vec, vec],
        out_specs=pl.BlockSpec((tm, dm), lambda i: (i, 0)),
        out_shape=jax.ShapeDtypeStruct((m, dm), F32),
        scratch_shapes=[pltpu.VMEM((tm, 4 * D_GROUP), BF16)],
        compiler_params=_params("parallel"),
        name="mixout",
    )(a, b, c, d, x, w, g, bb)


def _block_diag(w):
    dep, h, n, _ = w.shape
    eye = jnp.eye(h, dtype=w.dtype)
    return (eye[None, :, None, :, None] * w[:, :, :, None, :]).reshape(dep, h * n, h * n)


def _prep(p):
    pad128 = lambda v: jnp.pad(v, ((0, 0), (0, 128 - v.shape[-1])))
    q = dict(p)
    q['w_mix_out'] = p['w_mix_out'].astype(BF16)
    w_in_t = jnp.swapaxes(p['w_mix_in'], 1, 2)
    q['w_mix_in'] = w_in_t.astype(BF16)
    q['w_mix_ab'] = jnp.pad(w_in_t[:, 12 * D_GROUP:, :], ((0, 0), (0, 128 - 2 * N_HEADS), (0, 0))).astype(BF16)
    q['lru_w_a'] = _block_diag(p['lru_w_a']).astype(BF16)
    q['lru_w_x'] = _block_diag(p['lru_w_x']).astype(BF16)
    q['dn_a_log'] = pad128(p['dn_a_log'])
    q['dn_dt_bias'] = pad128(p['dn_dt_bias'])
    return q


def _token_mixers(x, st, p, l, s):
    bsz, seq, tm, tt, bb = s['bsz'], s['seq'], s['tm'], s['tt'], s['bb']
    c_sg = min(SG_CHUNK, seq)
    sg_w = p['sg_w'][:, :, :c_sg, :c_sg]
    sg_b_t = jnp.swapaxes(p['sg_b'][:, :, :c_sg], 1, 2)
    proj, ab = _proj(x, p['w_mix_in'], p['w_mix_ab'], l, tm=tm)
    proj3 = proj.reshape(bsz, seq, -1)
    ab3 = ab.reshape(bsz, seq, 128)
    if st is None:
        s_ret = dn_states = None
        s_lru, s_lru_conv = jnp.zeros((bsz, D_GROUP), F32), jnp.zeros((bsz, CONV_W - 1, D_GROUP), F32)
    else:
        s_ret, s_lru, s_lru_conv, s_dn, s_dn_conv = st
        s_lru, s_lru_conv = s_lru[l], s_lru_conv[l]
        dn_states = (s_dn_conv, s_dn)
    out_a, ret_new = _retention(proj3, s_ret, p['ret_norm_g'], s['pos'], l, tt=tt, bb=bb)
    out_b, lru_h_new, lru_conv_new = _rglru(
        proj3, s_lru_conv, s_lru.reshape(bsz, 1, D_GROUP), p['lru_conv_w'], p['lru_conv_b'],
        p['lru_w_a'], p['lru_b_a'], p['lru_w_x'], p['lru_b_x'], p['lru_lam'], l, tt=tt)
    out_c, sg_v = _spatial_gate(proj3, p['sg_ln_g'], p['sg_ln_b'], sg_w, sg_b_t, l,
                                tt=min(seq, 4 * SG_CHUNK), bb=bb, with_vn=s['keep_sg_v'])
    out_d, dn_new, dn_conv_new = _deltanet(proj3, ab3, dn_states, p['dn_conv_w'],
                                           p['dn_a_log'], p['dn_dt_bias'], p['dn_norm_g'], l, tt=tt, bb=bb)
    flat = lambda o: o.reshape(bsz * seq, D_GROUP)
    x = _mixout(flat(out_a), flat(out_b), flat(out_c), flat(out_d), x, p['w_mix_out'],
                p['ln2_g'], p['ln2_b'], l, tm=tm)
    return x, (ret_new, lru_h_new.reshape(bsz, D_GROUP), lru_conv_new, dn_new, dn_conv_new, sg_v)


def kernel(x_prompt, x_sample, state_ret, state_lru_h, state_lru_conv, state_dn, state_dn_conv, ffn1_w_in, ffn1_w_out, ln1_g, ln1_b, w_mix_in, ret_norm_g, lru_conv_w, lru_conv_b, lru_w_a, lru_b_a, lru_w_x, lru_b_x, lru_lam, sg_ln_g, sg_ln_b, sg_w, sg_b, dn_conv_w, dn_a_log, dn_dt_bias, dn_norm_g, w_mix_out, ln2_g, ln2_b, ffn2_w_in, ffn2_w_out, ln3_g, ln3_b):
    weights = dict(ffn1_w_in=ffn1_w_in, ffn1_w_out=ffn1_w_out, ln1_g=ln1_g, ln1_b=ln1_b, w_mix_in=w_mix_in,
                   ret_norm_g=ret_norm_g, lru_conv_w=lru_conv_w, lru_conv_b=lru_conv_b, lru_w_a=lru_w_a,
                   lru_b_a=lru_b_a, lru_w_x=lru_w_x, lru_b_x=lru_b_x, lru_lam=lru_lam, sg_ln_g=sg_ln_g,
                   sg_ln_b=sg_ln_b, sg_w=sg_w, sg_b=sg_b, dn_conv_w=dn_conv_w, dn_a_log=dn_a_log,
                   dn_dt_bias=dn_dt_bias, dn_norm_g=dn_norm_g, w_mix_out=w_mix_out, ln2_g=ln2_g, ln2_b=ln2_b,
                   ffn2_w_in=ffn2_w_in, ffn2_w_out=ffn2_w_out, ln3_g=ln3_g, ln3_b=ln3_b)
    p = _prep(weights)

    bp, lp, dm = x_prompt.shape
    bs, ls, _ = x_sample.shape
    prompt = dict(bsz=bp, seq=lp, tm=1024, tt=min(lp, 512), bb=1, pos=np.arange(lp), keep_sg_v=False)
    sample = dict(bsz=bs, seq=ls, tm=bs * ls, tt=ls, bb=min(bs, 8), pos=PAST_LEN + np.arange(ls), keep_sg_v=True)
    sample_states = (state_ret, state_lru_h, state_lru_conv, state_dn, state_dn_conv)
    xp = x_prompt.reshape(bp * lp, dm)
    xs = x_sample.reshape(bs * ls, dm)

    ffns = [(l, name) for l in range(DEPTH) for name in ('ffn1', 'ffn2')]
    ln_of = {'ffn1': ('ln1_g', 'ln1_b'), 'ffn2': ('ln3_g', 'ln3_b')}

    def ffn_pair(k, xp, xs, wb):
        l, name = ffns[k]
        g, b = p[ln_of[name][0]], p[ln_of[name][1]]
        nxt = None
        if k + 1 < len(ffns):
            ln, nname = ffns[k + 1]
            nxt = (weights[nname + '_w_in'], weights[nname + '_w_out'], ln)
        if wb is None:
            xs, wb = _ffn(xs, (weights[name + '_w_in'], weights[name + '_w_out']), g, b, l, tm=sample['tm'],
                          from_f32=True)
        else:
            xs, _ = _ffn(xs, wb, g, b, l, tm=sample['tm'])
        xp, wb_next = _ffn(xp, wb, g, b, l, tm=prompt['tm'], cast_next=nxt)
        return xp, xs, wb_next

    wb = None

    st_p, st_s = [], []
    for l in range(DEPTH):
        xp, xs, wb = ffn_pair(2 * l, xp, xs, wb)
        xp, new_p = _token_mixers(xp, None, p, l, prompt)
        xs, new_s = _token_mixers(xs, sample_states, p, l, sample)
        xp, xs, wb = ffn_pair(2 * l + 1, xp, xs, wb)
        st_p.append(new_p)
        st_s.append(new_s)

    stack = lambda sts, i: jnp.stack([s[i] for s in sts])
    return (xp.reshape(bp, lp, dm), xs.reshape(bs, ls, dm),
            stack(st_p, 0), stack(st_p, 1), stack(st_p, 2), stack(st_p, 3), stack(st_p, 4),
            stack(st_s, 0), stack(st_s, 1), stack(st_s, 2), stack(st_s, 3), stack(st_s, 4), stack(st_s, 5))
```

```python
import functools
import math

import numpy as np
import jax
import jax.numpy as jnp
from jax import lax
from jax.experimental import pallas as pl
from jax.experimental.pallas import tpu as pltpu

F32 = jnp.float32
BF16 = jnp.bfloat16

DEPTH = 2
PAST_LEN = 4096
CHUNK = 64
D_GROUP = 512
N_HEADS = 4
D_HEAD = D_GROUP // N_HEADS
LRU_C = 8.0
CONV_W = 4
SG_CHUNK = 128
ROPE_BASE = 10000.0
ALPHA = (2.0 * DEPTH) ** 0.25
EPS = 1e-5

FFN_TF = 512
PROJ_TN = 2048
VMEM_LIMIT = 58 * 1024 * 1024


def _params(*sem):
    return pltpu.CompilerParams(dimension_semantics=sem, vmem_limit_bytes=VMEM_LIMIT)


def _mm(a, b):
    return jnp.dot(a.astype(BF16), b.astype(BF16), preferred_element_type=F32)


def _dot_nt(a, b):
    return lax.dot_general(a, b, (((1,), (1,)), ((), ())), preferred_element_type=F32)


def _bmm(a, b):
    return jnp.einsum('nij,njk->nik', a.astype(BF16), b.astype(BF16), preferred_element_type=F32)


def _bmm_nt(a, b):
    return jnp.einsum('nid,njd->nij', a.astype(BF16), b.astype(BF16), preferred_element_type=F32)


def _layer_norm(y, g, b):
    mu = jnp.mean(y, -1, keepdims=True)
    yc = y - mu
    var = jnp.mean(yc * yc, -1, keepdims=True)
    return yc * lax.rsqrt(var + EPS) * g + b


def _silu(x):
    return x * jax.nn.sigmoid(x)


def _gelu(x):
    return jax.nn.gelu(x, approximate=True)


def _ffn_kernel(x_hbm, wg_ref, wu_ref, wo_ref, g_ref, b_ref, *rest, nf, ni, n_row, mode, l):
    *rest, xb_ref, xf_ref, x_sem = rest
    if mode == 'cast_next':
        ng_ref, nu_ref, no_ref, o_ref, ngo_ref, nuo_ref, noo_ref = rest
        ngo_ref[...] = ng_ref[...].astype(BF16)
        nuo_ref[...] = nu_ref[...].astype(BF16)
        noo_ref[...] = no_ref[...].astype(BF16)
    elif mode == 'from_f32':
        o_ref, wgo_ref, wuo_ref, woo_ref = rest
    else:
        (o_ref,) = rest
    i = pl.program_id(0)
    f = pl.program_id(1)
    tm = xf_ref.shape[0]

    def x_copy(k):
        return pltpu.make_async_copy(x_hbm.at[pl.ds(k * tm, tm), :], xf_ref, x_sem)

    @pl.when(f == 0)
    def _():
        @pl.when(i == 0)
        def _():
            x_copy(0).start()

        x_copy(i).wait()
        x = xf_ref[...]
        xb_ref[...] = x.astype(BF16)
        o_ref[...] = (2.0 * ALPHA) * x

    @pl.when((f == 1) & (i + 1 < ni))
    def _():
        x_copy(i + 1).start()

    wg, wu, wo = wg_ref[...], wu_ref[...], wo_ref[...]
    if mode == 'from_f32':
        wg, wu, wo = wg.astype(BF16), wu.astype(BF16), wo.astype(BF16)
        wgo_ref[...] = wg
        wuo_ref[...] = wu
        woo_ref[...] = wo
    tr = o_ref.shape[0] // n_row

    for r in range(n_row):
        rows = slice(r * tr, (r + 1) * tr)
        xb = xb_ref[rows, :]
        gate = jnp.dot(xb, wg, preferred_element_type=F32)
        up = jnp.dot(xb, wu, preferred_element_type=F32)
        h = (_silu(gate) * up).astype(BF16)
        o_ref[rows, :] += jnp.dot(h, wo, preferred_element_type=F32)

    @pl.when(f == nf - 1)
    def _():
        o_ref[...] = _layer_norm(0.5 * o_ref[...], g_ref[l:l + 1, :], b_ref[l:l + 1, :])


def _ffn(x, w, g, b, l, *, tm, tf=FFN_TF, cast_next=None, from_f32=False):
    m, d = x.shape
    d_ff = w[1].shape[1] if from_f32 else w[2].shape[0]
    nf = d_ff // tf
    assert nf * tf == d_ff
    ni = m // tm
    vec = pl.BlockSpec((DEPTH, d), lambda i, f: (0, 0))
    assert nf >= 2 or ni == 1
    x_spec = pl.BlockSpec(memory_space=pl.ANY)
    tile_g = pl.BlockSpec((d, tf), lambda i, f: (0, f))
    tile_o = pl.BlockSpec((tf, d), lambda i, f: (f, 0))
    w_shapes = [jax.ShapeDtypeStruct((d, d_ff), BF16), jax.ShapeDtypeStruct((d, d_ff), BF16),
                jax.ShapeDtypeStruct((d_ff, d), BF16)]
    out_specs = [pl.BlockSpec((tm, d), lambda i, f: (i, 0))]
    out_shape = [jax.ShapeDtypeStruct((m, d), F32)]
    mode = None
    if from_f32:
        assert ni == 1 and cast_next is None
        mode = 'from_f32'
        w_in, w_out = w
        in_specs = [x_spec,
                    pl.BlockSpec((None, d, tf), lambda i, f: (l, 0, f)),
                    pl.BlockSpec((None, d, tf), lambda i, f: (l, 0, f + nf)),
                    pl.BlockSpec((None, tf, d), lambda i, f: (l, f, 0)), vec, vec]
        args = [x, w_in, w_in, w_out, g, b]
        out_specs += [tile_g, tile_g, tile_o]
        out_shape += w_shapes
    else:
        in_specs = [x_spec, tile_g, tile_g, tile_o, vec, vec]
        args = [x, w[0], w[1], w[2], g, b]
    if cast_next is not None:
        mode = 'cast_next'
        nw_in, nw_out, ln = cast_next
        ri, ro = d // ni, d_ff // (ni * nf)
        assert ri * ni == d and ro * ni * nf == d_ff
        in_specs += [pl.BlockSpec((None, ri, tf), lambda i, f: (ln, i, f)),
                     pl.BlockSpec((None, ri, tf), lambda i, f: (ln, i, f + nf)),
                     pl.BlockSpec((None, ro, d), lambda i, f: (ln, i * nf + f, 0))]
        args += [nw_in, nw_in, nw_out]
        out_specs += [pl.BlockSpec((ri, tf), lambda i, f: (i, f)), pl.BlockSpec((ri, tf), lambda i, f: (i, f)),
                      pl.BlockSpec((ro, d), lambda i, f: (i * nf + f, 0))]
        out_shape += w_shapes
    res = pl.pallas_call(
        functools.partial(_ffn_kernel, nf=nf, ni=ni, n_row=2 if tm >= 1024 else 1, mode=mode, l=l),
        grid=(ni, nf),
        in_specs=in_specs,
        out_specs=out_specs,
        out_shape=out_shape,
        scratch_shapes=[pltpu.VMEM((tm, d), BF16), pltpu.VMEM((tm, d), F32), pltpu.SemaphoreType.DMA(())],
        compiler_params=_params("arbitrary", "arbitrary"),
        name="ffn",
    )(*args)
    return res[0], (tuple(res[1:]) if mode is not None else None)


def _proj_kernel(x_ref, w_ref, wab_ref, o_ref, ab_ref, xb_ref):
    j = pl.program_id(1)

    @pl.when(j == 0)
    def _():
        xb = x_ref[...].astype(BF16)
        xb_ref[...] = xb
        ab_ref[...] = _dot_nt(xb, wab_ref[...])

    o_ref[...] = _dot_nt(xb_ref[...], w_ref[...])


def _proj(x, w_t, wab_t, l, *, tm):
    m, d = x.shape
    tn = PROJ_TN
    n = 12 * D_GROUP
    return pl.pallas_call(
        _proj_kernel,
        grid=(m // tm, n // tn),
        in_specs=[
            pl.BlockSpec((tm, d), lambda i, j: (i, 0)),
            pl.BlockSpec((None, tn, d), lambda i, j: (l, j, 0)),
            pl.BlockSpec((None, 128, d), lambda i, j: (l, 0, 0)),
        ],
        out_specs=[
            pl.BlockSpec((tm, tn), lambda i, j: (i, j)),
            pl.BlockSpec((tm, 128), lambda i, j: (i, 0)),
        ],
        out_shape=[jax.ShapeDtypeStruct((m, n), F32), jax.ShapeDtypeStruct((m, 128), F32)],
        scratch_shapes=[pltpu.VMEM((tm, d), BF16)],
        compiler_params=_params("parallel", "arbitrary"),
        name="proj",
    )(x, w_t, wab_t)


def _ret_tables(pos, c):
    half = D_HEAD // 2
    inv = ROPE_BASE ** (-np.arange(half, dtype=np.float64) / half)
    ang = pos.astype(np.float64)[:, None] * inv[None, :]
    cos = np.concatenate([np.cos(ang), np.cos(ang)], -1)
    sin = np.concatenate([-np.sin(ang), np.sin(ang)], -1)
    log_g = np.log1p(-np.exp2(-5.0 - np.arange(N_HEADS, dtype=np.float64)))
    idx = np.arange(c, dtype=np.float64)
    diff = idx[:, None] - idx[None, :]
    scale = D_HEAD ** -0.5
    dmask = np.where(diff >= 0, np.exp(log_g[:, None, None] * np.maximum(diff, 0.0)), 0.0) * scale
    kdec = np.exp(log_g[:, None] * (c - 1.0 - idx)[None, :]) * scale
    qdec = np.exp(log_g[:, None] * (idx + 1.0)[None, :])
    kdec = np.broadcast_to(kdec[:, :, None], (N_HEADS, c, D_HEAD))
    qdec = np.broadcast_to(qdec[:, :, None], (N_HEADS, c, D_HEAD))
    g_c = np.exp(log_g * c)
    as32 = lambda a: jnp.asarray(np.ascontiguousarray(a), F32)
    return as32(cos), as32(sin), as32(dmask), as32(qdec), as32(kdec), [float(v) for v in g_c]


def _ret_kernel(q_ref, k_ref, v_ref, g_ref, cos_ref, sin_ref, dm_ref, qd_ref, kd_ref, gn_ref, *rest,
                c, n, g_c, n_t, bb, has_state, l):
    if has_state:
        s0_ref, o_ref, sf_ref, s_scr, sst_scr = rest
    else:
        o_ref, sf_ref, s_scr, sst_scr = rest
    t = pl.program_id(1)
    heads = range(N_HEADS)
    lanes = [slice(h * D_HEAD, (h + 1) * D_HEAD) for h in heads]
    tt = n * c

    @pl.when(t == 0)
    def _():
        s_scr[...] = s0_ref[...] if has_state else jnp.zeros_like(s_scr)

    cos = jnp.concatenate([cos_ref[...]] * bb, axis=0)
    sin = jnp.concatenate([sin_ref[...]] * bb, axis=0)

    def rows(ref, h):
        return ref[:, :, lanes[h]].reshape(bb * tt, D_HEAD)

    def rot(x):
        return (x * cos + pltpu.roll(x, D_HEAD // 2, 1) * sin).reshape(bb * n, c, D_HEAD)

    q3 = [rot(rows(q_ref, h)) for h in heads]
    k3 = [rot(rows(k_ref, h)) for h in heads]
    v3 = [rows(v_ref, h).reshape(bb * n, c, D_HEAD) for h in heads]
    scores = [_bmm_nt(q3[h], k3[h]) * dm_ref[h] for h in heads]
    o_intra = [_bmm(scores[h], v3[h]) for h in heads]
    kdt = [jnp.swapaxes(k3[h] * kd_ref[h], 1, 2) for h in heads]
    upd = [_bmm(kdt[h], v3[h]) for h in heads]
    for b in range(bb):
        for h in heads:
            s = s_scr[b, h]
            for i in range(n):
                sst_scr[h, b * n + i] = s.astype(BF16)
                s = g_c[h] * s + upd[h][b * n + i]
            s_scr[b, h] = s
    for h in heads:
        o = o_intra[h] + _bmm(q3[h] * qd_ref[h], sst_scr[h])
        mu = jnp.mean(o, -1, keepdims=True)
        oc = o - mu
        var = jnp.mean(oc * oc, -1, keepdims=True)
        on = (oc * lax.rsqrt(var + EPS)).reshape(bb * tt, D_HEAD) * gn_ref[l:l + 1, lanes[h]]
        o_ref[:, :, lanes[h]] = (on * _silu(rows(g_ref, h))).reshape(bb, tt, D_HEAD).astype(o_ref.dtype)

    @pl.when(t == n_t - 1)
    def _():
        sf_ref[...] = s_scr[...]


def _retention(proj3, state0, gn, pos, l, *, tt, bb):
    bsz, seq, _ = proj3.shape
    c = min(CHUNK, seq)
    cos, sin, dmask, qdec, kdec, g_c = _ret_tables(pos, c)
    n_t = seq // tt
    n = tt // c
    col = lambda j: pl.BlockSpec((bb, tt, D_GROUP), lambda b, t: (b, t, j))
    full3 = pl.BlockSpec((N_HEADS, c, c), lambda b, t: (0, 0, 0))
    full3d = pl.BlockSpec((N_HEADS, c, D_HEAD), lambda b, t: (0, 0, 0))
    st = pl.BlockSpec((bb, N_HEADS, D_HEAD, D_HEAD), lambda b, t: (b, 0, 0, 0))
    has_state = state0 is not None
    state_in = [state0] if has_state else []
    state_spec = ([pl.BlockSpec((None, bb, N_HEADS, D_HEAD, D_HEAD), lambda b, t: (l, b, 0, 0, 0))]
                  if has_state else [])
    return pl.pallas_call(
        functools.partial(_ret_kernel, c=c, n=n, g_c=g_c, n_t=n_t, bb=bb, has_state=has_state, l=l),
        grid=(bsz // bb, n_t),
        in_specs=[col(0), col(1), col(2), col(3),
                  pl.BlockSpec((tt, D_HEAD), lambda b, t: (t, 0)),
                  pl.BlockSpec((tt, D_HEAD), lambda b, t: (t, 0)),
                  full3, full3d, full3d,
                  pl.BlockSpec((DEPTH, D_GROUP), lambda b, t: (0, 0))] + state_spec,
        out_specs=[pl.BlockSpec((bb, tt, D_GROUP), lambda b, t: (b, t, 0)), st],
        out_shape=[jax.ShapeDtypeStruct((bsz, seq, D_GROUP), BF16),
                   jax.ShapeDtypeStruct((bsz, N_HEADS, D_HEAD, D_HEAD), F32)],
        scratch_shapes=[pltpu.VMEM((bb, N_HEADS, D_HEAD, D_HEAD), F32),
                        pltpu.VMEM((N_HEADS, bb * n, D_HEAD, D_HEAD), BF16)],
        compiler_params=_params("parallel", "arbitrary"),
        name="ret",
    )(proj3, proj3, proj3, proj3, cos, sin, dmask, qdec, kdec, gn, *state_in)


PAD = 8


def _conv_block(ext_ref, x, w_ref, tt):
    ext_ref[PAD:PAD + tt, :] = x
    out = x * w_ref[CONV_W - 1:CONV_W, :]
    for j in range(1, CONV_W):
        out = out + ext_ref[PAD - j:PAD - j + tt, :] * w_ref[CONV_W - 1 - j:CONV_W - j, :]
    ext_ref[PAD - (CONV_W - 1):PAD, :] = x[tt - (CONV_W - 1):, :]
    return out


def _lru_kernel(y_ref, x_ref, cs_ref, h0_ref, cw_ref, cb_ref, wa_ref, ba_ref, wx_ref, bx_ref, lam_ref,
                o_ref, hn_ref, cn_ref, ext_ref, a_scr, b_scr, h_scr, hc_ref, *, tt, n_t, l):
    t = pl.program_id(1)
    row = slice(l, l + 1)

    @pl.when(t == 0)
    def _():
        ext_ref[PAD - (CONV_W - 1):PAD, :] = cs_ref[0]
        hc_ref[...] = h0_ref[0]

    x = x_ref[0]
    xc = _conv_block(ext_ref, x, cw_ref, tt) + cb_ref[row, :]
    r = jax.nn.sigmoid(_mm(xc, wa_ref[...]) + ba_ref[row, :])
    i = jax.nn.sigmoid(_mm(xc, wx_ref[...]) + bx_ref[row, :])
    log_a = -LRU_C * r * jax.nn.softplus(-lam_ref[row, :])
    a_scr[...] = jnp.exp(log_a)
    th = jnp.tanh(log_a)
    one_minus_a2 = -2.0 * th / (1.0 - th)
    b_scr[...] = jnp.sqrt(one_minus_a2) * (i * xc)

    def step(s, h):
        row = pl.ds(s, 1)
        h = a_scr[row, :] * h + b_scr[row, :]
        h_scr[row, :] = h
        return h

    hc_ref[...] = lax.fori_loop(0, tt, step, hc_ref[...], unroll=8)
    o_ref[0] = (h_scr[...] * _gelu(y_ref[0])).astype(o_ref.dtype)

    @pl.when(t == n_t - 1)
    def _():
        hn_ref[0] = hc_ref[...]
        cn_ref[0] = x[tt - (CONV_W - 1):, :]


def _rglru(proj3, conv_state, h0, cw, cb, wa, ba, wx, bx, lam, l, *, tt):
    bsz, seq, _ = proj3.shape
    n_t = seq // tt
    col = lambda j: pl.BlockSpec((1, tt, D_GROUP), lambda b, t: (b, t, j))
    vec = pl.BlockSpec((DEPTH, D_GROUP), lambda b, t: (0, 0))
    sq = pl.BlockSpec((None, D_GROUP, D_GROUP), lambda b, t: (l, 0, 0))
    cs = pl.BlockSpec((1, CONV_W - 1, D_GROUP), lambda b, t: (b, 0, 0))
    hs = pl.BlockSpec((1, 1, D_GROUP), lambda b, t: (b, 0, 0))
    return pl.pallas_call(
        functools.partial(_lru_kernel, tt=tt, n_t=n_t, l=l),
        grid=(bsz, n_t),
        in_specs=[col(4), col(5), cs, hs,
                  pl.BlockSpec((None, CONV_W, D_GROUP), lambda b, t: (l, 0, 0)), vec, sq, vec, sq, vec, vec],
        out_specs=[pl.BlockSpec((1, tt, D_GROUP), lambda b, t: (b, t, 0)), hs, cs],
        out_shape=[jax.ShapeDtypeStruct((bsz, seq, D_GROUP), BF16),
                   jax.ShapeDtypeStruct((bsz, 1, D_GROUP), F32),
                   jax.ShapeDtypeStruct((bsz, CONV_W - 1, D_GROUP), F32)],
        scratch_shapes=[pltpu.VMEM((tt + PAD, D_GROUP), F32), pltpu.VMEM((tt, D_GROUP), F32),
                        pltpu.VMEM((tt, D_GROUP), F32), pltpu.VMEM((tt, D_GROUP), F32),
                        pltpu.VMEM((1, D_GROUP), F32)],
        compiler_params=_params("parallel", "arbitrary"),
        name="lru",
    )(proj3, proj3, conv_state, h0, cw, cb, wa, ba, wx, bx, lam)


def _sg_kernel(u_ref, v_ref, lg_ref, lb_ref, ws_ref, bs_ref, o_ref, *vn_ref, c, tt, bb, l):
    u = _gelu(u_ref[...])
    vn = _layer_norm(_gelu(v_ref[...]), lg_ref[l:l + 1, :], lb_ref[l:l + 1, :])
    if vn_ref:
        vn_ref[0][...] = vn
    row = lax.broadcasted_iota(jnp.int32, (c, c), 0)
    col = lax.broadcasted_iota(jnp.int32, (c, c), 1)
    mask = (col // CHUNK) <= (row // CHUNK)
    for h in range(N_HEADS):
        lanes = slice(h * D_HEAD, (h + 1) * D_HEAD)
        w = jnp.where(mask, ws_ref[h], 0.0).astype(BF16)
        bias = bs_ref[:, h:h + 1]
        for b in range(bb):
            for i in range(tt // c):
                rows = slice(i * c, (i + 1) * c)
                s = _mm(w, vn[b, rows, lanes]) + bias
                o_ref[b, rows, lanes] = (u[b, rows, lanes] * s).astype(o_ref.dtype)


def _spatial_gate(proj3, lg, lb, ws, bs_t, l, *, tt, bb, with_vn):
    bsz, seq, _ = proj3.shape
    c = min(SG_CHUNK, seq)
    col = lambda j: pl.BlockSpec((bb, tt, D_GROUP), lambda b, t: (b, t, j))
    vec = pl.BlockSpec((DEPTH, D_GROUP), lambda b, t: (0, 0))
    out = pl.BlockSpec((bb, tt, D_GROUP), lambda b, t: (b, t, 0))
    res = pl.pallas_call(
        functools.partial(_sg_kernel, c=c, tt=tt, bb=bb, l=l),
        grid=(bsz // bb, seq // tt),
        in_specs=[col(6), col(7), vec, vec,
                  pl.BlockSpec((None, N_HEADS, c, c), lambda b, t: (l, 0, 0, 0)),
                  pl.BlockSpec((None, c, N_HEADS), lambda b, t: (l, 0, 0))],
        out_specs=[out, out] if with_vn else [out],
        out_shape=[jax.ShapeDtypeStruct((bsz, seq, D_GROUP), BF16)]
                  + ([jax.ShapeDtypeStruct((bsz, seq, D_GROUP), F32)] if with_vn else []),
        compiler_params=_params("parallel", "parallel"),
        name="sg",
    )(proj3, proj3, lg, lb, ws, bs_t)
    return (res[0], res[1]) if with_vn else (res[0], None)


INV_BLOCK = 16


def _unit_lower_inverse(mats, c):
    row = lax.broadcasted_iota(jnp.int32, (c, c), 0)
    col = lax.broadcasted_iota(jnp.int32, (c, c), 1)
    eye = (row == col).astype(F32)
    same = row // INV_BLOCK == col // INV_BLOCK
    ps = [jnp.where(same, a, 0.0) for a in mats]
    xs = [eye - p for p in ps]
    ps = [_bmm(p, p) for p in ps]
    n_sq = int(math.log2(INV_BLOCK)) - 1
    for step in range(n_sq):
        last = step == n_sq - 1
        lhs = xs if last else [jnp.concatenate([x, p], axis=1) for x, p in zip(xs, ps)]
        prod = [_bmm(l, p) for l, p in zip(lhs, ps)]
        xs = [x + r[:, :c, :] for x, r in zip(xs, prod)]
        if not last:
            ps = [r[:, c:, :] for r in prod]
    size = INV_BLOCK
    while size < c:
        big = 2 * size
        sel = (row // big == col // big) & (row // size != col // size)
        offs = [jnp.where(sel, a, 0.0) for a in mats]
        tmp = [_bmm(o, x) for o, x in zip(offs, xs)]
        xs = [x - _bmm(x, t) for x, t in zip(xs, tmp)]
        size = big
    return xs


def _dn_kernel(q_ref, k_ref, v_ref, z_ref, ab_ref, cw_ref, al_ref, dt_ref, gn_ref, *rest,
               c, n, n_t, bb, has_state, l):
    lrow = slice(l, l + 1)
    if has_state:
        cs_ref, s0_ref = rest[:2]
        rest = rest[2:]
    o_ref, sf_ref, cn_ref, ext_ref, s_scr, u_scr, wq_scr, attn_scr, kdt_scr, egl_scr = rest
    t = pl.program_id(1)
    tt = n * c
    nb = bb * n
    heads = range(N_HEADS)

    @pl.when(t == 0)
    def _():
        for j in range(3):
            hist = (cs_ref[:, :, j * D_GROUP:(j + 1) * D_GROUP] if has_state
                    else jnp.zeros((bb, CONV_W - 1, D_GROUP), F32))
            ext_ref[j, :, PAD - (CONV_W - 1):PAD, :] = hist
        s_scr[...] = s0_ref[...] if has_state else jnp.zeros_like(s_scr)

    raw = (q_ref, k_ref, v_ref)
    qf, kf, vf = [
        jnp.concatenate([_silu(_conv_block(ext_ref.at[j, b], raw[j][b],
                                           cw_ref.at[:, j * D_GROUP:(j + 1) * D_GROUP], tt))
                         for b in range(bb)], axis=0)
        for j in range(3)]

    row = lax.broadcasted_iota(jnp.int32, (c, c), 0)
    col = lax.broadcasted_iota(jnp.int32, (c, c), 1)
    tri = row >= col
    strict = row > col
    tri_f = tri.astype(F32)
    scale = D_HEAD ** -0.5

    ab = ab_ref[...].reshape(bb * tt, 128)
    g_all = -jnp.exp(al_ref[lrow, :]) * jax.nn.softplus(ab + dt_ref[lrow, :])
    beta_all = jax.nn.sigmoid(ab)
    gc_all = jnp.concatenate(
        [jnp.dot(tri_f, g_all[i * c:(i + 1) * c], preferred_element_type=F32, precision=lax.Precision.HIGHEST)
         for i in range(nb)], axis=0)
    gc_t = gc_all.T

    def l2n(x):
        return x * lax.rsqrt(jnp.sum(x * x, -1, keepdims=True) + 1e-6)

    def split(x, h, k0, k1):
        return x[k0 * c:k1 * c, h * D_HEAD:(h + 1) * D_HEAD].reshape(k1 - k0, c, D_HEAD)

    def chunk_local(k0, k1):
        m = k1 - k0
        rows = slice(k0 * c, k1 * c)
        q3 = [l2n(split(qf, h, k0, k1)) * scale for h in heads]
        k3 = [l2n(split(kf, h, k0, k1)) for h in heads]
        v3 = [split(vf, h, k0, k1) for h in heads]
        beta = [beta_all[rows, N_HEADS + h:N_HEADS + h + 1].reshape(m, c, 1) for h in heads]
        gcol = [gc_all[rows, h:h + 1].reshape(m, c, 1) for h in heads]
        grow = [jnp.stack([gc_t[h:h + 1, i * c:(i + 1) * c] for i in range(k0, k1)]) for h in heads]
        glast = [g[:, c - 1:c, :] for g in gcol]
        decay = [jnp.where(tri, jnp.exp(gc - gr), 0.0) for gc, gr in zip(gcol, grow)]
        egc = [jnp.exp(g) for g in gcol]
        kb = [k * b for k, b in zip(k3, beta)]
        vb = [v * b for v, b in zip(v3, beta)]
        a_mat = [jnp.where(strict, _bmm_nt(x, k) * d, 0.0) for x, k, d in zip(kb, k3, decay)]
        tinv = _unit_lower_inverse(a_mat, c)
        uw = [_bmm(ti, jnp.concatenate([x, y * e], axis=-1)) for ti, x, y, e in zip(tinv, vb, kb, egc)]
        attn = [_bmm_nt(q, k) * d for q, k, d in zip(q3, k3, decay)]
        for h in heads:
            u_scr[h, k0:k1] = uw[h][:, :, :D_HEAD]
            wq_scr[h, k0:k1] = jnp.concatenate([uw[h][:, :, D_HEAD:], q3[h] * egc[h]], axis=1).astype(BF16)
            attn_scr[h, k0:k1] = attn[h].astype(BF16)
            kdt_scr[h, k0:k1] = jnp.swapaxes(k3[h] * jnp.exp(glast[h] - gcol[h]), 1, 2).astype(BF16)
            egl_scr[h, k0:k1] = jnp.broadcast_to(jnp.exp(glast[h]), (m, 1, D_HEAD))

    def sequential(k0, k1, state):
        for k in range(k0, k1):
            b, i = divmod(k, n)
            rows = slice(i * c, (i + 1) * c)
            s = [state[b, h] for h in heads]
            r = [_mm(wq_scr[h, k], s[h]) for h in heads]
            v_new = [u_scr[h, k] - r[h][:c] for h in heads]
            for h in heads:
                state[b, h] = s[h] * egl_scr[h, k] + _mm(kdt_scr[h, k], v_new[h])
            o = [r[h][c:] + _mm(attn_scr[h, k], v_new[h]) for h in heads]
            for h in heads:
                lanes = slice(h * D_HEAD, (h + 1) * D_HEAD)
                on = o[h] * lax.rsqrt(jnp.mean(o[h] * o[h], -1, keepdims=True) + EPS) * gn_ref[lrow, :]
                o_ref[b, rows, lanes] = (on * _silu(z_ref[b, rows, lanes])).astype(o_ref.dtype)

    groups = 2 if (n > 1 and nb % 2 == 0) else 1
    bounds = [(g * nb // groups, (g + 1) * nb // groups) for g in range(groups)]
    for k0, k1 in bounds:
        chunk_local(k0, k1)
    state = {(b, h): s_scr[b, h] for b in range(bb) for h in heads}
    for k0, k1 in bounds:
        sequential(k0, k1, state)
    for (b, h), s in state.items():
        s_scr[b, h] = s

    @pl.when(t == n_t - 1)
    def _():
        sf_ref[...] = s_scr[...]
        for j in range(3):
            cn_ref[:, :, j * D_GROUP:(j + 1) * D_GROUP] = raw[j][:, tt - (CONV_W - 1):, :]


def _deltanet(proj3, ab3, states, cw, a_log, dt_bias, gn, l, *, tt, bb):
    bsz, seq, _ = proj3.shape
    c = min(CHUNK, seq)
    n_t = seq // tt
    n = tt // c
    nb = bb * n
    col = lambda j: pl.BlockSpec((bb, tt, D_GROUP), lambda b, t: (b, t, j))
    st = pl.BlockSpec((bb, N_HEADS, D_HEAD, D_HEAD), lambda b, t: (b, 0, 0, 0))
    cs = pl.BlockSpec((bb, CONV_W - 1, 3 * D_GROUP), lambda b, t: (b, 0, 0))
    lane_row = pl.BlockSpec((DEPTH, 128), lambda b, t: (0, 0))
    has_state = states is not None
    state_in = list(states) if has_state else []
    state_spec = ([pl.BlockSpec((None, bb, CONV_W - 1, 3 * D_GROUP), lambda b, t: (l, b, 0, 0)),
                   pl.BlockSpec((None, bb, N_HEADS, D_HEAD, D_HEAD), lambda b, t: (l, b, 0, 0, 0))]
                  if has_state else [])
    return pl.pallas_call(
        functools.partial(_dn_kernel, c=c, n=n, n_t=n_t, bb=bb, has_state=has_state, l=l),
        grid=(bsz // bb, n_t),
        in_specs=[col(8), col(9), col(10), col(11),
                  pl.BlockSpec((bb, tt, 128), lambda b, t: (b, t, 0)),
                  pl.BlockSpec((None, CONV_W, 3 * D_GROUP), lambda b, t: (l, 0, 0)),
                  lane_row, lane_row, lane_row] + state_spec,
        out_specs=[pl.BlockSpec((bb, tt, D_GROUP), lambda b, t: (b, t, 0)), st, cs],
        out_shape=[jax.ShapeDtypeStruct((bsz, seq, D_GROUP), BF16),
                   jax.ShapeDtypeStruct((bsz, N_HEADS, D_HEAD, D_HEAD), F32),
                   jax.ShapeDtypeStruct((bsz, CONV_W - 1, 3 * D_GROUP), F32)],
        scratch_shapes=[pltpu.VMEM((3, bb, tt + PAD, D_GROUP), F32),
                        pltpu.VMEM((bb, N_HEADS, D_HEAD, D_HEAD), F32),
                        pltpu.VMEM((N_HEADS, nb, c, D_HEAD), F32),
                        pltpu.VMEM((N_HEADS, nb, 2 * c, D_HEAD), BF16),
                        pltpu.VMEM((N_HEADS, nb, c, c), BF16),
                        pltpu.VMEM((N_HEADS, nb, D_HEAD, c), BF16),
                        pltpu.VMEM((N_HEADS, nb, 1, D_HEAD), F32)],
        compiler_params=_params("parallel", "arbitrary"),
        name="dn",
    )(proj3, proj3, proj3, proj3, ab3, cw, a_log, dt_bias, gn, *state_in)


def _mixout_kernel(a_ref, b_ref, c_ref, d_ref, x_ref, w_ref, g_ref, bb_ref, o_ref, cat_ref, *, l):
    for j, part in enumerate((a_ref, b_ref, c_ref, d_ref)):
        cat_ref[:, j * D_GROUP:(j + 1) * D_GROUP] = part[...]
    n_row = 4 if o_ref.shape[0] >= 1024 else 2
    tr = o_ref.shape[0] // n_row
    for r in range(n_row):
        rows = slice(r * tr, (r + 1) * tr)
        acc = jnp.dot(cat_ref[rows, :], w_ref[...], preferred_element_type=F32)
        o_ref[rows, :] = _layer_norm(ALPHA * x_ref[rows, :] + acc, g_ref[l:l + 1, :], bb_ref[l:l + 1, :])


def _mixout(a, b, c, d, x, w, g, bb, l, *, tm):
    m, dm = x.shape
    part = pl.BlockSpec((tm, D_GROUP), lambda i: (i, 0))
    vec = pl.BlockSpec((DEPTH, dm), lambda i: (0, 0))
    return pl.pallas_call(
        functools.partial(_mixout_kernel, l=l),
        grid=(m // tm,),
        in_specs=[part, part, part, part,
                  pl.BlockSpec((tm, dm), lambda i: (i, 0)),
                  pl.BlockSpec((None, 4 * D_GROUP, dm), lambda i: (l, 0, 0), pipeline_mode=pl.Buffered(1)),
                  vec, vec],
        out_specs=pl.BlockSpec((tm, dm), lambda i: (i, 0)),
        out_shape=jax.ShapeDtypeStruct((m, dm), F32),
        scratch_shapes=[pltpu.VMEM((tm, 4 * D_GROUP), BF16)],
        compiler_params=_params("parallel"),
        name="mixout",
    )(a, b, c, d, x, w, g, bb)


def _block_diag(w):
    dep, h, n, _ = w.shape
    eye = jnp.eye(h, dtype=w.dtype)
    return (eye[None, :, None, :, None] * w[:, :, :, None, :]).reshape(dep, h * n, h * n)


def _prep(p):
    pad128 = lambda v: jnp.pad(v, ((0, 0), (0, 128 - v.shape[-1])))
    q = dict(p)
    q['w_mix_out'] = p['w_mix_out'].astype(BF16)
    w_in_t = jnp.swapaxes(p['w_mix_in'], 1, 2)
    q['w_mix_in'] = w_in_t.astype(BF16)
    q['w_mix_ab'] = jnp.pad(w_in_t[:, 12 * D_GROUP:, :], ((0, 0), (0, 128 - 2 * N_HEADS), (0, 0))).astype(BF16)
    q['lru_w_a'] = _block_diag(p['lru_w_a']).astype(BF16)
    q['lru_w_x'] = _block_diag(p['lru_w_x']).astype(BF16)
    q['dn_a_log'] = pad128(p['dn_a_log'])
    q['dn_dt_bias'] = pad128(p['dn_dt_bias'])
    return q


def _token_mixers(x, st, p, l, s):
    bsz, seq, tm, tt, bb = s['bsz'], s['seq'], s['tm'], s['tt'], s['bb']
    c_sg = min(SG_CHUNK, seq)
    sg_w = p['sg_w'][:, :, :c_sg, :c_sg]
    sg_b_t = jnp.swapaxes(p['sg_b'][:, :, :c_sg], 1, 2)
    proj, ab = _proj(x, p['w_mix_in'], p['w_mix_ab'], l, tm=tm)
    proj3 = proj.reshape(bsz, seq, -1)
    ab3 = ab.reshape(bsz, seq, 128)
    if st is None:
        s_ret = dn_states = None
        s_lru, s_lru_conv = jnp.zeros((bsz, D_GROUP), F32), jnp.zeros((bsz, CONV_W - 1, D_GROUP), F32)
    else:
        s_ret, s_lru, s_lru_conv, s_dn, s_dn_conv = st
        s_lru, s_lru_conv = s_lru[l], s_lru_conv[l]
        dn_states = (s_dn_conv, s_dn)
    out_a, ret_new = _retention(proj3, s_ret, p['ret_norm_g'], s['pos'], l, tt=tt, bb=bb)
    out_b, lru_h_new, lru_conv_new = _rglru(
        proj3, s_lru_conv, s_lru.reshape(bsz, 1, D_GROUP), p['lru_conv_w'], p['lru_conv_b'],
        p['lru_w_a'], p['lru_b_a'], p['lru_w_x'], p['lru_b_x'], p['lru_lam'], l, tt=tt)
    out_c, sg_v = _spatial_gate(proj3, p['sg_ln_g'], p['sg_ln_b'], sg_w, sg_b_t, l,
                                tt=min(seq, 4 * SG_CHUNK), bb=bb, with_vn=s['keep_sg_v'])
    out_d, dn_new, dn_conv_new = _deltanet(proj3, ab3, dn_states, p['dn_conv_w'],
                                           p['dn_a_log'], p['dn_dt_bias'], p['dn_norm_g'], l, tt=tt, bb=bb)
    flat = lambda o: o.reshape(bsz * seq, D_GROUP)
    x = _mixout(flat(out_a), flat(out_b), flat(out_c), flat(out_d), x, p['w_mix_out'],
                p['ln2_g'], p['ln2_b'], l, tm=tm)
    return x, (ret_new, lru_h_new.reshape(bsz, D_GROUP), lru_conv_new, dn_new, dn_conv_new, sg_v)


def kernel(x_prompt, x_sample, state_ret, state_lru_h, state_lru_conv, state_dn, state_dn_conv, ffn1_w_in, ffn1_w_out, ln1_g, ln1_b, w_mix_in, ret_norm_g, lru_conv_w, lru_conv_b, lru_w_a, lru_b_a, lru_w_x, lru_b_x, lru_lam, sg_ln_g, sg_ln_b, sg_w, sg_b, dn_conv_w, dn_a_log, dn_dt_bias, dn_norm_g, w_mix_out, ln2_g, ln2_b, ffn2_w_in, ffn2_w_out, ln3_g, ln3_b):
    weights = dict(ffn1_w_in=ffn1_w_in, ffn1_w_out=ffn1_w_out, ln1_g=ln1_g, ln1_b=ln1_b, w_mix_in=w_mix_in,
                   ret_norm_g=ret_norm_g, lru_conv_w=lru_conv_w, lru_conv_b=lru_conv_b, lru_w_a=lru_w_a,
                   lru_b_a=lru_b_a, lru_w_x=lru_w_x, lru_b_x=lru_b_x, lru_lam=lru_lam, sg_ln_g=sg_ln_g,
                   sg_ln_b=sg_ln_b, sg_w=sg_w, sg_b=sg_b, dn_conv_w=dn_conv_w, dn_a_log=dn_a_log,
                   dn_dt_bias=dn_dt_bias, dn_norm_g=dn_norm_g, w_mix_out=w_mix_out, ln2_g=ln2_g, ln2_b=ln2_b,
                   ffn2_w_in=ffn2_w_in, ffn2_w_out=ffn2_w_out, ln3_g=ln3_g, ln3_b=ln3_b)
    p = _prep(weights)

    bp, lp, dm = x_prompt.shape
    bs, ls, _ = x_sample.shape
    prompt = dict(bsz=bp, seq=lp, tm=1024, tt=min(lp, 512), bb=1, pos=np.arange(lp), keep_sg_v=False)
    sample = dict(bsz=bs, seq=ls, tm=bs * ls, tt=ls, bb=min(bs, 8), pos=PAST_LEN + np.arange(ls), keep_sg_v=True)
    sample_states = (state_ret, state_lru_h, state_lru_conv, state_dn, state_dn_conv)
    xp = x_prompt.reshape(bp * lp, dm)
    xs = x_sample.reshape(bs * ls, dm)

    ffns = [(l, name) for l in range(DEPTH) for name in ('ffn1', 'ffn2')]
    ln_of = {'ffn1': ('ln1_g', 'ln1_b'), 'ffn2': ('ln3_g', 'ln3_b')}

    def ffn_pair(k, xp, xs, wb):
        l, name = ffns[k]
        g, b = p[ln_of[name][0]], p[ln_of[name][1]]
        nxt = None
        if k + 1 < len(ffns):
            ln, nname = ffns[k + 1]
            nxt = (weights[nname + '_w_in'], weights[nname + '_w_out'], ln)
        if wb is None:
            xs, wb = _ffn(xs, (weights[name + '_w_in'], weights[name + '_w_out']), g, b, l, tm=sample['tm'],
                          from_f32=True)
        else:
            xs, _ = _ffn(xs, wb, g, b, l, tm=sample['tm'])
        xp, wb_next = _ffn(xp, wb, g, b, l, tm=prompt['tm'], cast_next=nxt)
        return xp, xs, wb_next

    wb = None

    st_p, st_s = [], []
    for l in range(DEPTH):
        xp, xs, wb = ffn_pair(2 * l, xp, xs, wb)
        xp, new_p = _token_mixers(xp, None, p, l, prompt)
        xs, new_s = _token_mixers(xs, sample_states, p, l, sample)
        xp, xs, wb = ffn_pair(2 * l + 1, xp, xs, wb)
        st_p.append(new_p)
        st_s.append(new_s)

    stack = lambda sts, i: jnp.stack([s[i] for s in sts])
    return (xp.reshape(bp, lp, dm), xs.reshape(bs, ls, dm),
            stack(st_p, 0), stack(st_p, 1), stack(st_p, 2), stack(st_p, 3), stack(st_p, 4),
            stack(st_s, 0), stack(st_s, 1), stack(st_s, 2), stack(st_s, 3), stack(st_s, 4), stack(st_s, 5))
```

```python
import functools
import math

import numpy as np
import jax
import jax.numpy as jnp
from jax import lax
from jax.experimental import pallas as pl
from jax.experimental.pallas import tpu as pltpu

F32 = jnp.float32
BF16 = jnp.bfloat16

DEPTH = 2
PAST_LEN = 4096
CHUNK = 64
D_GROUP = 512
N_HEADS = 4
D_HEAD = D_GROUP // N_HEADS
LRU_C = 8.0
CONV_W = 4
SG_CHUNK = 128
ROPE_BASE = 10000.0
ALPHA = (2.0 * DEPTH) ** 0.25
EPS = 1e-5

FFN_TF = 512
PROJ_TN = 2048
VMEM_LIMIT = 58 * 1024 * 1024
VMEM_LIMIT_FFN = 61 * 1024 * 1024


def _params(*sem, vmem_limit=VMEM_LIMIT):
    return pltpu.CompilerParams(dimension_semantics=sem, vmem_limit_bytes=vmem_limit)


def _mm(a, b):
    return jnp.dot(a.astype(BF16), b.astype(BF16), preferred_element_type=F32)


def _dot_nt(a, b):
    return lax.dot_general(a, b, (((1,), (1,)), ((), ())), preferred_element_type=F32)


def _bmm(a, b):
    return jnp.einsum('nij,njk->nik', a.astype(BF16), b.astype(BF16), preferred_element_type=F32)


def _bmm_nt(a, b):
    return jnp.einsum('nid,njd->nij', a.astype(BF16), b.astype(BF16), preferred_element_type=F32)


def _layer_norm(y, g, b):
    mu = jnp.mean(y, -1, keepdims=True)
    yc = y - mu
    var = jnp.mean(yc * yc, -1, keepdims=True)
    return yc * lax.rsqrt(var + EPS) * g + b


def _silu(x):
    return x * jax.nn.sigmoid(x)


def _gelu(x):
    return jax.nn.gelu(x, approximate=True)


def _ffn_kernel(x_hbm, wg_ref, wu_ref, wo_ref, g_ref, b_ref, *rest, nf, ni, n_row, mode, l):
    *rest, xb_ref, xf_ref, x_sem = rest
    if mode == 'cast_next':
        ng_ref, nu_ref, no_ref, o_ref, ngo_ref, nuo_ref, noo_ref = rest
        ngo_ref[...] = ng_ref[...].astype(BF16)
        nuo_ref[...] = nu_ref[...].astype(BF16)
        noo_ref[...] = no_ref[...].astype(BF16)
    elif mode == 'from_f32':
        o_ref, wgo_ref, wuo_ref, woo_ref = rest
    else:
        (o_ref,) = rest
    i = pl.program_id(0)
    f = pl.program_id(1)
    tm = xf_ref.shape[0]

    def x_copy(k):
        return pltpu.make_async_copy(x_hbm.at[pl.ds(k * tm, tm), :], xf_ref, x_sem)

    @pl.when(f == 0)
    def _():
        @pl.when(i == 0)
        def _():
            x_copy(0).start()

        x_copy(i).wait()
        x = xf_ref[...]
        xb_ref[...] = x.astype(BF16)
        o_ref[...] = (2.0 * ALPHA) * x

    @pl.when((f == 1) & (i + 1 < ni))
    def _():
        x_copy(i + 1).start()

    wg, wu, wo = wg_ref[...], wu_ref[...], wo_ref[...]
    if mode == 'from_f32':
        wg, wu, wo = wg.astype(BF16), wu.astype(BF16), wo.astype(BF16)
        wgo_ref[...] = wg
        wuo_ref[...] = wu
        woo_ref[...] = wo
    tr = o_ref.shape[0] // n_row

    def step(last):
        for r in range(n_row):
            rows = slice(r * tr, (r + 1) * tr)
            xb = xb_ref[rows, :]
            gate = jnp.dot(xb, wg, preferred_element_type=F32)
            up = jnp.dot(xb, wu, preferred_element_type=F32)
            h = (_silu(gate) * up).astype(BF16)
            acc = o_ref[rows, :] + jnp.dot(h, wo, preferred_element_type=F32)
            o_ref[rows, :] = _layer_norm(0.5 * acc, g_ref[l:l + 1, :], b_ref[l:l + 1, :]) if last else acc

    @pl.when(f < nf - 1)
    def _():
        step(False)

    @pl.when(f == nf - 1)
    def _():
        step(True)


def _ffn(x, w, g, b, l, *, tm, tf=FFN_TF, cast_next=None, from_f32=False):
    m, d = x.shape
    d_ff = w[1].shape[1] if from_f32 else w[2].shape[0]
    nf = d_ff // tf
    assert nf * tf == d_ff
    ni = m // tm
    vec = pl.BlockSpec((DEPTH, d), lambda i, f: (0, 0))
    assert nf >= 2 or ni == 1
    x_spec = pl.BlockSpec(memory_space=pl.ANY)
    tile_g = pl.BlockSpec((d, tf), lambda i, f: (0, f))
    tile_o = pl.BlockSpec((tf, d), lambda i, f: (f, 0))
    w_shapes = [jax.ShapeDtypeStruct((d, d_ff), BF16), jax.ShapeDtypeStruct((d, d_ff), BF16),
                jax.ShapeDtypeStruct((d_ff, d), BF16)]
    out_specs = [pl.BlockSpec((tm, d), lambda i, f: (i, 0))]
    out_shape = [jax.ShapeDtypeStruct((m, d), F32)]
    mode = None
    if from_f32:
        assert ni == 1 and cast_next is None
        mode = 'from_f32'
        w_in, w_out = w
        in_specs = [x_spec,
                    pl.BlockSpec((None, d, tf), lambda i, f: (l, 0, f)),
                    pl.BlockSpec((None, d, tf), lambda i, f: (l, 0, f + nf)),
                    pl.BlockSpec((None, tf, d), lambda i, f: (l, f, 0)), vec, vec]
        args = [x, w_in, w_in, w_out, g, b]
        out_specs += [tile_g, tile_g, tile_o]
        out_shape += w_shapes
    else:
        in_specs = [x_spec, tile_g, tile_g, tile_o, vec, vec]
        args = [x, w[0], w[1], w[2], g, b]
    if cast_next is not None:
        mode = 'cast_next'
        nw_in, nw_out, ln = cast_next
        ri, ro = d // ni, d_ff // (ni * nf)
        assert ri * ni == d and ro * ni * nf == d_ff
        in_specs += [pl.BlockSpec((None, ri, tf), lambda i, f: (ln, i, f)),
                     pl.BlockSpec((None, ri, tf), lambda i, f: (ln, i, f + nf)),
                     pl.BlockSpec((None, ro, d), lambda i, f: (ln, i * nf + f, 0))]
        args += [nw_in, nw_in, nw_out]
        out_specs += [pl.BlockSpec((ri, tf), lambda i, f: (i, f)), pl.BlockSpec((ri, tf), lambda i, f: (i, f)),
                      pl.BlockSpec((ro, d), lambda i, f: (i * nf + f, 0))]
        out_shape += w_shapes
    res = pl.pallas_call(
        functools.partial(_ffn_kernel, nf=nf, ni=ni, n_row=2 if tm >= 1024 else 1, mode=mode, l=l),
        grid=(ni, nf),
        in_specs=in_specs,
        out_specs=out_specs,
        out_shape=out_shape,
        scratch_shapes=[pltpu.VMEM((tm, d), BF16), pltpu.VMEM((tm, d), F32), pltpu.SemaphoreType.DMA(())],
        compiler_params=_params("arbitrary", "arbitrary", vmem_limit=VMEM_LIMIT_FFN),
        name="ffn",
    )(*args)
    return res[0], (tuple(res[1:]) if mode is not None else None)


def _proj_kernel(x_ref, w_ref, wab_ref, o_ref, ab_ref, xb_ref):
    j = pl.program_id(1)

    @pl.when(j == 0)
    def _():
        xb = x_ref[...].astype(BF16)
        xb_ref[...] = xb
        ab_ref[...] = _dot_nt(xb, wab_ref[...])

    o_ref[...] = _dot_nt(xb_ref[...], w_ref[...])


def _proj(x, w_t, wab_t, l, *, tm):
    m, d = x.shape
    tn = PROJ_TN
    n = 12 * D_GROUP
    return pl.pallas_call(
        _proj_kernel,
        grid=(m // tm, n // tn),
        in_specs=[
            pl.BlockSpec((tm, d), lambda i, j: (i, 0)),
            pl.BlockSpec((None, tn, d), lambda i, j: (l, j, 0)),
            pl.BlockSpec((None, 128, d), lambda i, j: (l, 0, 0)),
        ],
        out_specs=[
            pl.BlockSpec((tm, tn), lambda i, j: (i, j)),
            pl.BlockSpec((tm, 128), lambda i, j: (i, 0)),
        ],
        out_shape=[jax.ShapeDtypeStruct((m, n), F32), jax.ShapeDtypeStruct((m, 128), F32)],
        scratch_shapes=[pltpu.VMEM((tm, d), BF16)],
        compiler_params=_params("parallel", "arbitrary"),
        name="proj",
    )(x, w_t, wab_t)


def _ret_tables(pos, c):
    half = D_HEAD // 2
    inv = ROPE_BASE ** (-np.arange(half, dtype=np.float64) / half)
    ang = pos.astype(np.float64)[:, None] * inv[None, :]
    cos = np.concatenate([np.cos(ang), np.cos(ang)], -1)
    sin = np.concatenate([-np.sin(ang), np.sin(ang)], -1)
    log_g = np.log1p(-np.exp2(-5.0 - np.arange(N_HEADS, dtype=np.float64)))
    idx = np.arange(c, dtype=np.float64)
    diff = idx[:, None] - idx[None, :]
    scale = D_HEAD ** -0.5
    dmask = np.where(diff >= 0, np.exp(log_g[:, None, None] * np.maximum(diff, 0.0)), 0.0) * scale
    kdec = np.exp(log_g[:, None] * (c - 1.0 - idx)[None, :]) * scale
    qdec = np.exp(log_g[:, None] * (idx + 1.0)[None, :])
    kdec = np.broadcast_to(kdec[:, :, None], (N_HEADS, c, D_HEAD))
    qdec = np.broadcast_to(qdec[:, :, None], (N_HEADS, c, D_HEAD))
    g_c = np.exp(log_g * c)
    as32 = lambda a: jnp.asarray(np.ascontiguousarray(a), F32)
    return as32(cos), as32(sin), as32(dmask), as32(qdec), as32(kdec), [float(v) for v in g_c]


def _ret_kernel(q_ref, k_ref, v_ref, g_ref, cos_ref, sin_ref, dm_ref, qd_ref, kd_ref, gn_ref, *rest,
                c, n, g_c, n_t, bb, has_state, l):
    if has_state:
        s0_ref, o_ref, sf_ref, s_scr, sst_scr = rest
    else:
        o_ref, sf_ref, s_scr, sst_scr = rest
    t = pl.program_id(1)
    heads = range(N_HEADS)
    lanes = [slice(h * D_HEAD, (h + 1) * D_HEAD) for h in heads]
    tt = n * c

    @pl.when(t == 0)
    def _():
        s_scr[...] = s0_ref[...] if has_state else jnp.zeros_like(s_scr)

    cos = jnp.concatenate([cos_ref[...]] * bb, axis=0)
    sin = jnp.concatenate([sin_ref[...]] * bb, axis=0)

    def rows(ref, h):
        return ref[:, :, lanes[h]].reshape(bb * tt, D_HEAD)

    def rot(x):
        return (x * cos + pltpu.roll(x, D_HEAD // 2, 1) * sin).reshape(bb * n, c, D_HEAD)

    q3 = [rot(rows(q_ref, h)) for h in heads]
    k3 = [rot(rows(k_ref, h)) for h in heads]
    v3 = [rows(v_ref, h).reshape(bb * n, c, D_HEAD) for h in heads]
    scores = [_bmm_nt(q3[h], k3[h]) * dm_ref[h] for h in heads]
    o_intra = [_bmm(scores[h], v3[h]) for h in heads]
    kdt = [jnp.swapaxes(k3[h] * kd_ref[h], 1, 2) for h in heads]
    upd = [_bmm(kdt[h], v3[h]) for h in heads]
    for b in range(bb):
        for h in heads:
            s = s_scr[b, h]
            for i in range(n):
                sst_scr[h, b * n + i] = s.astype(BF16)
                s = g_c[h] * s + upd[h][b * n + i]
            s_scr[b, h] = s
    for h in heads:
        o = o_intra[h] + _bmm(q3[h] * qd_ref[h], sst_scr[h])
        mu = jnp.mean(o, -1, keepdims=True)
        oc = o - mu
        var = jnp.mean(oc * oc, -1, keepdims=True)
        on = (oc * lax.rsqrt(var + EPS)).reshape(bb * tt, D_HEAD) * gn_ref[l:l + 1, lanes[h]]
        o_ref[:, :, lanes[h]] = (on * _silu(rows(g_ref, h))).reshape(bb, tt, D_HEAD).astype(o_ref.dtype)

    @pl.when(t == n_t - 1)
    def _():
        sf_ref[...] = s_scr[...]


def _retention(proj3, state0, gn, pos, l, *, tt, bb):
    bsz, seq, _ = proj3.shape
    c = min(CHUNK, seq)
    cos, sin, dmask, qdec, kdec, g_c = _ret_tables(pos, c)
    n_t = seq // tt
    n = tt // c
    col = lambda j: pl.BlockSpec((bb, tt, D_GROUP), lambda b, t: (b, t, j))
    full3 = pl.BlockSpec((N_HEADS, c, c), lambda b, t: (0, 0, 0))
    full3d = pl.BlockSpec((N_HEADS, c, D_HEAD), lambda b, t: (0, 0, 0))
    st = pl.BlockSpec((bb, N_HEADS, D_HEAD, D_HEAD), lambda b, t: (b, 0, 0, 0))
    has_state = state0 is not None
    state_in = [state0] if has_state else []
    state_spec = ([pl.BlockSpec((None, bb, N_HEADS, D_HEAD, D_HEAD), lambda b, t: (l, b, 0, 0, 0))]
                  if has_state else [])
    return pl.pallas_call(
        functools.partial(_ret_kernel, c=c, n=n, g_c=g_c, n_t=n_t, bb=bb, has_state=has_state, l=l),
        grid=(bsz // bb, n_t),
        in_specs=[col(0), col(1), col(2), col(3),
                  pl.BlockSpec((tt, D_HEAD), lambda b, t: (t, 0)),
                  pl.BlockSpec((tt, D_HEAD), lambda b, t: (t, 0)),
                  full3, full3d, full3d,
                  pl.BlockSpec((DEPTH, D_GROUP), lambda b, t: (0, 0))] + state_spec,
        out_specs=[pl.BlockSpec((bb, tt, D_GROUP), lambda b, t: (b, t, 0)), st],
        out_shape=[jax.ShapeDtypeStruct((bsz, seq, D_GROUP), BF16),
                   jax.ShapeDtypeStruct((bsz, N_HEADS, D_HEAD, D_HEAD), F32)],
        scratch_shapes=[pltpu.VMEM((bb, N_HEADS, D_HEAD, D_HEAD), F32),
                        pltpu.VMEM((N_HEADS, bb * n, D_HEAD, D_HEAD), BF16)],
        compiler_params=_params("parallel", "arbitrary"),
        name="ret",
    )(proj3, proj3, proj3, proj3, cos, sin, dmask, qdec, kdec, gn, *state_in)


PAD = 8


def _conv_block(ext_ref, x, w_ref, tt):
    ext_ref[PAD:PAD + tt, :] = x
    out = x * w_ref[CONV_W - 1:CONV_W, :]
    for j in range(1, CONV_W):
        out = out + ext_ref[PAD - j:PAD - j + tt, :] * w_ref[CONV_W - 1 - j:CONV_W - j, :]
    ext_ref[PAD - (CONV_W - 1):PAD, :] = x[tt - (CONV_W - 1):, :]
    return out


def _lru_kernel(y_ref, x_ref, cs_ref, h0_ref, cw_ref, cb_ref, wa_ref, ba_ref, wx_ref, bx_ref, lam_ref,
                o_ref, hn_ref, cn_ref, ext_ref, a_scr, b_scr, h_scr, hc_ref, *, tt, n_t, l):
    t = pl.program_id(1)
    row = slice(l, l + 1)

    @pl.when(t == 0)
    def _():
        ext_ref[PAD - (CONV_W - 1):PAD, :] = cs_ref[0]
        hc_ref[...] = h0_ref[0]

    x = x_ref[0]
    xc = _conv_block(ext_ref, x, cw_ref, tt) + cb_ref[row, :]
    r = jax.nn.sigmoid(_mm(xc, wa_ref[...]) + ba_ref[row, :])
    i = jax.nn.sigmoid(_mm(xc, wx_ref[...]) + bx_ref[row, :])
    log_a = -LRU_C * r * jax.nn.softplus(-lam_ref[row, :])
    a_scr[...] = jnp.exp(log_a)
    th = jnp.tanh(log_a)
    one_minus_a2 = -2.0 * th / (1.0 - th)
    b_scr[...] = jnp.sqrt(one_minus_a2) * (i * xc)

    def step(s, h):
        row = pl.ds(s, 1)
        h = a_scr[row, :] * h + b_scr[row, :]
        h_scr[row, :] = h
        return h

    hc_ref[...] = lax.fori_loop(0, tt, step, hc_ref[...], unroll=8)
    o_ref[0] = (h_scr[...] * _gelu(y_ref[0])).astype(o_ref.dtype)

    @pl.when(t == n_t - 1)
    def _():
        hn_ref[0] = hc_ref[...]
        cn_ref[0] = x[tt - (CONV_W - 1):, :]


def _rglru(proj3, conv_state, h0, cw, cb, wa, ba, wx, bx, lam, l, *, tt):
    bsz, seq, _ = proj3.shape
    n_t = seq // tt
    col = lambda j: pl.BlockSpec((1, tt, D_GROUP), lambda b, t: (b, t, j))
    vec = pl.BlockSpec((DEPTH, D_GROUP), lambda b, t: (0, 0))
    sq = pl.BlockSpec((None, D_GROUP, D_GROUP), lambda b, t: (l, 0, 0))
    cs = pl.BlockSpec((1, CONV_W - 1, D_GROUP), lambda b, t: (b, 0, 0))
    hs = pl.BlockSpec((1, 1, D_GROUP), lambda b, t: (b, 0, 0))
    return pl.pallas_call(
        functools.partial(_lru_kernel, tt=tt, n_t=n_t, l=l),
        grid=(bsz, n_t),
        in_specs=[col(4), col(5), cs, hs,
                  pl.BlockSpec((None, CONV_W, D_GROUP), lambda b, t: (l, 0, 0)), vec, sq, vec, sq, vec, vec],
        out_specs=[pl.BlockSpec((1, tt, D_GROUP), lambda b, t: (b, t, 0)), hs, cs],
        out_shape=[jax.ShapeDtypeStruct((bsz, seq, D_GROUP), BF16),
                   jax.ShapeDtypeStruct((bsz, 1, D_GROUP), F32),
                   jax.ShapeDtypeStruct((bsz, CONV_W - 1, D_GROUP), F32)],
        scratch_shapes=[pltpu.VMEM((tt + PAD, D_GROUP), F32), pltpu.VMEM((tt, D_GROUP), F32),
                        pltpu.VMEM((tt, D_GROUP), F32), pltpu.VMEM((tt, D_GROUP), F32),
                        pltpu.VMEM((1, D_GROUP), F32)],
        compiler_params=_params("parallel", "arbitrary"),
        name="lru",
    )(proj3, proj3, conv_state, h0, cw, cb, wa, ba, wx, bx, lam)


def _sg_kernel(u_ref, v_ref, lg_ref, lb_ref, ws_ref, bs_ref, o_ref, *vn_ref, c, tt, bb, l):
    u = _gelu(u_ref[...])
    vn = _layer_norm(_gelu(v_ref[...]), lg_ref[l:l + 1, :], lb_ref[l:l + 1, :])
    if vn_ref:
        vn_ref[0][...] = vn
    row = lax.broadcasted_iota(jnp.int32, (c, c), 0)
    col = lax.broadcasted_iota(jnp.int32, (c, c), 1)
    mask = (col // CHUNK) <= (row // CHUNK)
    for h in range(N_HEADS):
        lanes = slice(h * D_HEAD, (h + 1) * D_HEAD)
        w = jnp.where(mask, ws_ref[h], 0.0).astype(BF16)
        bias = bs_ref[:, h:h + 1]
        for b in range(bb):
            for i in range(tt // c):
                rows = slice(i * c, (i + 1) * c)
                s = _mm(w, vn[b, rows, lanes]) + bias
                o_ref[b, rows, lanes] = (u[b, rows, lanes] * s).astype(o_ref.dtype)


def _spatial_gate(proj3, lg, lb, ws, bs_t, l, *, tt, bb, with_vn):
    bsz, seq, _ = proj3.shape
    c = min(SG_CHUNK, seq)
    col = lambda j: pl.BlockSpec((bb, tt, D_GROUP), lambda b, t: (b, t, j))
    vec = pl.BlockSpec((DEPTH, D_GROUP), lambda b, t: (0, 0))
    out = pl.BlockSpec((bb, tt, D_GROUP), lambda b, t: (b, t, 0))
    res = pl.pallas_call(
        functools.partial(_sg_kernel, c=c, tt=tt, bb=bb, l=l),
        grid=(bsz // bb, seq // tt),
        in_specs=[col(6), col(7), vec, vec,
                  pl.BlockSpec((None, N_HEADS, c, c), lambda b, t: (l, 0, 0, 0)),
                  pl.BlockSpec((None, c, N_HEADS), lambda b, t: (l, 0, 0))],
        out_specs=[out, out] if with_vn else [out],
        out_shape=[jax.ShapeDtypeStruct((bsz, seq, D_GROUP), BF16)]
                  + ([jax.ShapeDtypeStruct((bsz, seq, D_GROUP), F32)] if with_vn else []),
        compiler_params=_params("parallel", "parallel"),
        name="sg",
    )(proj3, proj3, lg, lb, ws, bs_t)
    return (res[0], res[1]) if with_vn else (res[0], None)


INV_BLOCK = 16


def _unit_lower_inverse(mats, c):
    row = lax.broadcasted_iota(jnp.int32, (c, c), 0)
    col = lax.broadcasted_iota(jnp.int32, (c, c), 1)
    eye = (row == col).astype(F32)
    same = row // INV_BLOCK == col // INV_BLOCK
    ps = [jnp.where(same, a, 0.0) for a in mats]
    xs = [eye - p for p in ps]
    ps = [_bmm(p, p) for p in ps]
    n_sq = int(math.log2(INV_BLOCK)) - 1
    for step in range(n_sq):
        last = step == n_sq - 1
        lhs = xs if last else [jnp.concatenate([x, p], axis=1) for x, p in zip(xs, ps)]
        prod = [_bmm(l, p) for l, p in zip(lhs, ps)]
        xs = [x + r[:, :c, :] for x, r in zip(xs, prod)]
        if not last:
            ps = [r[:, c:, :] for r in prod]
    size = INV_BLOCK
    while size < c:
        big = 2 * size
        sel = (row // big == col // big) & (row // size != col // size)
        offs = [jnp.where(sel, a, 0.0) for a in mats]
        tmp = [_bmm(o, x) for o, x in zip(offs, xs)]
        xs = [x - _bmm(x, t) for x, t in zip(xs, tmp)]
        size = big
    return xs


def _dn_kernel(q_ref, k_ref, v_ref, z_ref, ab_ref, cw_ref, al_ref, dt_ref, gn_ref, *rest,
               c, n, n_t, bb, has_state, l):
    lrow = slice(l, l + 1)
    if has_state:
        cs_ref, s0_ref = rest[:2]
        rest = rest[2:]
    o_ref, sf_ref, cn_ref, ext_ref, s_scr, u_scr, wq_scr, attn_scr, kdt_scr, egl_scr = rest
    t = pl.program_id(1)
    tt = n * c
    nb = bb * n
    heads = range(N_HEADS)

    @pl.when(t == 0)
    def _():
        for j in range(3):
            hist = (cs_ref[:, :, j * D_GROUP:(j + 1) * D_GROUP] if has_state
                    else jnp.zeros((bb, CONV_W - 1, D_GROUP), F32))
            ext_ref[j, :, PAD - (CONV_W - 1):PAD, :] = hist
        s_scr[...] = s0_ref[...] if has_state else jnp.zeros_like(s_scr)

    raw = (q_ref, k_ref, v_ref)
    qf, kf, vf = [
        jnp.concatenate([_silu(_conv_block(ext_ref.at[j, b], raw[j][b],
                                           cw_ref.at[:, j * D_GROUP:(j + 1) * D_GROUP], tt))
                         for b in range(bb)], axis=0)
        for j in range(3)]

    row = lax.broadcasted_iota(jnp.int32, (c, c), 0)
    col = lax.broadcasted_iota(jnp.int32, (c, c), 1)
    tri = row >= col
    strict = row > col
    tri_f = tri.astype(F32)
    scale = D_HEAD ** -0.5

    ab = ab_ref[...].reshape(bb * tt, 128)
    g_all = -jnp.exp(al_ref[lrow, :]) * jax.nn.softplus(ab + dt_ref[lrow, :])
    beta_all = jax.nn.sigmoid(ab)
    gc_all = jnp.concatenate(
        [jnp.dot(tri_f, g_all[i * c:(i + 1) * c], preferred_element_type=F32, precision=lax.Precision.HIGHEST)
         for i in range(nb)], axis=0)
    gc_t = gc_all.T

    def l2n(x):
        return x * lax.rsqrt(jnp.sum(x * x, -1, keepdims=True) + 1e-6)

    def split(x, h, k0, k1):
        return x[k0 * c:k1 * c, h * D_HEAD:(h + 1) * D_HEAD].reshape(k1 - k0, c, D_HEAD)

    def chunk_local(k0, k1):
        m = k1 - k0
        rows = slice(k0 * c, k1 * c)
        q3 = [l2n(split(qf, h, k0, k1)) * scale for h in heads]
        k3 = [l2n(split(kf, h, k0, k1)) for h in heads]
        v3 = [split(vf, h, k0, k1) for h in heads]
        beta = [beta_all[rows, N_HEADS + h:N_HEADS + h + 1].reshape(m, c, 1) for h in heads]
        gcol = [gc_all[rows, h:h + 1].reshape(m, c, 1) for h in heads]
        grow = [jnp.stack([gc_t[h:h + 1, i * c:(i + 1) * c] for i in range(k0, k1)]) for h in heads]
        glast = [g[:, c - 1:c, :] for g in gcol]
        decay = [jnp.where(tri, jnp.exp(gc - gr), 0.0) for gc, gr in zip(gcol, grow)]
        egc = [jnp.exp(g) for g in gcol]
        kb = [k * b for k, b in zip(k3, beta)]
        vb = [v * b for v, b in zip(v3, beta)]
        a_mat = [jnp.where(strict, _bmm_nt(x, k) * d, 0.0) for x, k, d in zip(kb, k3, decay)]
        tinv = _unit_lower_inverse(a_mat, c)
        uw = [_bmm(ti, jnp.concatenate([x, y * e], axis=-1)) for ti, x, y, e in zip(tinv, vb, kb, egc)]
        attn = [_bmm_nt(q, k) * d for q, k, d in zip(q3, k3, decay)]
        for h in heads:
            u_scr[h, k0:k1] = uw[h][:, :, :D_HEAD]
            wq_scr[h, k0:k1] = jnp.concatenate([uw[h][:, :, D_HEAD:], q3[h] * egc[h]], axis=1).astype(BF16)
            attn_scr[h, k0:k1] = attn[h].astype(BF16)
            kdt_scr[h, k0:k1] = jnp.swapaxes(k3[h] * jnp.exp(glast[h] - gcol[h]), 1, 2).astype(BF16)
            egl_scr[h, k0:k1] = jnp.broadcast_to(jnp.exp(glast[h]), (m, 1, D_HEAD))

    def sequential(k0, k1, state):
        for k in range(k0, k1):
            b, i = divmod(k, n)
            rows = slice(i * c, (i + 1) * c)
            s = [state[b, h] for h in heads]
            r = [_mm(wq_scr[h, k], s[h]) for h in heads]
            v_new = [u_scr[h, k] - r[h][:c] for h in heads]
            for h in heads:
                state[b, h] = s[h] * egl_scr[h, k] + _mm(kdt_scr[h, k], v_new[h])
            o = [r[h][c:] + _mm(attn_scr[h, k], v_new[h]) for h in heads]
            for h in heads:
                lanes = slice(h * D_HEAD, (h + 1) * D_HEAD)
                on = o[h] * lax.rsqrt(jnp.mean(o[h] * o[h], -1, keepdims=True) + EPS) * gn_ref[lrow, :]
                o_ref[b, rows, lanes] = (on * _silu(z_ref[b, rows, lanes])).astype(o_ref.dtype)

    groups = 2 if (n > 1 and nb % 2 == 0) else 1
    bounds = [(g * nb // groups, (g + 1) * nb // groups) for g in range(groups)]
    for k0, k1 in bounds:
        chunk_local(k0, k1)
    state = {(b, h): s_scr[b, h] for b in range(bb) for h in heads}
    for k0, k1 in bounds:
        sequential(k0, k1, state)
    for (b, h), s in state.items():
        s_scr[b, h] = s

    @pl.when(t == n_t - 1)
    def _():
        sf_ref[...] = s_scr[...]
        for j in range(3):
            cn_ref[:, :, j * D_GROUP:(j + 1) * D_GROUP] = raw[j][:, tt - (CONV_W - 1):, :]


def _deltanet(proj3, ab3, states, cw, a_log, dt_bias, gn, l, *, tt, bb):
    bsz, seq, _ = proj3.shape
    c = min(CHUNK, seq)
    n_t = seq // tt
    n = tt // c
    nb = bb * n
    col = lambda j: pl.BlockSpec((bb, tt, D_GROUP), lambda b, t: (b, t, j))
    st = pl.BlockSpec((bb, N_HEADS, D_HEAD, D_HEAD), lambda b, t: (b, 0, 0, 0))
    cs = pl.BlockSpec((bb, CONV_W - 1, 3 * D_GROUP), lambda b, t: (b, 0, 0))
    lane_row = pl.BlockSpec((DEPTH, 128), lambda b, t: (0, 0))
    has_state = states is not None
    state_in = list(states) if has_state else []
    state_spec = ([pl.BlockSpec((None, bb, CONV_W - 1, 3 * D_GROUP), lambda b, t: (l, b, 0, 0)),
                   pl.BlockSpec((None, bb, N_HEADS, D_HEAD, D_HEAD), lambda b, t: (l, b, 0, 0, 0))]
                  if has_state else [])
    return pl.pallas_call(
        functools.partial(_dn_kernel, c=c, n=n, n_t=n_t, bb=bb, has_state=has_state, l=l),
        grid=(bsz // bb, n_t),
        in_specs=[col(8), col(9), col(10), col(11),
                  pl.BlockSpec((bb, tt, 128), lambda b, t: (b, t, 0)),
                  pl.BlockSpec((None, CONV_W, 3 * D_GROUP), lambda b, t: (l, 0, 0)),
                  lane_row, lane_row, lane_row] + state_spec,
        out_specs=[pl.BlockSpec((bb, tt, D_GROUP), lambda b, t: (b, t, 0)), st, cs],
        out_shape=[jax.ShapeDtypeStruct((bsz, seq, D_GROUP), BF16),
                   jax.ShapeDtypeStruct((bsz, N_HEADS, D_HEAD, D_HEAD), F32),
                   jax.ShapeDtypeStruct((bsz, CONV_W - 1, 3 * D_GROUP), F32)],
        scratch_shapes=[pltpu.VMEM((3, bb, tt + PAD, D_GROUP), F32),
                        pltpu.VMEM((bb, N_HEADS, D_HEAD, D_HEAD), F32),
                        pltpu.VMEM((N_HEADS, nb, c, D_HEAD), F32),
                        pltpu.VMEM((N_HEADS, nb, 2 * c, D_HEAD), BF16),
                        pltpu.VMEM((N_HEADS, nb, c, c), BF16),
                        pltpu.VMEM((N_HEADS, nb, D_HEAD, c), BF16),
                        pltpu.VMEM((N_HEADS, nb, 1, D_HEAD), F32)],
        compiler_params=_params("parallel", "arbitrary"),
        name="dn",
    )(proj3, proj3, proj3, proj3, ab3, cw, a_log, dt_bias, gn, *state_in)


def _mixout_kernel(a_ref, b_ref, c_ref, d_ref, x_ref, w_ref, g_ref, bb_ref, o_ref, cat_ref, *, l):
    for j, part in enumerate((a_ref, b_ref, c_ref, d_ref)):
        cat_ref[:, j * D_GROUP:(j + 1) * D_GROUP] = part[...]
    n_row = 4 if o_ref.shape[0] >= 1024 else 2
    tr = o_ref.shape[0] // n_row
    for r in range(n_row):
        rows = slice(r * tr, (r + 1) * tr)
        acc = jnp.dot(cat_ref[rows, :], w_ref[...], preferred_element_type=F32)
        o_ref[rows, :] = _layer_norm(ALPHA * x_ref[rows, :] + acc, g_ref[l:l + 1, :], bb_ref[l:l + 1, :])


def _mixout(a, b, c, d, x, w, g, bb, l, *, tm):
    m, dm = x.shape
    part = pl.BlockSpec((tm, D_GROUP), lambda i: (i, 0))
    vec = pl.BlockSpec((DEPTH, dm), lambda i: (0, 0))
    return pl.pallas_call(
        functools.partial(_mixout_kernel, l=l),
        grid=(m // tm,),
        in_specs=[part, part, part, part,
                  pl.BlockSpec((tm, dm), lambda i: (i, 0)),
                  pl.BlockSpec((None, 4 * D_GROUP, dm), lambda i: (l, 0, 0), pipeline_mode=pl.Buffered(1)),
                  vec, vec],
        out_specs=pl.BlockSpec((tm, dm), lambda i: (i, 0)),
        out_shape=jax.ShapeDtypeStruct((m, dm), F32),
        scratch_shapes=[pltpu.VMEM((tm, 4 * D_GROUP), BF16)],
        compiler_params=_params("parallel"),
        name="mixout",
    )(a, b, c, d, x, w, g, bb)


def _block_diag(w):
    dep, h, n, _ = w.shape
    eye = jnp.eye(h, dtype=w.dtype)
    return (eye[None, :, None, :, None] * w[:, :, :, None, :]).reshape(dep, h * n, h * n)


def _prep(p):
    pad128 = lambda v: jnp.pad(v, ((0, 0), (0, 128 - v.shape[-1])))
    q = dict(p)
    q['w_mix_out'] = p['w_mix_out'].astype(BF16)
    w_in_t = jnp.swapaxes(p['w_mix_in'], 1, 2)
    q['w_mix_in'] = w_in_t.astype(BF16)
    q['w_mix_ab'] = jnp.pad(w_in_t[:, 12 * D_GROUP:, :], ((0, 0), (0, 128 - 2 * N_HEADS), (0, 0))).astype(BF16)
    q['lru_w_a'] = _block_diag(p['lru_w_a']).astype(BF16)
    q['lru_w_x'] = _block_diag(p['lru_w_x']).astype(BF16)
    q['dn_a_log'] = pad128(p['dn_a_log'])
    q['dn_dt_bias'] = pad128(p['dn_dt_bias'])
    return q


def _token_mixers(x, st, p, l, s):
    bsz, seq, tm, tt, bb = s['bsz'], s['seq'], s['tm'], s['tt'], s['bb']
    c_sg = min(SG_CHUNK, seq)
    sg_w = p['sg_w'][:, :, :c_sg, :c_sg]
    sg_b_t = jnp.swapaxes(p['sg_b'][:, :, :c_sg], 1, 2)
    proj, ab = _proj(x, p['w_mix_in'], p['w_mix_ab'], l, tm=tm)
    proj3 = proj.reshape(bsz, seq, -1)
    ab3 = ab.reshape(bsz, seq, 128)
    if st is None:
        s_ret = dn_states = None
        s_lru, s_lru_conv = jnp.zeros((bsz, D_GROUP), F32), jnp.zeros((bsz, CONV_W - 1, D_GROUP), F32)
    else:
        s_ret, s_lru, s_lru_conv, s_dn, s_dn_conv = st
        s_lru, s_lru_conv = s_lru[l], s_lru_conv[l]
        dn_states = (s_dn_conv, s_dn)
    out_a, ret_new = _retention(proj3, s_ret, p['ret_norm_g'], s['pos'], l, tt=tt, bb=bb)
    out_b, lru_h_new, lru_conv_new = _rglru(
        proj3, s_lru_conv, s_lru.reshape(bsz, 1, D_GROUP), p['lru_conv_w'], p['lru_conv_b'],
        p['lru_w_a'], p['lru_b_a'], p['lru_w_x'], p['lru_b_x'], p['lru_lam'], l, tt=tt)
    out_c, sg_v = _spatial_gate(proj3, p['sg_ln_g'], p['sg_ln_b'], sg_w, sg_b_t, l,
                                tt=min(seq, 4 * SG_CHUNK), bb=bb, with_vn=s['keep_sg_v'])
    out_d, dn_new, dn_conv_new = _deltanet(proj3, ab3, dn_states, p['dn_conv_w'],
                                           p['dn_a_log'], p['dn_dt_bias'], p['dn_norm_g'], l, tt=tt, bb=bb)
    flat = lambda o: o.reshape(bsz * seq, D_GROUP)
    x = _mixout(flat(out_a), flat(out_b), flat(out_c), flat(out_d), x, p['w_mix_out'],
                p['ln2_g'], p['ln2_b'], l, tm=tm)
    return x, (ret_new, lru_h_new.reshape(bsz, D_GROUP), lru_conv_new, dn_new, dn_conv_new, sg_v)


def kernel(x_prompt, x_sample, state_ret, state_lru_h, state_lru_conv, state_dn, state_dn_conv, ffn1_w_in, ffn1_w_out, ln1_g, ln1_b, w_mix_in, ret_norm_g, lru_conv_w, lru_conv_b, lru_w_a, lru_b_a, lru_w_x, lru_b_x, lru_lam, sg_ln_g, sg_ln_b, sg_w, sg_b, dn_conv_w, dn_a_log, dn_dt_bias, dn_norm_g, w_mix_out, ln2_g, ln2_b, ffn2_w_in, ffn2_w_out, ln3_g, ln3_b):
    weights = dict(ffn1_w_in=ffn1_w_in, ffn1_w_out=ffn1_w_out, ln1_g=ln1_g, ln1_b=ln1_b, w_mix_in=w_mix_in,
                   ret_norm_g=ret_norm_g, lru_conv_w=lru_conv_w, lru_conv_b=lru_conv_b, lru_w_a=lru_w_a,
                   lru_b_a=lru_b_a, lru_w_x=lru_w_x, lru_b_x=lru_b_x, lru_lam=lru_lam, sg_ln_g=sg_ln_g,
                   sg_ln_b=sg_ln_b, sg_w=sg_w, sg_b=sg_b, dn_conv_w=dn_conv_w, dn_a_log=dn_a_log,
                   dn_dt_bias=dn_dt_bias, dn_norm_g=dn_norm_g, w_mix_out=w_mix_out, ln2_g=ln2_g, ln2_b=ln2_b,
                   ffn2_w_in=ffn2_w_in, ffn2_w_out=ffn2_w_out, ln3_g=ln3_g, ln3_b=ln3_b)
    p = _prep(weights)

    bp, lp, dm = x_prompt.shape
    bs, ls, _ = x_sample.shape
    prompt = dict(bsz=bp, seq=lp, tm=1024, tt=min(lp, 512), bb=1, pos=np.arange(lp), keep_sg_v=False)
    sample = dict(bsz=bs, seq=ls, tm=bs * ls, tt=ls, bb=min(bs, 8), pos=PAST_LEN + np.arange(ls), keep_sg_v=True)
    sample_states = (state_ret, state_lru_h, state_lru_conv, state_dn, state_dn_conv)
    xp = x_prompt.reshape(bp * lp, dm)
    xs = x_sample.reshape(bs * ls, dm)

    ffns = [(l, name) for l in range(DEPTH) for name in ('ffn1', 'ffn2')]
    ln_of = {'ffn1': ('ln1_g', 'ln1_b'), 'ffn2': ('ln3_g', 'ln3_b')}

    def ffn_pair(k, xp, xs, wb):
        l, name = ffns[k]
        g, b = p[ln_of[name][0]], p[ln_of[name][1]]
        nxt = None
        if k + 1 < len(ffns):
            ln, nname = ffns[k + 1]
            nxt = (weights[nname + '_w_in'], weights[nname + '_w_out'], ln)
        if wb is None:
            xs, wb = _ffn(xs, (weights[name + '_w_in'], weights[name + '_w_out']), g, b, l, tm=sample['tm'],
                          from_f32=True)
        else:
            xs, _ = _ffn(xs, wb, g, b, l, tm=sample['tm'])
        xp, wb_next = _ffn(xp, wb, g, b, l, tm=prompt['tm'], cast_next=nxt)
        return xp, xs, wb_next

    wb = None

    st_p, st_s = [], []
    for l in range(DEPTH):
        xp, xs, wb = ffn_pair(2 * l, xp, xs, wb)
        xp, new_p = _token_mixers(xp, None, p, l, prompt)
        xs, new_s = _token_mixers(xs, sample_states, p, l, sample)
        xp, xs, wb = ffn_pair(2 * l + 1, xp, xs, wb)
        st_p.append(new_p)
        st_s.append(new_s)

    stack = lambda sts, i: jnp.stack([s[i] for s in sts])
    return (xp.reshape(bp, lp, dm), xs.reshape(bs, ls, dm),
            stack(st_p, 0), stack(st_p, 1), stack(st_p, 2), stack(st_p, 3), stack(st_p, 4),
            stack(st_s, 0), stack(st_s, 1), stack(st_s, 2), stack(st_s, 3), stack(st_s, 4), stack(st_s, 5))
```

```python
import functools
import math

import numpy as np
import jax
import jax.numpy as jnp
from jax import lax
from jax.experimental import pallas as pl
from jax.experimental.pallas import tpu as pltpu

F32 = jnp.float32
BF16 = jnp.bfloat16

DEPTH = 2
PAST_LEN = 4096
CHUNK = 64
D_GROUP = 512
N_HEADS = 4
D_HEAD = D_GROUP // N_HEADS
LRU_C = 8.0
CONV_W = 4
SG_CHUNK = 128
ROPE_BASE = 10000.0
ALPHA = (2.0 * DEPTH) ** 0.25
EPS = 1e-5

FFN_TF = 512
PROJ_TN = 2048
VMEM_LIMIT = 58 * 1024 * 1024


def _params(*sem):
    return pltpu.CompilerParams(dimension_semantics=sem, vmem_limit_bytes=VMEM_LIMIT)


def _mm(a, b):
    return jnp.dot(a.astype(BF16), b.astype(BF16), preferred_element_type=F32)


def _dot_nt(a, b):
    return lax.dot_general(a, b, (((1,), (1,)), ((), ())), preferred_element_type=F32)


def _bmm(a, b):
    return jnp.einsum('nij,njk->nik', a.astype(BF16), b.astype(BF16), preferred_element_type=F32)


def _bmm_nt(a, b):
    return jnp.einsum('nid,njd->nij', a.astype(BF16), b.astype(BF16), preferred_element_type=F32)


def _layer_norm(y, g, b):
    mu = jnp.mean(y, -1, keepdims=True)
    yc = y - mu
    var = jnp.mean(yc * yc, -1, keepdims=True)
    return yc * lax.rsqrt(var + EPS) * g + b


def _silu(x):
    return x * jax.nn.sigmoid(x)


def _gelu(x):
    return jax.nn.gelu(x, approximate=True)


def _ffn_kernel(x_hbm, wg_ref, wu_ref, wo_ref, g_ref, b_ref, *rest, nf, ni, n_row, mode, l):
    *rest, xb_ref, xf_ref, x_sem = rest
    if mode == 'cast_next':
        ng_ref, nu_ref, no_ref, o_ref, ngo_ref, nuo_ref, noo_ref = rest
        ngo_ref[...] = ng_ref[...].astype(BF16)
        nuo_ref[...] = nu_ref[...].astype(BF16)
        noo_ref[...] = no_ref[...].astype(BF16)
    elif mode == 'from_f32':
        o_ref, wgo_ref, wuo_ref, woo_ref = rest
    else:
        (o_ref,) = rest
    i = pl.program_id(0)
    f = pl.program_id(1)
    tm = xf_ref.shape[0]

    def x_copy(k):
        return pltpu.make_async_copy(x_hbm.at[pl.ds(k * tm, tm), :], xf_ref, x_sem)

    @pl.when(f == 0)
    def _():
        @pl.when(i == 0)
        def _():
            x_copy(0).start()

        x_copy(i).wait()
        x = xf_ref[...]
        xb_ref[...] = x.astype(BF16)
        o_ref[...] = (2.0 * ALPHA) * x

    @pl.when((f == 1) & (i + 1 < ni))
    def _():
        x_copy(i + 1).start()

    wg, wu, wo = wg_ref[...], wu_ref[...], wo_ref[...]
    if mode == 'from_f32':
        wg, wu, wo = wg.astype(BF16), wu.astype(BF16), wo.astype(BF16)
        wgo_ref[...] = wg
        wuo_ref[...] = wu
        woo_ref[...] = wo
    tr = o_ref.shape[0] // n_row

    for r in range(n_row):
        rows = slice(r * tr, (r + 1) * tr)
        xb = xb_ref[rows, :]
        gate = jnp.dot(xb, wg, preferred_element_type=F32)
        up = jnp.dot(xb, wu, preferred_element_type=F32)
        h = (_silu(gate) * up).astype(BF16)
        o_ref[rows, :] += jnp.dot(h, wo, preferred_element_type=F32)

    @pl.when(f == nf - 1)
    def _():
        o_ref[...] = _layer_norm(0.5 * o_ref[...], g_ref[l:l + 1, :], b_ref[l:l + 1, :])


def _ffn(x, w, g, b, l, *, tm, tf=FFN_TF, cast_next=None, from_f32=False):
    m, d = x.shape
    d_ff = w[1].shape[1] if from_f32 else w[2].shape[0]
    nf = d_ff // tf
    assert nf * tf == d_ff
    ni = m // tm
    vec = pl.BlockSpec((DEPTH, d), lambda i, f: (0, 0))
    assert nf >= 2 or ni == 1
    x_spec = pl.BlockSpec(memory_space=pl.ANY)
    tile_g = pl.BlockSpec((d, tf), lambda i, f: (0, f))
    tile_o = pl.BlockSpec((tf, d), lambda i, f: (f, 0))
    w_shapes = [jax.ShapeDtypeStruct((d, d_ff), BF16), jax.ShapeDtypeStruct((d, d_ff), BF16),
                jax.ShapeDtypeStruct((d_ff, d), BF16)]
    out_specs = [pl.BlockSpec((tm, d), lambda i, f: (i, 0))]
    out_shape = [jax.ShapeDtypeStruct((m, d), F32)]
    mode = None
    if from_f32:
        assert ni == 1 and cast_next is None
        mode = 'from_f32'
        w_in, w_out = w
        in_specs = [x_spec,
                    pl.BlockSpec((None, d, tf), lambda i, f: (l, 0, f)),
                    pl.BlockSpec((None, d, tf), lambda i, f: (l, 0, f + nf)),
                    pl.BlockSpec((None, tf, d), lambda i, f: (l, f, 0)), vec, vec]
        args = [x, w_in, w_in, w_out, g, b]
        out_specs += [tile_g, tile_g, tile_o]
        out_shape += w_shapes
    else:
        in_specs = [x_spec, tile_g, tile_g, tile_o, vec, vec]
        args = [x, w[0], w[1], w[2], g, b]
    if cast_next is not None:
        mode = 'cast_next'
        nw_in, nw_out, ln = cast_next
        ri, ro = d // ni, d_ff // (ni * nf)
        assert ri * ni == d and ro * ni * nf == d_ff
        in_specs += [pl.BlockSpec((None, ri, tf), lambda i, f: (ln, i, f)),
                     pl.BlockSpec((None, ri, tf), lambda i, f: (ln, i, f + nf)),
                     pl.BlockSpec((None, ro, d), lambda i, f: (ln, i * nf + f, 0))]
        args += [nw_in, nw_in, nw_out]
        out_specs += [pl.BlockSpec((ri, tf), lambda i, f: (i, f)), pl.BlockSpec((ri, tf), lambda i, f: (i, f)),
                      pl.BlockSpec((ro, d), lambda i, f: (i * nf + f, 0))]
        out_shape += w_shapes
    res = pl.pallas_call(
        functools.partial(_ffn_kernel, nf=nf, ni=ni, n_row=2 if tm >= 1024 else 1, mode=mode, l=l),
        grid=(ni, nf),
        in_specs=in_specs,
        out_specs=out_specs,
        out_shape=out_shape,
        scratch_shapes=[pltpu.VMEM((tm, d), BF16), pltpu.VMEM((tm, d), F32), pltpu.SemaphoreType.DMA(())],
        compiler_params=_params("arbitrary", "arbitrary"),
        name="ffn",
    )(*args)
    return res[0], (tuple(res[1:]) if mode is not None else None)


def _proj_kernel(x_ref, w_ref, wab_ref, o_ref, ab_ref, xb_ref):
    j = pl.program_id(1)

    @pl.when(j == 0)
    def _():
        xb = x_ref[...].astype(BF16)
        xb_ref[...] = xb
        ab_ref[...] = _dot_nt(xb, wab_ref[...])

    o_ref[...] = _dot_nt(xb_ref[...], w_ref[...])


def _proj(x, w_t, wab_t, l, *, tm):
    m, d = x.shape
    tn = PROJ_TN
    n = 12 * D_GROUP
    return pl.pallas_call(
        _proj_kernel,
        grid=(m // tm, n // tn),
        in_specs=[
            pl.BlockSpec((tm, d), lambda i, j: (i, 0)),
            pl.BlockSpec((None, tn, d), lambda i, j: (l, j, 0)),
            pl.BlockSpec((None, 128, d), lambda i, j: (l, 0, 0)),
        ],
        out_specs=[
            pl.BlockSpec((tm, tn), lambda i, j: (i, j)),
            pl.BlockSpec((tm, 128), lambda i, j: (i, 0)),
        ],
        out_shape=[jax.ShapeDtypeStruct((m, n), F32), jax.ShapeDtypeStruct((m, 128), F32)],
        scratch_shapes=[pltpu.VMEM((tm, d), BF16)],
        compiler_params=_params("parallel", "arbitrary"),
        name="proj",
    )(x, w_t, wab_t)


def _ret_tables(pos, c):
    half = D_HEAD // 2
    inv = ROPE_BASE ** (-np.arange(half, dtype=np.float64) / half)
    ang = pos.astype(np.float64)[:, None] * inv[None, :]
    cos = np.concatenate([np.cos(ang), np.cos(ang)], -1)
    sin = np.concatenate([-np.sin(ang), np.sin(ang)], -1)
    log_g = np.log1p(-np.exp2(-5.0 - np.arange(N_HEADS, dtype=np.float64)))
    idx = np.arange(c, dtype=np.float64)
    diff = idx[:, None] - idx[None, :]
    scale = D_HEAD ** -0.5
    dmask = np.where(diff >= 0, np.exp(log_g[:, None, None] * np.maximum(diff, 0.0)), 0.0) * scale
    kdec = np.exp(log_g[:, None] * (c - 1.0 - idx)[None, :]) * scale
    qdec = np.exp(log_g[:, None] * (idx + 1.0)[None, :])
    kdec = np.broadcast_to(kdec[:, :, None], (N_HEADS, c, D_HEAD))
    qdec = np.broadcast_to(qdec[:, :, None], (N_HEADS, c, D_HEAD))
    g_c = np.exp(log_g * c)
    as32 = lambda a: jnp.asarray(np.ascontiguousarray(a), F32)
    return as32(cos), as32(sin), as32(dmask), as32(qdec), as32(kdec), [float(v) for v in g_c]


def _ret_kernel(q_ref, k_ref, v_ref, g_ref, cos_ref, sin_ref, dm_ref, qd_ref, kd_ref, gn_ref, *rest,
                c, n, g_c, n_t, bb, has_state, l):
    if has_state:
        s0_ref, o_ref, sf_ref, s_scr, sst_scr = rest
    else:
        o_ref, sf_ref, s_scr, sst_scr = rest
    t = pl.program_id(1)
    heads = range(N_HEADS)
    lanes = [slice(h * D_HEAD, (h + 1) * D_HEAD) for h in heads]
    tt = n * c

    @pl.when(t == 0)
    def _():
        s_scr[...] = s0_ref[...] if has_state else jnp.zeros_like(s_scr)

    cos = jnp.concatenate([cos_ref[...]] * bb, axis=0)
    sin = jnp.concatenate([sin_ref[...]] * bb, axis=0)

    def rows(ref, h):
        return ref[:, :, lanes[h]].reshape(bb * tt, D_HEAD)

    def rot(x):
        return (x * cos + pltpu.roll(x, D_HEAD // 2, 1) * sin).reshape(bb * n, c, D_HEAD)

    q3 = [rot(rows(q_ref, h)) for h in heads]
    k3 = [rot(rows(k_ref, h)) for h in heads]
    v3 = [rows(v_ref, h).reshape(bb * n, c, D_HEAD) for h in heads]
    scores = [_bmm_nt(q3[h], k3[h]) * dm_ref[h] for h in heads]
    o_intra = [_bmm(scores[h], v3[h]) for h in heads]
    kdt = [jnp.swapaxes(k3[h] * kd_ref[h], 1, 2) for h in heads]
    upd = [_bmm(kdt[h], v3[h]) for h in heads]
    for b in range(bb):
        for h in heads:
            s = s_scr[b, h]
            for i in range(n):
                sst_scr[h, b * n + i] = s.astype(BF16)
                s = g_c[h] * s + upd[h][b * n + i]
            s_scr[b, h] = s
    for h in heads:
        o = o_intra[h] + _bmm(q3[h] * qd_ref[h], sst_scr[h])
        mu = jnp.mean(o, -1, keepdims=True)
        oc = o - mu
        var = jnp.mean(oc * oc, -1, keepdims=True)
        on = (oc * lax.rsqrt(var + EPS)).reshape(bb * tt, D_HEAD) * gn_ref[l:l + 1, lanes[h]]
        o_ref[:, :, lanes[h]] = (on * _silu(rows(g_ref, h))).reshape(bb, tt, D_HEAD).astype(o_ref.dtype)

    @pl.when(t == n_t - 1)
    def _():
        sf_ref[...] = s_scr[...]


def _retention(proj3, state0, gn, pos, l, *, tt, bb):
    bsz, seq, _ = proj3.shape
    c = min(CHUNK, seq)
    cos, sin, dmask, qdec, kdec, g_c = _ret_tables(pos, c)
    n_t = seq // tt
    n = tt // c
    col = lambda j: pl.BlockSpec((bb, tt, D_GROUP), lambda b, t: (b, t, j))
    full3 = pl.BlockSpec((N_HEADS, c, c), lambda b, t: (0, 0, 0))
    full3d = pl.BlockSpec((N_HEADS, c, D_HEAD), lambda b, t: (0, 0, 0))
    st = pl.BlockSpec((bb, N_HEADS, D_HEAD, D_HEAD), lambda b, t: (b, 0, 0, 0))
    has_state = state0 is not None
    state_in = [state0] if has_state else []
    state_spec = ([pl.BlockSpec((None, bb, N_HEADS, D_HEAD, D_HEAD), lambda b, t: (l, b, 0, 0, 0))]
                  if has_state else [])
    return pl.pallas_call(
        functools.partial(_ret_kernel, c=c, n=n, g_c=g_c, n_t=n_t, bb=bb, has_state=has_state, l=l),
        grid=(bsz // bb, n_t),
        in_specs=[col(0), col(1), col(2), col(3),
                  pl.BlockSpec((tt, D_HEAD), lambda b, t: (t, 0)),
                  pl.BlockSpec((tt, D_HEAD), lambda b, t: (t, 0)),
                  full3, full3d, full3d,
                  pl.BlockSpec((DEPTH, D_GROUP), lambda b, t: (0, 0))] + state_spec,
        out_specs=[pl.BlockSpec((bb, tt, D_GROUP), lambda b, t: (b, t, 0)), st],
        out_shape=[jax.ShapeDtypeStruct((bsz, seq, D_GROUP), BF16),
                   jax.ShapeDtypeStruct((bsz, N_HEADS, D_HEAD, D_HEAD), F32)],
        scratch_shapes=[pltpu.VMEM((bb, N_HEADS, D_HEAD, D_HEAD), F32),
                        pltpu.VMEM((N_HEADS, bb * n, D_HEAD, D_HEAD), BF16)],
        compiler_params=_params("parallel", "arbitrary"),
        name="ret",
    )(proj3, proj3, proj3, proj3, cos, sin, dmask, qdec, kdec, gn, *state_in)


PAD = 8


def _conv_block(ext_ref, x, w_ref, tt):
    ext_ref[PAD:PAD + tt, :] = x
    out = x * w_ref[CONV_W - 1:CONV_W, :]
    for j in range(1, CONV_W):
        out = out + ext_ref[PAD - j:PAD - j + tt, :] * w_ref[CONV_W - 1 - j:CONV_W - j, :]
    ext_ref[PAD - (CONV_W - 1):PAD, :] = x[tt - (CONV_W - 1):, :]
    return out


def _lru_kernel(y_ref, x_ref, cs_ref, h0_ref, cw_ref, cb_ref, wa_ref, ba_ref, wx_ref, bx_ref, lam_ref,
                o_ref, hn_ref, cn_ref, ext_ref, a_scr, b_scr, h_scr, hc_ref, *, tt, n_t, l):
    t = pl.program_id(1)
    row = slice(l, l + 1)

    @pl.when(t == 0)
    def _():
        ext_ref[PAD - (CONV_W - 1):PAD, :] = cs_ref[0]
        hc_ref[...] = h0_ref[0]

    x = x_ref[0]
    xc = _conv_block(ext_ref, x, cw_ref, tt) + cb_ref[row, :]
    r = jax.nn.sigmoid(_mm(xc, wa_ref[...]) + ba_ref[row, :])
    i = jax.nn.sigmoid(_mm(xc, wx_ref[...]) + bx_ref[row, :])
    log_a = -LRU_C * r * jax.nn.softplus(-lam_ref[row, :])
    a_scr[...] = jnp.exp(log_a)
    th = jnp.tanh(log_a)
    one_minus_a2 = -2.0 * th / (1.0 - th)
    b_scr[...] = jnp.sqrt(one_minus_a2) * (i * xc)

    def step(s, h):
        row = pl.ds(s, 1)
        h = a_scr[row, :] * h + b_scr[row, :]
        h_scr[row, :] = h
        return h

    hc_ref[...] = lax.fori_loop(0, tt, step, hc_ref[...], unroll=8)
    o_ref[0] = (h_scr[...] * _gelu(y_ref[0])).astype(o_ref.dtype)

    @pl.when(t == n_t - 1)
    def _():
        hn_ref[0] = hc_ref[...]
        cn_ref[0] = x[tt - (CONV_W - 1):, :]


def _rglru(proj3, conv_state, h0, cw, cb, wa, ba, wx, bx, lam, l, *, tt):
    bsz, seq, _ = proj3.shape
    n_t = seq // tt
    col = lambda j: pl.BlockSpec((1, tt, D_GROUP), lambda b, t: (b, t, j))
    vec = pl.BlockSpec((DEPTH, D_GROUP), lambda b, t: (0, 0))
    sq = pl.BlockSpec((None, D_GROUP, D_GROUP), lambda b, t: (l, 0, 0))
    cs = pl.BlockSpec((1, CONV_W - 1, D_GROUP), lambda b, t: (b, 0, 0))
    hs = pl.BlockSpec((1, 1, D_GROUP), lambda b, t: (b, 0, 0))
    return pl.pallas_call(
        functools.partial(_lru_kernel, tt=tt, n_t=n_t, l=l),
        grid=(bsz, n_t),
        in_specs=[col(4), col(5), cs, hs,
                  pl.BlockSpec((None, CONV_W, D_GROUP), lambda b, t: (l, 0, 0)), vec, sq, vec, sq, vec, vec],
        out_specs=[pl.BlockSpec((1, tt, D_GROUP), lambda b, t: (b, t, 0)), hs, cs],
        out_shape=[jax.ShapeDtypeStruct((bsz, seq, D_GROUP), BF16),
                   jax.ShapeDtypeStruct((bsz, 1, D_GROUP), F32),
                   jax.ShapeDtypeStruct((bsz, CONV_W - 1, D_GROUP), F32)],
        scratch_shapes=[pltpu.VMEM((tt + PAD, D_GROUP), F32), pltpu.VMEM((tt, D_GROUP), F32),
                        pltpu.VMEM((tt, D_GROUP), F32), pltpu.VMEM((tt, D_GROUP), F32),
                        pltpu.VMEM((1, D_GROUP), F32)],
        compiler_params=_params("parallel", "arbitrary"),
        name="lru",
    )(proj3, proj3, conv_state, h0, cw, cb, wa, ba, wx, bx, lam)


def _sg_kernel(u_ref, v_ref, lg_ref, lb_ref, ws_ref, bs_ref, o_ref, *vn_ref, c, tt, bb, l):
    u = _gelu(u_ref[...])
    vn = _layer_norm(_gelu(v_ref[...]), lg_ref[l:l + 1, :], lb_ref[l:l + 1, :])
    if vn_ref:
        vn_ref[0][...] = vn
    row = lax.broadcasted_iota(jnp.int32, (c, c), 0)
    col = lax.broadcasted_iota(jnp.int32, (c, c), 1)
    mask = (col // CHUNK) <= (row // CHUNK)
    for h in range(N_HEADS):
        lanes = slice(h * D_HEAD, (h + 1) * D_HEAD)
        w = jnp.where(mask, ws_ref[h], 0.0).astype(BF16)
        bias = bs_ref[:, h:h + 1]
        for b in range(bb):
            for i in range(tt // c):
                rows = slice(i * c, (i + 1) * c)
                s = _mm(w, vn[b, rows, lanes]) + bias
                o_ref[b, rows, lanes] = (u[b, rows, lanes] * s).astype(o_ref.dtype)


def _spatial_gate(proj3, lg, lb, ws, bs_t, l, *, tt, bb, with_vn):
    bsz, seq, _ = proj3.shape
    c = min(SG_CHUNK, seq)
    col = lambda j: pl.BlockSpec((bb, tt, D_GROUP), lambda b, t: (b, t, j))
    vec = pl.BlockSpec((DEPTH, D_GROUP), lambda b, t: (0, 0))
    out = pl.BlockSpec((bb, tt, D_GROUP), lambda b, t: (b, t, 0))
    res = pl.pallas_call(
        functools.partial(_sg_kernel, c=c, tt=tt, bb=bb, l=l),
        grid=(bsz // bb, seq // tt),
        in_specs=[col(6), col(7), vec, vec,
                  pl.BlockSpec((None, N_HEADS, c, c), lambda b, t: (l, 0, 0, 0)),
                  pl.BlockSpec((None, c, N_HEADS), lambda b, t: (l, 0, 0))],
        out_specs=[out, out] if with_vn else [out],
        out_shape=[jax.ShapeDtypeStruct((bsz, seq, D_GROUP), BF16)]
                  + ([jax.ShapeDtypeStruct((bsz, seq, D_GROUP), F32)] if with_vn else []),
        compiler_params=_params("parallel", "parallel"),
        name="sg",
    )(proj3, proj3, lg, lb, ws, bs_t)
    return (res[0], res[1]) if with_vn else (res[0], None)


INV_BLOCK = 16


def _unit_lower_inverse(mats, c):
    row = lax.broadcasted_iota(jnp.int32, (c, c), 0)
    col = lax.broadcasted_iota(jnp.int32, (c, c), 1)
    eye = (row == col).astype(F32)
    same = row // INV_BLOCK == col // INV_BLOCK
    ps = [jnp.where(same, a, 0.0) for a in mats]
    xs = [eye - p for p in ps]
    ps = [_bmm(p, p) for p in ps]
    n_sq = int(math.log2(INV_BLOCK)) - 1
    for step in range(n_sq):
        last = step == n_sq - 1
        lhs = xs if last else [jnp.concatenate([x, p], axis=1) for x, p in zip(xs, ps)]
        prod = [_bmm(l, p) for l, p in zip(lhs, ps)]
        xs = [x + r[:, :c, :] for x, r in zip(xs, prod)]
        if not last:
            ps = [r[:, c:, :] for r in prod]
    size = INV_BLOCK
    while size < c:
        big = 2 * size
        sel = (row // big == col // big) & (row // size != col // size)
        offs = [jnp.where(sel, a, 0.0) for a in mats]
        tmp = [_bmm(o, x) for o, x in zip(offs, xs)]
        xs = [x - _bmm(x, t) for x, t in zip(xs, tmp)]
        size = big
    return xs


def _dn_kernel(q_ref, k_ref, v_ref, z_ref, ab_ref, cw_ref, al_ref, dt_ref, gn_ref, *rest,
               c, n, n_t, bb, has_state, l):
    lrow = slice(l, l + 1)
    if has_state:
        cs_ref, s0_ref = rest[:2]
        rest = rest[2:]
    o_ref, sf_ref, cn_ref, ext_ref, s_scr, u_scr, wq_scr, attn_scr, kdt_scr, egl_scr = rest
    t = pl.program_id(1)
    tt = n * c
    nb = bb * n
    heads = range(N_HEADS)

    @pl.when(t == 0)
    def _():
        for j in range(3):
            hist = (cs_ref[:, :, j * D_GROUP:(j + 1) * D_GROUP] if has_state
                    else jnp.zeros((bb, CONV_W - 1, D_GROUP), F32))
            ext_ref[j, :, PAD - (CONV_W - 1):PAD, :] = hist
        s_scr[...] = s0_ref[...] if has_state else jnp.zeros_like(s_scr)

    raw = (q_ref, k_ref, v_ref)
    qf, kf, vf = [
        jnp.concatenate([_silu(_conv_block(ext_ref.at[j, b], raw[j][b],
                                           cw_ref.at[:, j * D_GROUP:(j + 1) * D_GROUP], tt))
                         for b in range(bb)], axis=0)
        for j in range(3)]

    row = lax.broadcasted_iota(jnp.int32, (c, c), 0)
    col = lax.broadcasted_iota(jnp.int32, (c, c), 1)
    tri = row >= col
    strict = row > col
    tri_f = tri.astype(F32)
    scale = D_HEAD ** -0.5

    ab = ab_ref[...].reshape(bb * tt, 128)
    g_all = -jnp.exp(al_ref[lrow, :]) * jax.nn.softplus(ab + dt_ref[lrow, :])
    beta_all = jax.nn.sigmoid(ab)
    gc_all = jnp.concatenate(
        [jnp.dot(tri_f, g_all[i * c:(i + 1) * c], preferred_element_type=F32, precision=lax.Precision.HIGHEST)
         for i in range(nb)], axis=0)
    gc_t = gc_all.T

    def l2n(x):
        return x * lax.rsqrt(jnp.sum(x * x, -1, keepdims=True) + 1e-6)

    def split(x, h, k0, k1):
        return x[k0 * c:k1 * c, h * D_HEAD:(h + 1) * D_HEAD].reshape(k1 - k0, c, D_HEAD)

    def chunk_local(k0, k1):
        m = k1 - k0
        rows = slice(k0 * c, k1 * c)
        q3 = [l2n(split(qf, h, k0, k1)) * scale for h in heads]
        k3 = [l2n(split(kf, h, k0, k1)) for h in heads]
        v3 = [split(vf, h, k0, k1) for h in heads]
        beta = [beta_all[rows, N_HEADS + h:N_HEADS + h + 1].reshape(m, c, 1) for h in heads]
        gcol = [gc_all[rows, h:h + 1].reshape(m, c, 1) for h in heads]
        grow = [jnp.stack([gc_t[h:h + 1, i * c:(i + 1) * c] for i in range(k0, k1)]) for h in heads]
        glast = [g[:, c - 1:c, :] for g in gcol]
        decay = [jnp.where(tri, jnp.exp(gc - gr), 0.0) for gc, gr in zip(gcol, grow)]
        egc = [jnp.exp(g) for g in gcol]
        kb = [k * b for k, b in zip(k3, beta)]
        vb = [v * b for v, b in zip(v3, beta)]
        a_mat = [jnp.where(strict, _bmm_nt(x, k) * d, 0.0) for x, k, d in zip(kb, k3, decay)]
        tinv = _unit_lower_inverse(a_mat, c)
        uw = [_bmm(ti, jnp.concatenate([x, y * e], axis=-1)) for ti, x, y, e in zip(tinv, vb, kb, egc)]
        attn = [_bmm_nt(q, k) * d for q, k, d in zip(q3, k3, decay)]
        for h in heads:
            u_scr[h, k0:k1] = uw[h][:, :, :D_HEAD]
            wq_scr[h, k0:k1] = jnp.concatenate([uw[h][:, :, D_HEAD:], q3[h] * egc[h]], axis=1).astype(BF16)
            attn_scr[h, k0:k1] = attn[h].astype(BF16)
            kdt_scr[h, k0:k1] = jnp.swapaxes(k3[h] * jnp.exp(glast[h] - gcol[h]), 1, 2).astype(BF16)
            egl_scr[h, k0:k1] = jnp.broadcast_to(jnp.exp(glast[h]), (m, 1, D_HEAD))

    def sequential(k0, k1, state):
        for k in range(k0, k1):
            b, i = divmod(k, n)
            rows = slice(i * c, (i + 1) * c)
            s = [state[b, h] for h in heads]
            r = [_mm(wq_scr[h, k], s[h]) for h in heads]
            v_new = [u_scr[h, k] - r[h][:c] for h in heads]
            for h in heads:
                state[b, h] = s[h] * egl_scr[h, k] + _mm(kdt_scr[h, k], v_new[h])
            o = [r[h][c:] + _mm(attn_scr[h, k], v_new[h]) for h in heads]
            for h in heads:
                lanes = slice(h * D_HEAD, (h + 1) * D_HEAD)
                on = o[h] * lax.rsqrt(jnp.mean(o[h] * o[h], -1, keepdims=True) + EPS) * gn_ref[lrow, :]
                o_ref[b, rows, lanes] = (on * _silu(z_ref[b, rows, lanes])).astype(o_ref.dtype)

    groups = 2 if (n > 1 and nb % 2 == 0) else 1
    bounds = [(g * nb // groups, (g + 1) * nb // groups) for g in range(groups)]
    for k0, k1 in bounds:
        chunk_local(k0, k1)
    state = {(b, h): s_scr[b, h] for b in range(bb) for h in heads}
    for k0, k1 in bounds:
        sequential(k0, k1, state)
    for (b, h), s in state.items():
        s_scr[b, h] = s

    @pl.when(t == n_t - 1)
    def _():
        sf_ref[...] = s_scr[...]
        for j in range(3):
            cn_ref[:, :, j * D_GROUP:(j + 1) * D_GROUP] = raw[j][:, tt - (CONV_W - 1):, :]


def _deltanet(proj3, ab3, states, cw, a_log, dt_bias, gn, l, *, tt, bb):
    bsz, seq, _ = proj3.shape
    c = min(CHUNK, seq)
    n_t = seq // tt
    n = tt // c
    nb = bb * n
    col = lambda j: pl.BlockSpec((bb, tt, D_GROUP), lambda b, t: (b, t, j))
    st = pl.BlockSpec((bb, N_HEADS, D_HEAD, D_HEAD), lambda b, t: (b, 0, 0, 0))
    cs = pl.BlockSpec((bb, CONV_W - 1, 3 * D_GROUP), lambda b, t: (b, 0, 0))
    lane_row = pl.BlockSpec((DEPTH, 128), lambda b, t: (0, 0))
    has_state = states is not None
    state_in = list(states) if has_state else []
    state_spec = ([pl.BlockSpec((None, bb, CONV_W - 1, 3 * D_GROUP), lambda b, t: (l, b, 0, 0)),
                   pl.BlockSpec((None, bb, N_HEADS, D_HEAD, D_HEAD), lambda b, t: (l, b, 0, 0, 0))]
                  if has_state else [])
    return pl.pallas_call(
        functools.partial(_dn_kernel, c=c, n=n, n_t=n_t, bb=bb, has_state=has_state, l=l),
        grid=(bsz // bb, n_t),
        in_specs=[col(8), col(9), col(10), col(11),
                  pl.BlockSpec((bb, tt, 128), lambda b, t: (b, t, 0)),
                  pl.BlockSpec((None, CONV_W, 3 * D_GROUP), lambda b, t: (l, 0, 0)),
                  lane_row, lane_row, lane_row] + state_spec,
        out_specs=[pl.BlockSpec((bb, tt, D_GROUP), lambda b, t: (b, t, 0)), st, cs],
        out_shape=[jax.ShapeDtypeStruct((bsz, seq, D_GROUP), BF16),
                   jax.ShapeDtypeStruct((bsz, N_HEADS, D_HEAD, D_HEAD), F32),
                   jax.ShapeDtypeStruct((bsz, CONV_W - 1, 3 * D_GROUP), F32)],
        scratch_shapes=[pltpu.VMEM((3, bb, tt + PAD, D_GROUP), F32),
                        pltpu.VMEM((bb, N_HEADS, D_HEAD, D_HEAD), F32),
                        pltpu.VMEM((N_HEADS, nb, c, D_HEAD), F32),
                        pltpu.VMEM((N_HEADS, nb, 2 * c, D_HEAD), BF16),
                        pltpu.VMEM((N_HEADS, nb, c, c), BF16),
                        pltpu.VMEM((N_HEADS, nb, D_HEAD, c), BF16),
                        pltpu.VMEM((N_HEADS, nb, 1, D_HEAD), F32)],
        compiler_params=_params("parallel", "arbitrary"),
        name="dn",
    )(proj3, proj3, proj3, proj3, ab3, cw, a_log, dt_bias, gn, *state_in)


def _mixout_kernel(a_ref, b_ref, c_ref, d_ref, x_ref, w_ref, g_ref, bb_ref, o_ref, cat_ref, *, l):
    for j, part in enumerate((a_ref, b_ref, c_ref, d_ref)):
        cat_ref[:, j * D_GROUP:(j + 1) * D_GROUP] = part[...]
    n_row = 4 if o_ref.shape[0] >= 1024 else 2
    tr = o_ref.shape[0] // n_row
    for r in range(n_row):
        rows = slice(r * tr, (r + 1) * tr)
        acc = jnp.dot(cat_ref[rows, :], w_ref[...], preferred_element_type=F32)
        o_ref[rows, :] = _layer_norm(ALPHA * x_ref[rows, :] + acc, g_ref[l:l + 1, :], bb_ref[l:l + 1, :])


def _mixout(a, b, c, d, x, w, g, bb, l, *, tm):
    m, dm = x.shape
    part = pl.BlockSpec((tm, D_GROUP), lambda i: (i, 0))
    vec = pl.BlockSpec((DEPTH, dm), lambda i: (0, 0))
    return pl.pallas_call(
        functools.partial(_mixout_kernel, l=l),
        grid=(m // tm,),
        in_specs=[part, part, part, part,
                  pl.BlockSpec((tm, dm), lambda i: (i, 0)),
                  pl.BlockSpec((None, 4 * D_GROUP, dm), lambda i: (l, 0, 0), pipeline_mode=pl.Buffered(1)),
                  vec, vec],
        out_specs=pl.BlockSpec((tm, dm), lambda i: (i, 0)),
        out_shape=jax.ShapeDtypeStruct((m, dm), F32),
        scratch_shapes=[pltpu.VMEM((tm, 4 * D_GROUP), BF16)],
        compiler_params=_params("parallel"),
        name="mixout",
    )(a, b, c, d, x, w, g, bb)


def _block_diag(w):
    dep, h, n, _ = w.shape
    eye = jnp.eye(h, dtype=w.dtype)
    return (eye[None, :, None, :, None] * w[:, :, :, None, :]).reshape(dep, h * n, h * n)


def _prep(p):
    pad128 = lambda v: jnp.pad(v, ((0, 0), (0, 128 - v.shape[-1])))
    q = dict(p)
    q['w_mix_out'] = p['w_mix_out'].astype(BF16)
    w_in_t = jnp.swapaxes(p['w_mix_in'], 1, 2)
    q['w_mix_in'] = w_in_t.astype(BF16)
    q['w_mix_ab'] = jnp.pad(w_in_t[:, 12 * D_GROUP:, :], ((0, 0), (0, 128 - 2 * N_HEADS), (0, 0))).astype(BF16)
    q['lru_w_a'] = _block_diag(p['lru_w_a']).astype(BF16)
    q['lru_w_x'] = _block_diag(p['lru_w_x']).astype(BF16)
    q['dn_a_log'] = pad128(p['dn_a_log'])
    q['dn_dt_bias'] = pad128(p['dn_dt_bias'])
    return q


def _token_mixers(x, st, p, l, s):
    bsz, seq, tm, tt, bb = s['bsz'], s['seq'], s['tm'], s['tt'], s['bb']
    c_sg = min(SG_CHUNK, seq)
    sg_w = p['sg_w'][:, :, :c_sg, :c_sg]
    sg_b_t = jnp.swapaxes(p['sg_b'][:, :, :c_sg], 1, 2)
    proj, ab = _proj(x, p['w_mix_in'], p['w_mix_ab'], l, tm=tm)
    proj3 = proj.reshape(bsz, seq, -1)
    ab3 = ab.reshape(bsz, seq, 128)
    if st is None:
        s_ret = dn_states = None
        s_lru, s_lru_conv = jnp.zeros((bsz, D_GROUP), F32), jnp.zeros((bsz, CONV_W - 1, D_GROUP), F32)
    else:
        s_ret, s_lru, s_lru_conv, s_dn, s_dn_conv = st
        s_lru, s_lru_conv = s_lru[l], s_lru_conv[l]
        dn_states = (s_dn_conv, s_dn)
    tt_wide = min(seq, 2 * tt)
    out_a, ret_new = _retention(proj3, s_ret, p['ret_norm_g'], s['pos'], l, tt=tt_wide, bb=bb)
    out_b, lru_h_new, lru_conv_new = _rglru(
        proj3, s_lru_conv, s_lru.reshape(bsz, 1, D_GROUP), p['lru_conv_w'], p['lru_conv_b'],
        p['lru_w_a'], p['lru_b_a'], p['lru_w_x'], p['lru_b_x'], p['lru_lam'], l, tt=tt_wide)
    out_c, sg_v = _spatial_gate(proj3, p['sg_ln_g'], p['sg_ln_b'], sg_w, sg_b_t, l,
                                tt=tt_wide, bb=bb, with_vn=s['keep_sg_v'])
    out_d, dn_new, dn_conv_new = _deltanet(proj3, ab3, dn_states, p['dn_conv_w'],
                                           p['dn_a_log'], p['dn_dt_bias'], p['dn_norm_g'], l, tt=tt, bb=bb)
    flat = lambda o: o.reshape(bsz * seq, D_GROUP)
    x = _mixout(flat(out_a), flat(out_b), flat(out_c), flat(out_d), x, p['w_mix_out'],
                p['ln2_g'], p['ln2_b'], l, tm=tm)
    return x, (ret_new, lru_h_new.reshape(bsz, D_GROUP), lru_conv_new, dn_new, dn_conv_new, sg_v)


def kernel(x_prompt, x_sample, state_ret, state_lru_h, state_lru_conv, state_dn, state_dn_conv, ffn1_w_in, ffn1_w_out, ln1_g, ln1_b, w_mix_in, ret_norm_g, lru_conv_w, lru_conv_b, lru_w_a, lru_b_a, lru_w_x, lru_b_x, lru_lam, sg_ln_g, sg_ln_b, sg_w, sg_b, dn_conv_w, dn_a_log, dn_dt_bias, dn_norm_g, w_mix_out, ln2_g, ln2_b, ffn2_w_in, ffn2_w_out, ln3_g, ln3_b):
    weights = dict(ffn1_w_in=ffn1_w_in, ffn1_w_out=ffn1_w_out, ln1_g=ln1_g, ln1_b=ln1_b, w_mix_in=w_mix_in,
                   ret_norm_g=ret_norm_g, lru_conv_w=lru_conv_w, lru_conv_b=lru_conv_b, lru_w_a=lru_w_a,
                   lru_b_a=lru_b_a, lru_w_x=lru_w_x, lru_b_x=lru_b_x, lru_lam=lru_lam, sg_ln_g=sg_ln_g,
                   sg_ln_b=sg_ln_b, sg_w=sg_w, sg_b=sg_b, dn_conv_w=dn_conv_w, dn_a_log=dn_a_log,
                   dn_dt_bias=dn_dt_bias, dn_norm_g=dn_norm_g, w_mix_out=w_mix_out, ln2_g=ln2_g, ln2_b=ln2_b,
                   ffn2_w_in=ffn2_w_in, ffn2_w_out=ffn2_w_out, ln3_g=ln3_g, ln3_b=ln3_b)
    p = _prep(weights)

    bp, lp, dm = x_prompt.shape
    bs, ls, _ = x_sample.shape
    prompt = dict(bsz=bp, seq=lp, tm=1024, tt=min(lp, 512), bb=1, pos=np.arange(lp), keep_sg_v=False)
    sample = dict(bsz=bs, seq=ls, tm=bs * ls, tt=ls, bb=min(bs, 8), pos=PAST_LEN + np.arange(ls), keep_sg_v=True)
    sample_states = (state_ret, state_lru_h, state_lru_conv, state_dn, state_dn_conv)
    xp = x_prompt.reshape(bp * lp, dm)
    xs = x_sample.reshape(bs * ls, dm)

    ffns = [(l, name) for l in range(DEPTH) for name in ('ffn1', 'ffn2')]
    ln_of = {'ffn1': ('ln1_g', 'ln1_b'), 'ffn2': ('ln3_g', 'ln3_b')}

    def ffn_pair(k, xp, xs, wb):
        l, name = ffns[k]
        g, b = p[ln_of[name][0]], p[ln_of[name][1]]
        nxt = None
        if k + 1 < len(ffns):
            ln, nname = ffns[k + 1]
            nxt = (weights[nname + '_w_in'], weights[nname + '_w_out'], ln)
        if wb is None:
            xs, wb = _ffn(xs, (weights[name + '_w_in'], weights[name + '_w_out']), g, b, l, tm=sample['tm'],
                          from_f32=True)
        else:
            xs, _ = _ffn(xs, wb, g, b, l, tm=sample['tm'])
        xp, wb_next = _ffn(xp, wb, g, b, l, tm=prompt['tm'], cast_next=nxt)
        return xp, xs, wb_next

    wb = None

    st_p, st_s = [], []
    for l in range(DEPTH):
        xp, xs, wb = ffn_pair(2 * l, xp, xs, wb)
        xp, new_p = _token_mixers(xp, None, p, l, prompt)
        xs, new_s = _token_mixers(xs, sample_states, p, l, sample)
        xp, xs, wb = ffn_pair(2 * l + 1, xp, xs, wb)
        st_p.append(new_p)
        st_s.append(new_s)

    stack = lambda sts, i: jnp.stack([s[i] for s in sts])
    return (xp.reshape(bp, lp, dm), xs.reshape(bs, ls, dm),
            stack(st_p, 0), stack(st_p, 1), stack(st_p, 2), stack(st_p, 3), stack(st_p, 4),
            stack(st_s, 0), stack(st_s, 1), stack(st_s, 2), stack(st_s, 3), stack(st_s, 4), stack(st_s, 5))
```

```python
import functools
import math

import numpy as np
import jax
import jax.numpy as jnp
from jax import lax
from jax.experimental import pallas as pl
from jax.experimental.pallas import tpu as pltpu

F32 = jnp.float32
BF16 = jnp.bfloat16

DEPTH = 2
PAST_LEN = 4096
CHUNK = 64
D_GROUP = 512
N_HEADS = 4
D_HEAD = D_GROUP // N_HEADS
LRU_C = 8.0
CONV_W = 4
SG_CHUNK = 128
ROPE_BASE = 10000.0
ALPHA = (2.0 * DEPTH) ** 0.25
EPS = 1e-5

FFN_TF = 512
PROJ_TN = 2048
VMEM_LIMIT = 58 * 1024 * 1024


def _params(*sem):
    return pltpu.CompilerParams(dimension_semantics=sem, vmem_limit_bytes=VMEM_LIMIT)


def _mm(a, b):
    return jnp.dot(a.astype(BF16), b.astype(BF16), preferred_element_type=F32)


def _dot_nt(a, b):
    return lax.dot_general(a, b, (((1,), (1,)), ((), ())), preferred_element_type=F32)


def _bmm(a, b):
    return jnp.einsum('nij,njk->nik', a.astype(BF16), b.astype(BF16), preferred_element_type=F32)


def _bmm_nt(a, b):
    return jnp.einsum('nid,njd->nij', a.astype(BF16), b.astype(BF16), preferred_element_type=F32)


def _layer_norm(y, g, b):
    mu = jnp.mean(y, -1, keepdims=True)
    yc = y - mu
    var = jnp.mean(yc * yc, -1, keepdims=True)
    return yc * lax.rsqrt(var + EPS) * g + b


def _silu(x):
    return x * jax.nn.sigmoid(x)


def _gelu(x):
    return jax.nn.gelu(x, approximate=True)


def _ffn_kernel(x_hbm, wg_ref, wu_ref, wo_ref, g_ref, b_ref, *rest, nf, ni, n_row, mode, l):
    *rest, xb_ref, xf_ref, x_sem = rest
    if mode == 'cast_next':
        ng_ref, nu_ref, no_ref, o_ref, ngo_ref, nuo_ref, noo_ref = rest
        ngo_ref[...] = ng_ref[...].astype(BF16)
        nuo_ref[...] = nu_ref[...].astype(BF16)
        noo_ref[...] = no_ref[...].astype(BF16)
    elif mode == 'from_f32':
        o_ref, wgo_ref, wuo_ref, woo_ref = rest
    else:
        (o_ref,) = rest
    i = pl.program_id(0)
    f = pl.program_id(1)
    tm = xf_ref.shape[0]

    def x_copy(k):
        return pltpu.make_async_copy(x_hbm.at[pl.ds(k * tm, tm), :], xf_ref, x_sem)

    @pl.when(f == 0)
    def _():
        @pl.when(i == 0)
        def _():
            x_copy(0).start()

        x_copy(i).wait()
        x = xf_ref[...]
        xb_ref[...] = x.astype(BF16)
        o_ref[...] = (2.0 * ALPHA) * x

    @pl.when((f == 1) & (i + 1 < ni))
    def _():
        x_copy(i + 1).start()

    wg, wu, wo = wg_ref[...], wu_ref[...], wo_ref[...]
    if mode == 'from_f32':
        wg, wu, wo = wg.astype(BF16), wu.astype(BF16), wo.astype(BF16)
        wgo_ref[...] = wg
        wuo_ref[...] = wu
        woo_ref[...] = wo
    tr = o_ref.shape[0] // n_row

    for r in range(n_row):
        rows = slice(r * tr, (r + 1) * tr)
        xb = xb_ref[rows, :]
        gate = jnp.dot(xb, wg, preferred_element_type=F32)
        up = jnp.dot(xb, wu, preferred_element_type=F32)
        h = (_silu(gate) * up).astype(BF16)
        o_ref[rows, :] += jnp.dot(h, wo, preferred_element_type=F32)

    @pl.when(f == nf - 1)
    def _():
        o_ref[...] = _layer_norm(0.5 * o_ref[...], g_ref[l:l + 1, :], b_ref[l:l + 1, :])


def _ffn(x, w, g, b, l, *, tm, tf=FFN_TF, cast_next=None, from_f32=False):
    m, d = x.shape
    d_ff = w[1].shape[1] if from_f32 else w[2].shape[0]
    nf = d_ff // tf
    assert nf * tf == d_ff
    ni = m // tm
    vec = pl.BlockSpec((DEPTH, d), lambda i, f: (0, 0))
    assert nf >= 2 or ni == 1
    x_spec = pl.BlockSpec(memory_space=pl.ANY)
    tile_g = pl.BlockSpec((d, tf), lambda i, f: (0, f))
    tile_o = pl.BlockSpec((tf, d), lambda i, f: (f, 0))
    w_shapes = [jax.ShapeDtypeStruct((d, d_ff), BF16), jax.ShapeDtypeStruct((d, d_ff), BF16),
                jax.ShapeDtypeStruct((d_ff, d), BF16)]
    out_specs = [pl.BlockSpec((tm, d), lambda i, f: (i, 0))]
    out_shape = [jax.ShapeDtypeStruct((m, d), F32)]
    mode = None
    if from_f32:
        assert ni == 1 and cast_next is None
        mode = 'from_f32'
        w_in, w_out = w
        in_specs = [x_spec,
                    pl.BlockSpec((None, d, tf), lambda i, f: (l, 0, f)),
                    pl.BlockSpec((None, d, tf), lambda i, f: (l, 0, f + nf)),
                    pl.BlockSpec((None, tf, d), lambda i, f: (l, f, 0)), vec, vec]
        args = [x, w_in, w_in, w_out, g, b]
        out_specs += [tile_g, tile_g, tile_o]
        out_shape += w_shapes
    else:
        in_specs = [x_spec, tile_g, tile_g, tile_o, vec, vec]
        args = [x, w[0], w[1], w[2], g, b]
    if cast_next is not None:
        mode = 'cast_next'
        nw_in, nw_out, ln = cast_next
        ri, ro = d // ni, d_ff // (ni * nf)
        assert ri * ni == d and ro * ni * nf == d_ff
        in_specs += [pl.BlockSpec((None, ri, tf), lambda i, f: (ln, i, f)),
                     pl.BlockSpec((None, ri, tf), lambda i, f: (ln, i, f + nf)),
                     pl.BlockSpec((None, ro, d), lambda i, f: (ln, i * nf + f, 0))]
        args += [nw_in, nw_in, nw_out]
        out_specs += [pl.BlockSpec((ri, tf), lambda i, f: (i, f)), pl.BlockSpec((ri, tf), lambda i, f: (i, f)),
                      pl.BlockSpec((ro, d), lambda i, f: (i * nf + f, 0))]
        out_shape += w_shapes
    res = pl.pallas_call(
        functools.partial(_ffn_kernel, nf=nf, ni=ni, n_row=2 if tm >= 1024 else 1, mode=mode, l=l),
        grid=(ni, nf),
        in_specs=in_specs,
        out_specs=out_specs,
        out_shape=out_shape,
        scratch_shapes=[pltpu.VMEM((tm, d), BF16), pltpu.VMEM((tm, d), F32), pltpu.SemaphoreType.DMA(())],
        compiler_params=_params("arbitrary", "arbitrary"),
        name="ffn",
    )(*args)
    return res[0], (tuple(res[1:]) if mode is not None else None)


def _proj_kernel(x_ref, w_ref, wab_ref, o_ref, ab_ref, xb_ref):
    j = pl.program_id(1)

    @pl.when(j == 0)
    def _():
        xb = x_ref[...].astype(BF16)
        xb_ref[...] = xb
        ab_ref[...] = _dot_nt(xb, wab_ref[...])

    o_ref[...] = _dot_nt(xb_ref[...], w_ref[...])


def _proj(x, w_t, wab_t, l, *, tm):
    m, d = x.shape
    tn = PROJ_TN
    n = 12 * D_GROUP
    return pl.pallas_call(
        _proj_kernel,
        grid=(m // tm, n // tn),
        in_specs=[
            pl.BlockSpec((tm, d), lambda i, j: (i, 0)),
            pl.BlockSpec((None, tn, d), lambda i, j: (l, j, 0)),
            pl.BlockSpec((None, 128, d), lambda i, j: (l, 0, 0)),
        ],
        out_specs=[
            pl.BlockSpec((tm, tn), lambda i, j: (i, j)),
            pl.BlockSpec((tm, 128), lambda i, j: (i, 0)),
        ],
        out_shape=[jax.ShapeDtypeStruct((m, n), F32), jax.ShapeDtypeStruct((m, 128), F32)],
        scratch_shapes=[pltpu.VMEM((tm, d), BF16)],
        compiler_params=_params("parallel", "arbitrary"),
        name="proj",
    )(x, w_t, wab_t)


def _ret_tables(pos, c):
    half = D_HEAD // 2
    inv = ROPE_BASE ** (-np.arange(half, dtype=np.float64) / half)
    ang = pos.astype(np.float64)[:, None] * inv[None, :]
    cos = np.concatenate([np.cos(ang), np.cos(ang)], -1)
    sin = np.concatenate([-np.sin(ang), np.sin(ang)], -1)
    log_g = np.log1p(-np.exp2(-5.0 - np.arange(N_HEADS, dtype=np.float64)))
    idx = np.arange(c, dtype=np.float64)
    diff = idx[:, None] - idx[None, :]
    scale = D_HEAD ** -0.5
    dmask = np.where(diff >= 0, np.exp(log_g[:, None, None] * np.maximum(diff, 0.0)), 0.0) * scale
    kdec = np.exp(log_g[:, None] * (c - 1.0 - idx)[None, :]) * scale
    qdec = np.exp(log_g[:, None] * (idx + 1.0)[None, :])
    kdec = np.broadcast_to(kdec[:, :, None], (N_HEADS, c, D_HEAD))
    qdec = np.broadcast_to(qdec[:, :, None], (N_HEADS, c, D_HEAD))
    g_c = np.exp(log_g * c)
    as32 = lambda a: jnp.asarray(np.ascontiguousarray(a), F32)
    return as32(cos), as32(sin), as32(dmask), as32(qdec), as32(kdec), [float(v) for v in g_c]


def _ret_kernel(q_ref, k_ref, v_ref, g_ref, cos_ref, sin_ref, dm_ref, qd_ref, kd_ref, gn_ref, *rest,
                c, n, g_c, n_t, bb, has_state, l):
    if has_state:
        s0_ref, o_ref, sf_ref, s_scr, sst_scr = rest
    else:
        o_ref, sf_ref, s_scr, sst_scr = rest
    t = pl.program_id(1)
    heads = range(N_HEADS)
    lanes = [slice(h * D_HEAD, (h + 1) * D_HEAD) for h in heads]
    tt = n * c

    @pl.when(t == 0)
    def _():
        s_scr[...] = s0_ref[...] if has_state else jnp.zeros_like(s_scr)

    cos = jnp.concatenate([cos_ref[...]] * bb, axis=0)
    sin = jnp.concatenate([sin_ref[...]] * bb, axis=0)

    def rows(ref, h):
        return ref[:, :, lanes[h]].reshape(bb * tt, D_HEAD)

    def rot(x):
        return (x * cos + pltpu.roll(x, D_HEAD // 2, 1) * sin).reshape(bb * n, c, D_HEAD)

    q3 = [rot(rows(q_ref, h)) for h in heads]
    k3 = [rot(rows(k_ref, h)) for h in heads]
    v3 = [rows(v_ref, h).reshape(bb * n, c, D_HEAD) for h in heads]
    scores = [_bmm_nt(q3[h], k3[h]) * dm_ref[h] for h in heads]
    o_intra = [_bmm(scores[h], v3[h]) for h in heads]
    kdt = [jnp.swapaxes(k3[h] * kd_ref[h], 1, 2) for h in heads]
    upd = [_bmm(kdt[h], v3[h]) for h in heads]
    for b in range(bb):
        for h in heads:
            s = s_scr[b, h]
            for i in range(n):
                sst_scr[h, b * n + i] = s.astype(BF16)
                s = g_c[h] * s + upd[h][b * n + i]
            s_scr[b, h] = s
    for h in heads:
        o = o_intra[h] + _bmm(q3[h] * qd_ref[h], sst_scr[h])
        mu = jnp.mean(o, -1, keepdims=True)
        oc = o - mu
        var = jnp.mean(oc * oc, -1, keepdims=True)
        on = (oc * lax.rsqrt(var + EPS)).reshape(bb * tt, D_HEAD) * gn_ref[l:l + 1, lanes[h]]
        o_ref[:, :, lanes[h]] = (on * _silu(rows(g_ref, h))).reshape(bb, tt, D_HEAD).astype(o_ref.dtype)

    @pl.when(t == n_t - 1)
    def _():
        sf_ref[...] = s_scr[...]


def _retention(proj3, state0, gn, pos, l, *, tt, bb):
    bsz, seq, _ = proj3.shape
    c = min(CHUNK, seq)
    cos, sin, dmask, qdec, kdec, g_c = _ret_tables(pos, c)
    n_t = seq // tt
    n = tt // c
    col = lambda j: pl.BlockSpec((bb, tt, D_GROUP), lambda b, t: (b, t, j))
    full3 = pl.BlockSpec((N_HEADS, c, c), lambda b, t: (0, 0, 0))
    full3d = pl.BlockSpec((N_HEADS, c, D_HEAD), lambda b, t: (0, 0, 0))
    st = pl.BlockSpec((bb, N_HEADS, D_HEAD, D_HEAD), lambda b, t: (b, 0, 0, 0))
    has_state = state0 is not None
    state_in = [state0] if has_state else []
    state_spec = ([pl.BlockSpec((None, bb, N_HEADS, D_HEAD, D_HEAD), lambda b, t: (l, b, 0, 0, 0))]
                  if has_state else [])
    return pl.pallas_call(
        functools.partial(_ret_kernel, c=c, n=n, g_c=g_c, n_t=n_t, bb=bb, has_state=has_state, l=l),
        grid=(bsz // bb, n_t),
        in_specs=[col(0), col(1), col(2), col(3),
                  pl.BlockSpec((tt, D_HEAD), lambda b, t: (t, 0)),
                  pl.BlockSpec((tt, D_HEAD), lambda b, t: (t, 0)),
                  full3, full3d, full3d,
                  pl.BlockSpec((DEPTH, D_GROUP), lambda b, t: (0, 0))] + state_spec,
        out_specs=[pl.BlockSpec((bb, tt, D_GROUP), lambda b, t: (b, t, 0)), st],
        out_shape=[jax.ShapeDtypeStruct((bsz, seq, D_GROUP), BF16),
                   jax.ShapeDtypeStruct((bsz, N_HEADS, D_HEAD, D_HEAD), F32)],
        scratch_shapes=[pltpu.VMEM((bb, N_HEADS, D_HEAD, D_HEAD), F32),
                        pltpu.VMEM((N_HEADS, bb * n, D_HEAD, D_HEAD), BF16)],
        compiler_params=_params("parallel", "arbitrary"),
        name="ret",
    )(proj3, proj3, proj3, proj3, cos, sin, dmask, qdec, kdec, gn, *state_in)


PAD = 8


def _conv_block(ext_ref, x, w_ref, tt):
    ext_ref[PAD:PAD + tt, :] = x
    out = x * w_ref[CONV_W - 1:CONV_W, :]
    for j in range(1, CONV_W):
        out = out + ext_ref[PAD - j:PAD - j + tt, :] * w_ref[CONV_W - 1 - j:CONV_W - j, :]
    ext_ref[PAD - (CONV_W - 1):PAD, :] = x[tt - (CONV_W - 1):, :]
    return out


def _lru_kernel(y_ref, x_ref, cs_ref, h0_ref, cw_ref, cb_ref, wa_ref, ba_ref, wx_ref, bx_ref, lam_ref,
                o_ref, hn_ref, cn_ref, ext_ref, a_scr, b_scr, h_scr, hc_ref, *, tt, n_t, l):
    t = pl.program_id(1)
    row = slice(l, l + 1)

    @pl.when(t == 0)
    def _():
        ext_ref[PAD - (CONV_W - 1):PAD, :] = cs_ref[0]
        hc_ref[...] = h0_ref[0]

    x = x_ref[0]
    xc = _conv_block(ext_ref, x, cw_ref, tt) + cb_ref[row, :]
    r = jax.nn.sigmoid(_mm(xc, wa_ref[...]) + ba_ref[row, :])
    i = jax.nn.sigmoid(_mm(xc, wx_ref[...]) + bx_ref[row, :])
    log_a = -LRU_C * r * jax.nn.softplus(-lam_ref[row, :])
    a_scr[...] = jnp.exp(log_a)
    th = jnp.tanh(log_a)
    one_minus_a2 = -2.0 * th / (1.0 - th)
    b_scr[...] = jnp.sqrt(one_minus_a2) * (i * xc)

    def step(s, h):
        row = pl.ds(s, 1)
        h = a_scr[row, :] * h + b_scr[row, :]
        h_scr[row, :] = h
        return h

    hc_ref[...] = lax.fori_loop(0, tt, step, hc_ref[...], unroll=8)
    o_ref[0] = (h_scr[...] * _gelu(y_ref[0])).astype(o_ref.dtype)

    @pl.when(t == n_t - 1)
    def _():
        hn_ref[0] = hc_ref[...]
        cn_ref[0] = x[tt - (CONV_W - 1):, :]


def _rglru(proj3, conv_state, h0, cw, cb, wa, ba, wx, bx, lam, l, *, tt):
    bsz, seq, _ = proj3.shape
    n_t = seq // tt
    col = lambda j: pl.BlockSpec((1, tt, D_GROUP), lambda b, t: (b, t, j))
    vec = pl.BlockSpec((DEPTH, D_GROUP), lambda b, t: (0, 0))
    sq = pl.BlockSpec((None, D_GROUP, D_GROUP), lambda b, t: (l, 0, 0))
    cs = pl.BlockSpec((1, CONV_W - 1, D_GROUP), lambda b, t: (b, 0, 0))
    hs = pl.BlockSpec((1, 1, D_GROUP), lambda b, t: (b, 0, 0))
    return pl.pallas_call(
        functools.partial(_lru_kernel, tt=tt, n_t=n_t, l=l),
        grid=(bsz, n_t),
        in_specs=[col(4), col(5), cs, hs,
                  pl.BlockSpec((None, CONV_W, D_GROUP), lambda b, t: (l, 0, 0)), vec, sq, vec, sq, vec, vec],
        out_specs=[pl.BlockSpec((1, tt, D_GROUP), lambda b, t: (b, t, 0)), hs, cs],
        out_shape=[jax.ShapeDtypeStruct((bsz, seq, D_GROUP), BF16),
                   jax.ShapeDtypeStruct((bsz, 1, D_GROUP), F32),
                   jax.ShapeDtypeStruct((bsz, CONV_W - 1, D_GROUP), F32)],
        scratch_shapes=[pltpu.VMEM((tt + PAD, D_GROUP), F32), pltpu.VMEM((tt, D_GROUP), F32),
                        pltpu.VMEM((tt, D_GROUP), F32), pltpu.VMEM((tt, D_GROUP), F32),
                        pltpu.VMEM((1, D_GROUP), F32)],
        compiler_params=_params("parallel", "arbitrary"),
        name="lru",
    )(proj3, proj3, conv_state, h0, cw, cb, wa, ba, wx, bx, lam)


def _sg_kernel(u_ref, v_ref, lg_ref, lb_ref, ws_ref, bs_ref, o_ref, *vn_ref, c, tt, bb, l):
    u = _gelu(u_ref[...])
    vn = _layer_norm(_gelu(v_ref[...]), lg_ref[l:l + 1, :], lb_ref[l:l + 1, :])
    if vn_ref:
        vn_ref[0][...] = vn
    row = lax.broadcasted_iota(jnp.int32, (c, c), 0)
    col = lax.broadcasted_iota(jnp.int32, (c, c), 1)
    mask = (col // CHUNK) <= (row // CHUNK)
    for h in range(N_HEADS):
        lanes = slice(h * D_HEAD, (h + 1) * D_HEAD)
        w = jnp.where(mask, ws_ref[h], 0.0).astype(BF16)
        bias = bs_ref[:, h:h + 1]
        for b in range(bb):
            for i in range(tt // c):
                rows = slice(i * c, (i + 1) * c)
                s = _mm(w, vn[b, rows, lanes]) + bias
                o_ref[b, rows, lanes] = (u[b, rows, lanes] * s).astype(o_ref.dtype)


def _spatial_gate(proj3, lg, lb, ws, bs_t, l, *, tt, bb, with_vn):
    bsz, seq, _ = proj3.shape
    c = min(SG_CHUNK, seq)
    col = lambda j: pl.BlockSpec((bb, tt, D_GROUP), lambda b, t: (b, t, j))
    vec = pl.BlockSpec((DEPTH, D_GROUP), lambda b, t: (0, 0))
    out = pl.BlockSpec((bb, tt, D_GROUP), lambda b, t: (b, t, 0))
    res = pl.pallas_call(
        functools.partial(_sg_kernel, c=c, tt=tt, bb=bb, l=l),
        grid=(bsz // bb, seq // tt),
        in_specs=[col(6), col(7), vec, vec,
                  pl.BlockSpec((None, N_HEADS, c, c), lambda b, t: (l, 0, 0, 0)),
                  pl.BlockSpec((None, c, N_HEADS), lambda b, t: (l, 0, 0))],
        out_specs=[out, out] if with_vn else [out],
        out_shape=[jax.ShapeDtypeStruct((bsz, seq, D_GROUP), BF16)]
                  + ([jax.ShapeDtypeStruct((bsz, seq, D_GROUP), F32)] if with_vn else []),
        compiler_params=_params("parallel", "parallel"),
        name="sg",
    )(proj3, proj3, lg, lb, ws, bs_t)
    return (res[0], res[1]) if with_vn else (res[0], None)


INV_BLOCK = 16


def _unit_lower_inverse(mats, c):
    row = lax.broadcasted_iota(jnp.int32, (c, c), 0)
    col = lax.broadcasted_iota(jnp.int32, (c, c), 1)
    eye = (row == col).astype(F32)
    same = row // INV_BLOCK == col // INV_BLOCK
    ps = [jnp.where(same, a, 0.0) for a in mats]
    xs = [eye - p for p in ps]
    ps = [_bmm(p, p) for p in ps]
    n_sq = int(math.log2(INV_BLOCK)) - 1
    for step in range(n_sq):
        last = step == n_sq - 1
        lhs = xs if last else [jnp.concatenate([x, p], axis=1) for x, p in zip(xs, ps)]
        prod = [_bmm(l, p) for l, p in zip(lhs, ps)]
        xs = [x + r[:, :c, :] for x, r in zip(xs, prod)]
        if not last:
            ps = [r[:, c:, :] for r in prod]
    size = INV_BLOCK
    while size < c:
        big = 2 * size
        sel = (row // big == col // big) & (row // size != col // size)
        offs = [jnp.where(sel, a, 0.0) for a in mats]
        tmp = [_bmm(o, x) for o, x in zip(offs, xs)]
        xs = [x - _bmm(x, t) for x, t in zip(xs, tmp)]
        size = big
    return xs


def _dn_kernel(q_ref, k_ref, v_ref, z_ref, ab_ref, cw_ref, al_ref, dt_ref, gn_ref, *rest,
               c, n, n_t, bb, has_state, l):
    lrow = slice(l, l + 1)
    if has_state:
        cs_ref, s0_ref = rest[:2]
        rest = rest[2:]
    o_ref, sf_ref, cn_ref, ext_ref, s_scr, u_scr, wq_scr, attn_scr, kdt_scr, egl_scr = rest
    t = pl.program_id(1)
    tt = n * c
    nb = bb * n
    heads = range(N_HEADS)

    @pl.when(t == 0)
    def _():
        for j in range(3):
            hist = (cs_ref[:, :, j * D_GROUP:(j + 1) * D_GROUP] if has_state
                    else jnp.zeros((bb, CONV_W - 1, D_GROUP), F32))
            ext_ref[j, :, PAD - (CONV_W - 1):PAD, :] = hist
        s_scr[...] = s0_ref[...] if has_state else jnp.zeros_like(s_scr)

    raw = (q_ref, k_ref, v_ref)
    qf, kf, vf = [
        jnp.concatenate([_silu(_conv_block(ext_ref.at[j, b], raw[j][b],
                                           cw_ref.at[:, j * D_GROUP:(j + 1) * D_GROUP], tt))
                         for b in range(bb)], axis=0)
        for j in range(3)]

    row = lax.broadcasted_iota(jnp.int32, (c, c), 0)
    col = lax.broadcasted_iota(jnp.int32, (c, c), 1)
    tri = row >= col
    strict = row > col
    tri_f = tri.astype(F32)
    scale = D_HEAD ** -0.5

    ab = ab_ref[...].reshape(bb * tt, 128)
    g_all = -jnp.exp(al_ref[lrow, :]) * jax.nn.softplus(ab + dt_ref[lrow, :])
    beta_all = jax.nn.sigmoid(ab)
    gc_all = jnp.concatenate(
        [jnp.dot(tri_f, g_all[i * c:(i + 1) * c], preferred_element_type=F32, precision=lax.Precision.HIGHEST)
         for i in range(nb)], axis=0)
    gc_t = gc_all.T

    def l2n(x):
        return x * lax.rsqrt(jnp.sum(x * x, -1, keepdims=True) + 1e-6)

    def split(x, h, k0, k1):
        return x[k0 * c:k1 * c, h * D_HEAD:(h + 1) * D_HEAD].reshape(k1 - k0, c, D_HEAD)

    def chunk_local(k0, k1):
        m = k1 - k0
        rows = slice(k0 * c, k1 * c)
        q3 = [l2n(split(qf, h, k0, k1)) * scale for h in heads]
        k3 = [l2n(split(kf, h, k0, k1)) for h in heads]
        v3 = [split(vf, h, k0, k1) for h in heads]
        beta = [beta_all[rows, N_HEADS + h:N_HEADS + h + 1].reshape(m, c, 1) for h in heads]
        gcol = [gc_all[rows, h:h + 1].reshape(m, c, 1) for h in heads]
        grow = [jnp.stack([gc_t[h:h + 1, i * c:(i + 1) * c] for i in range(k0, k1)]) for h in heads]
        glast = [g[:, c - 1:c, :] for g in gcol]
        decay = [jnp.where(tri, jnp.exp(gc - gr), 0.0) for gc, gr in zip(gcol, grow)]
        egc = [jnp.exp(g) for g in gcol]
        kb = [k * b for k, b in zip(k3, beta)]
        vb = [v * b for v, b in zip(v3, beta)]
        a_mat = [jnp.where(strict, _bmm_nt(x, k) * d, 0.0) for x, k, d in zip(kb, k3, decay)]
        tinv = _unit_lower_inverse(a_mat, c)
        uw = [_bmm(ti, jnp.concatenate([x, y * e], axis=-1)) for ti, x, y, e in zip(tinv, vb, kb, egc)]
        attn = [_bmm_nt(q, k) * d for q, k, d in zip(q3, k3, decay)]
        for h in heads:
            u_scr[h, k0:k1] = uw[h][:, :, :D_HEAD]
            wq_scr[h, k0:k1] = jnp.concatenate([uw[h][:, :, D_HEAD:], q3[h] * egc[h]], axis=1).astype(BF16)
            attn_scr[h, k0:k1] = attn[h].astype(BF16)
            kdt_scr[h, k0:k1] = jnp.swapaxes(k3[h] * jnp.exp(glast[h] - gcol[h]), 1, 2).astype(BF16)
            egl_scr[h, k0:k1] = jnp.broadcast_to(jnp.exp(glast[h]), (m, 1, D_HEAD))

    def sequential(k0, k1, state):
        for k in range(k0, k1):
            b, i = divmod(k, n)
            rows = slice(i * c, (i + 1) * c)
            s = [state[b, h] for h in heads]
            r = [_mm(wq_scr[h, k], s[h]) for h in heads]
            v_new = [u_scr[h, k] - r[h][:c] for h in heads]
            for h in heads:
                state[b, h] = s[h] * egl_scr[h, k] + _mm(kdt_scr[h, k], v_new[h])
            o = [r[h][c:] + _mm(attn_scr[h, k], v_new[h]) for h in heads]
            for h in heads:
                lanes = slice(h * D_HEAD, (h + 1) * D_HEAD)
                on = o[h] * lax.rsqrt(jnp.mean(o[h] * o[h], -1, keepdims=True) + EPS) * gn_ref[lrow, :]
                o_ref[b, rows, lanes] = (on * _silu(z_ref[b, rows, lanes])).astype(o_ref.dtype)

    groups = 2 if (n > 1 and nb % 2 == 0) else 1
    bounds = [(g * nb // groups, (g + 1) * nb // groups) for g in range(groups)]
    for k0, k1 in bounds:
        chunk_local(k0, k1)
    state = {(b, h): s_scr[b, h] for b in range(bb) for h in heads}
    for k0, k1 in bounds:
        sequential(k0, k1, state)
    for (b, h), s in state.items():
        s_scr[b, h] = s

    @pl.when(t == n_t - 1)
    def _():
        sf_ref[...] = s_scr[...]
        for j in range(3):
            cn_ref[:, :, j * D_GROUP:(j + 1) * D_GROUP] = raw[j][:, tt - (CONV_W - 1):, :]


def _deltanet(proj3, ab3, states, cw, a_log, dt_bias, gn, l, *, tt, bb):
    bsz, seq, _ = proj3.shape
    c = min(CHUNK, seq)
    n_t = seq // tt
    n = tt // c
    nb = bb * n
    col = lambda j: pl.BlockSpec((bb, tt, D_GROUP), lambda b, t: (b, t, j))
    st = pl.BlockSpec((bb, N_HEADS, D_HEAD, D_HEAD), lambda b, t: (b, 0, 0, 0))
    cs = pl.BlockSpec((bb, CONV_W - 1, 3 * D_GROUP), lambda b, t: (b, 0, 0))
    lane_row = pl.BlockSpec((DEPTH, 128), lambda b, t: (0, 0))
    has_state = states is not None
    state_in = list(states) if has_state else []
    state_spec = ([pl.BlockSpec((None, bb, CONV_W - 1, 3 * D_GROUP), lambda b, t: (l, b, 0, 0)),
                   pl.BlockSpec((None, bb, N_HEADS, D_HEAD, D_HEAD), lambda b, t: (l, b, 0, 0, 0))]
                  if has_state else [])
    return pl.pallas_call(
        functools.partial(_dn_kernel, c=c, n=n, n_t=n_t, bb=bb, has_state=has_state, l=l),
        grid=(bsz // bb, n_t),
        in_specs=[col(8), col(9), col(10), col(11),
                  pl.BlockSpec((bb, tt, 128), lambda b, t: (b, t, 0)),
                  pl.BlockSpec((None, CONV_W, 3 * D_GROUP), lambda b, t: (l, 0, 0)),
                  lane_row, lane_row, lane_row] + state_spec,
        out_specs=[pl.BlockSpec((bb, tt, D_GROUP), lambda b, t: (b, t, 0)), st, cs],
        out_shape=[jax.ShapeDtypeStruct((bsz, seq, D_GROUP), BF16),
                   jax.ShapeDtypeStruct((bsz, N_HEADS, D_HEAD, D_HEAD), F32),
                   jax.ShapeDtypeStruct((bsz, CONV_W - 1, 3 * D_GROUP), F32)],
        scratch_shapes=[pltpu.VMEM((3, bb, tt + PAD, D_GROUP), F32),
                        pltpu.VMEM((bb, N_HEADS, D_HEAD, D_HEAD), F32),
                        pltpu.VMEM((N_HEADS, nb, c, D_HEAD), F32),
                        pltpu.VMEM((N_HEADS, nb, 2 * c, D_HEAD), BF16),
                        pltpu.VMEM((N_HEADS, nb, c, c), BF16),
                        pltpu.VMEM((N_HEADS, nb, D_HEAD, c), BF16),
                        pltpu.VMEM((N_HEADS, nb, 1, D_HEAD), F32)],
        compiler_params=_params("parallel", "arbitrary"),
        name="dn",
    )(proj3, proj3, proj3, proj3, ab3, cw, a_log, dt_bias, gn, *state_in)


def _mixout_kernel(a_ref, b_ref, c_ref, d_ref, x_ref, w_ref, g_ref, bb_ref, o_ref, cat_ref, *, l):
    for j, part in enumerate((a_ref, b_ref, c_ref, d_ref)):
        cat_ref[:, j * D_GROUP:(j + 1) * D_GROUP] = part[...]
    n_row = 4 if o_ref.shape[0] >= 1024 else 2
    tr = o_ref.shape[0] // n_row
    for r in range(n_row):
        rows = slice(r * tr, (r + 1) * tr)
        acc = jnp.dot(cat_ref[rows, :], w_ref[...], preferred_element_type=F32)
        o_ref[rows, :] = _layer_norm(ALPHA * x_ref[rows, :] + acc, g_ref[l:l + 1, :], bb_ref[l:l + 1, :])


def _mixout(a, b, c, d, x, w, g, bb, l, *, tm):
    m, dm = x.shape
    part = pl.BlockSpec((tm, D_GROUP), lambda i: (i, 0))
    vec = pl.BlockSpec((DEPTH, dm), lambda i: (0, 0))
    return pl.pallas_call(
        functools.partial(_mixout_kernel, l=l),
        grid=(m // tm,),
        in_specs=[part, part, part, part,
                  pl.BlockSpec((tm, dm), lambda i: (i, 0)),
                  pl.BlockSpec((None, 4 * D_GROUP, dm), lambda i: (l, 0, 0), pipeline_mode=pl.Buffered(1)),
                  vec, vec],
        out_specs=pl.BlockSpec((tm, dm), lambda i: (i, 0)),
        out_shape=jax.ShapeDtypeStruct((m, dm), F32),
        scratch_shapes=[pltpu.VMEM((tm, 4 * D_GROUP), BF16)],
        compiler_params=_params("parallel"),
        name="mixout",
    )(a, b, c, d, x, w, g, bb)


def _block_diag(w):
    dep, h, n, _ = w.shape
    eye = jnp.eye(h, dtype=w.dtype)
    return (eye[None, :, None, :, None] * w[:, :, :, None, :]).reshape(dep, h * n, h * n)


def _prep(p):
    pad128 = lambda v: jnp.pad(v, ((0, 0), (0, 128 - v.shape[-1])))
    q = dict(p)
    q['w_mix_out'] = p['w_mix_out'].astype(BF16)
    w_in_t = jnp.swapaxes(p['w_mix_in'], 1, 2)
    q['w_mix_in'] = w_in_t.astype(BF16)
    q['w_mix_ab'] = jnp.pad(w_in_t[:, 12 * D_GROUP:, :], ((0, 0), (0, 128 - 2 * N_HEADS), (0, 0))).astype(BF16)
    q['lru_w_a'] = _block_diag(p['lru_w_a']).astype(BF16)
    q['lru_w_x'] = _block_diag(p['lru_w_x']).astype(BF16)
    q['dn_a_log'] = pad128(p['dn_a_log'])
    q['dn_dt_bias'] = pad128(p['dn_dt_bias'])
    return q


def _token_mixers(x, st, p, l, s):
    bsz, seq, tm, tt, bb = s['bsz'], s['seq'], s['tm'], s['tt'], s['bb']
    c_sg = min(SG_CHUNK, seq)
    sg_w = p['sg_w'][:, :, :c_sg, :c_sg]
    sg_b_t = jnp.swapaxes(p['sg_b'][:, :, :c_sg], 1, 2)
    proj, ab = _proj(x, p['w_mix_in'], p['w_mix_ab'], l, tm=tm)
    proj3 = proj.reshape(bsz, seq, -1)
    ab3 = ab.reshape(bsz, seq, 128)
    if st is None:
        s_ret = dn_states = None
        s_lru, s_lru_conv = jnp.zeros((bsz, D_GROUP), F32), jnp.zeros((bsz, CONV_W - 1, D_GROUP), F32)
    else:
        s_ret, s_lru, s_lru_conv, s_dn, s_dn_conv = st
        s_lru, s_lru_conv = s_lru[l], s_lru_conv[l]
        dn_states = (s_dn_conv, s_dn)
    tt_wide = min(seq, 2 * tt)
    out_a, ret_new = _retention(proj3, s_ret, p['ret_norm_g'], s['pos'], l, tt=tt_wide, bb=bb)
    out_b, lru_h_new, lru_conv_new = _rglru(
        proj3, s_lru_conv, s_lru.reshape(bsz, 1, D_GROUP), p['lru_conv_w'], p['lru_conv_b'],
        p['lru_w_a'], p['lru_b_a'], p['lru_w_x'], p['lru_b_x'], p['lru_lam'], l, tt=tt_wide)
    out_c, sg_v = _spatial_gate(proj3, p['sg_ln_g'], p['sg_ln_b'], sg_w, sg_b_t, l,
                                tt=tt_wide, bb=bb, with_vn=s['keep_sg_v'])
    out_d, dn_new, dn_conv_new = _deltanet(proj3, ab3, dn_states, p['dn_conv_w'],
                                           p['dn_a_log'], p['dn_dt_bias'], p['dn_norm_g'], l, tt=tt_wide, bb=bb)
    flat = lambda o: o.reshape(bsz * seq, D_GROUP)
    x = _mixout(flat(out_a), flat(out_b), flat(out_c), flat(out_d), x, p['w_mix_out'],
                p['ln2_g'], p['ln2_b'], l, tm=tm)
    return x, (ret_new, lru_h_new.reshape(bsz, D_GROUP), lru_conv_new, dn_new, dn_conv_new, sg_v)


def kernel(x_prompt, x_sample, state_ret, state_lru_h, state_lru_conv, state_dn, state_dn_conv, ffn1_w_in, ffn1_w_out, ln1_g, ln1_b, w_mix_in, ret_norm_g, lru_conv_w, lru_conv_b, lru_w_a, lru_b_a, lru_w_x, lru_b_x, lru_lam, sg_ln_g, sg_ln_b, sg_w, sg_b, dn_conv_w, dn_a_log, dn_dt_bias, dn_norm_g, w_mix_out, ln2_g, ln2_b, ffn2_w_in, ffn2_w_out, ln3_g, ln3_b):
    weights = dict(ffn1_w_in=ffn1_w_in, ffn1_w_out=ffn1_w_out, ln1_g=ln1_g, ln1_b=ln1_b, w_mix_in=w_mix_in,
                   ret_norm_g=ret_norm_g, lru_conv_w=lru_conv_w, lru_conv_b=lru_conv_b, lru_w_a=lru_w_a,
                   lru_b_a=lru_b_a, lru_w_x=lru_w_x, lru_b_x=lru_b_x, lru_lam=lru_lam, sg_ln_g=sg_ln_g,
                   sg_ln_b=sg_ln_b, sg_w=sg_w, sg_b=sg_b, dn_conv_w=dn_conv_w, dn_a_log=dn_a_log,
                   dn_dt_bias=dn_dt_bias, dn_norm_g=dn_norm_g, w_mix_out=w_mix_out, ln2_g=ln2_g, ln2_b=ln2_b,
                   ffn2_w_in=ffn2_w_in, ffn2_w_out=ffn2_w_out, ln3_g=ln3_g, ln3_b=ln3_b)
    p = _prep(weights)

    bp, lp, dm = x_prompt.shape
    bs, ls, _ = x_sample.shape
    prompt = dict(bsz=bp, seq=lp, tm=1024, tt=min(lp, 512), bb=1, pos=np.arange(lp), keep_sg_v=False)
    sample = dict(bsz=bs, seq=ls, tm=bs * ls, tt=ls, bb=min(bs, 8), pos=PAST_LEN + np.arange(ls), keep_sg_v=True)
    sample_states = (state_ret, state_lru_h, state_lru_conv, state_dn, state_dn_conv)
    xp = x_prompt.reshape(bp * lp, dm)
    xs = x_sample.reshape(bs * ls, dm)

    ffns = [(l, name) for l in range(DEPTH) for name in ('ffn1', 'ffn2')]
    ln_of = {'ffn1': ('ln1_g', 'ln1_b'), 'ffn2': ('ln3_g', 'ln3_b')}

    def ffn_pair(k, xp, xs, wb):
        l, name = ffns[k]
        g, b = p[ln_of[name][0]], p[ln_of[name][1]]
        nxt = None
        if k + 1 < len(ffns):
            ln, nname = ffns[k + 1]
            nxt = (weights[nname + '_w_in'], weights[nname + '_w_out'], ln)
        if wb is None:
            xs, wb = _ffn(xs, (weights[name + '_w_in'], weights[name + '_w_out']), g, b, l, tm=sample['tm'],
                          from_f32=True)
        else:
            xs, _ = _ffn(xs, wb, g, b, l, tm=sample['tm'])
        xp, wb_next = _ffn(xp, wb, g, b, l, tm=prompt['tm'], cast_next=nxt)
        return xp, xs, wb_next

    wb = None

    st_p, st_s = [], []
    for l in range(DEPTH):
        xp, xs, wb = ffn_pair(2 * l, xp, xs, wb)
        xp, new_p = _token_mixers(xp, None, p, l, prompt)
        xs, new_s = _token_mixers(xs, sample_states, p, l, sample)
        xp, xs, wb = ffn_pair(2 * l + 1, xp, xs, wb)
        st_p.append(new_p)
        st_s.append(new_s)

    stack = lambda sts, i: jnp.stack([s[i] for s in sts])
    return (xp.reshape(bp, lp, dm), xs.reshape(bs, ls, dm),
            stack(st_p, 0), stack(st_p, 1), stack(st_p, 2), stack(st_p, 3), stack(st_p, 4),
            stack(st_s, 0), stack(st_s, 1), stack(st_s, 2), stack(st_s, 3), stack(st_s, 4), stack(st_s, 5))
```
